```python
import numpy as np
import jax
import jax.numpy as jnp
from jax import lax

D_MODEL = 4096
BATCH = 4
SEQ = 2048
DEPTH = 2
DEC_BATCH = 8
DEC_SEQ = 1
PAST_LEN = 16384
PAGE_SIZE = 128

N_HEADS = 16
HEAD_DIM = 128
N_KV_HEADS = 4
Q_PER_KV = N_HEADS // N_KV_HEADS
NSA_W = N_HEADS * HEAD_DIM
KV_W = N_KV_HEADS * HEAD_DIM
CMP_BLK = 32
CMP_STRIDE = 16
SEL_BLK = 64
N_SEL = 16
WINDOW = 512
BAND_BLK = 128
SEL_QBLK = 64
SCALE = HEAD_DIM ** -0.5
CHUNK = 128
MLP_W = D_MODEL // 4
MLP_GROUPS = 8
MLP_GW = MLP_W // MLP_GROUPS
LRU_W = D_MODEL // 4
LRU_HEADS = 8
LRU_BH = LRU_W // LRU_HEADS
CONV_W = 4
LRU_C = 8.0
N_BRANCH = 3
EPS = 1e-6
BIG = 1e9
NEG = -1e30
IN_SIZES = (NSA_W, KV_W, KV_W, KV_W, KV_W, KV_W, KV_W, 3 * N_HEADS, NSA_W,
            MLP_W, MLP_W, MLP_W, LRU_W, LRU_W, N_BRANCH * D_MODEL)
N_IN = sum(IN_SIZES)

kernel_name = 'hybrid_nsa_gmlp_rglru_step'


def _split_in(z):
    cuts, acc = [], 0
    for size in IN_SIZES[:-1]:
        acc += size
        cuts.append(acc)
    return jnp.split(z, cuts, axis=-1)


def rms_norm(x, g):
    xf = x.astype(jnp.float32)
    y = xf * lax.rsqrt(jnp.mean(xf * xf, axis=-1, keepdims=True) + EPS)
    return (y * g.astype(jnp.float32)).astype(x.dtype)


def layer_norm(x, g, b):
    xf = x.astype(jnp.float32)
    xc = xf - jnp.mean(xf, axis=-1, keepdims=True)
    var = jnp.mean(xc * xc, axis=-1, keepdims=True)
    return (xc * lax.rsqrt(var + EPS) * g.astype(jnp.float32) + b.astype(jnp.float32)).astype(x.dtype)


def masked_softmax(s, mask):
    p = jax.nn.softmax(jnp.where(mask, s, NEG), axis=-1)
    return jnp.where(mask, p, 0.0)


def compress(k, pe, w):
    b, l = k.shape[:2]
    nsub = l // CMP_STRIDE
    r = CMP_BLK // CMP_STRIDE
    nc = nsub - r + 1
    sub = k[:, :nsub * CMP_STRIDE].reshape(b, nsub, CMP_STRIDE, N_KV_HEADS, HEAD_DIM)
    blocks = jnp.concatenate([sub[:, i:i + nc] for i in range(r)], axis=2)
    blocks = blocks + pe[:, None, :]
    return jnp.einsum('bnlgd,lde->bnge', blocks, w)


def cmp_to_sel(nc, ns):
    c0 = np.arange(nc) * CMP_STRIDE
    s0 = np.arange(ns) * SEL_BLK
    m = (c0[:, None] < s0[None, :] + SEL_BLK) & (c0[:, None] + CMP_BLK > s0[None, :])
    return jnp.asarray(m.astype(np.float32))


def cmp_branch(q, k_raw, v_raw, t_pos, pe_k, pe_v, w_k, w_v, g_k):
    kc = rms_norm(compress(k_raw, pe_k, w_k), g_k)
    vc = compress(v_raw, pe_v, w_v)
    end = jnp.arange(kc.shape[1]) * CMP_STRIDE + (CMP_BLK - 1)
    mask = (end[None, :] <= t_pos[:, None])[None, :, None, None, :]
    s = jnp.einsum('btgqd,bngd->btgqn', q, kc).astype(jnp.float32) * SCALE
    p = masked_softmax(s, mask)
    o = jnp.einsum('btgqn,bngd->btgqd', p.astype(vc.dtype), vc)
    return o, p


def select_blocks(p_cmp, t_pos, ns):
    imp = jnp.einsum('btgn,ns->btgs', p_cmp.sum(axis=3), cmp_to_sel(p_cmp.shape[-1], ns))
    blk = jnp.arange(ns)[None, :]
    cur = (t_pos // SEL_BLK)[:, None]
    forced = (blk == 0) | (blk == cur) | (blk == cur - 1)
    future = blk > cur
    imp = jnp.where(future[None, :, None, :], -BIG, jnp.where(forced[None, :, None, :], BIG, imp))
    _, idx = lax.top_k(imp, min(N_SEL, ns))
    return idx


def sel_branch(q, k, v, idx, t_pos):
    b, l = k.shape[:2]
    ns = -(-l // SEL_BLK)
    pad = ((0, 0), (0, ns * SEL_BLK - l), (0, 0), (0, 0))
    kb = jnp.pad(k, pad).reshape(b, ns, SEL_BLK, N_KV_HEADS, HEAD_DIM).transpose(0, 3, 1, 2, 4)
    vb = jnp.pad(v, pad).reshape(b, ns, SEL_BLK, N_KV_HEADS, HEAD_DIM).transpose(0, 3, 1, 2, 4)
    t = q.shape[1]
    qb = SEL_QBLK if t % SEL_QBLK == 0 else t
    nq = t // qb
    n = idx.shape[-1]
    bi = jnp.arange(b)[:, None, None, None]
    gi = jnp.arange(N_KV_HEADS)[None, None, :, None]

    def blockify(a):
        return a.reshape((b, nq, qb) + a.shape[2:]).swapaxes(0, 1)

    def one_block(args):
        qq, ii, tt = args
        kg = kb[bi, gi, ii]
        vg = vb[bi, gi, ii]
        s = jnp.einsum('btgqd,btgnld->btgqnl', qq, kg).astype(jnp.float32) * SCALE
        kpos = ii[..., None] * SEL_BLK + jnp.arange(SEL_BLK)
        mask = (kpos <= tt[None, :, None, None, None]).reshape(b, qb, N_KV_HEADS, 1, n * SEL_BLK)
        p = masked_softmax(s.reshape(b, qb, N_KV_HEADS, Q_PER_KV, n * SEL_BLK), mask)
        p = p.reshape(b, qb, N_KV_HEADS, Q_PER_KV, n, SEL_BLK)
        return jnp.einsum('btgqnl,btgnld->btgqd', p.astype(vg.dtype), vg)

    o = lax.map(one_block, (blockify(q), blockify(idx), t_pos.reshape(nq, qb)))
    return o.swapaxes(0, 1).reshape(b, t, N_KV_HEADS, Q_PER_KV, HEAD_DIM)


def win_branch_band(q, k, v):
    b, s = k.shape[:2]
    nb = s // BAND_BLK
    nprev = WINDOW // BAND_BLK
    nk = (nprev + 1) * BAND_BLK
    pad = ((0, 0), (WINDOW, 0), (0, 0), (0, 0))
    kp = jnp.pad(k, pad).reshape(b, nb + nprev, BAND_BLK, N_KV_HEADS, HEAD_DIM)
    vp = jnp.pad(v, pad).reshape(b, nb + nprev, BAND_BLK, N_KV_HEADS, HEAD_DIM)
    kband = jnp.concatenate([kp[:, i:i + nb] for i in range(nprev + 1)], axis=2)
    vband = jnp.concatenate([vp[:, i:i + nb] for i in range(nprev + 1)], axis=2)
    qq = q.reshape(b, nb, BAND_BLK, N_KV_HEADS, Q_PER_KV, HEAD_DIM)
    sc = jnp.einsum('bntgqd,bnkgd->bntgqk', qq, kband).astype(jnp.float32) * SCALE
    t_pos = jnp.arange(s).reshape(nb, BAND_BLK)
    k_pos = jnp.arange(nb)[:, None] * BAND_BLK - WINDOW + jnp.arange(nk)[None, :]
    diff = t_pos[:, :, None] - k_pos[:, None, :]
    mask = (diff >= 0) & (diff <= WINDOW) & (k_pos[:, None, :] >= 0)
    p = masked_softmax(sc, mask[None, :, :, None, None, :])
    o = jnp.einsum('bntgqk,bnkgd->bntgqd', p.astype(vband.dtype), vband)
    return o.reshape(b, s, N_KV_HEADS, Q_PER_KV, HEAD_DIM)


def win_branch_dense(q, k, v, t_pos, k_pos):
    sc = jnp.einsum('btgqd,bkgd->btgqk', q, k).astype(jnp.float32) * SCALE
    diff = t_pos[:, None] - k_pos[None, :]
    mask = ((diff >= 0) & (diff <= WINDOW))[None, :, None, None, :]
    p = masked_softmax(sc, mask)
    return jnp.einsum('btgqk,bkgd->btgqd', p.astype(v.dtype), v)


def chunk_mlp(u, v, ln_g, ln_b, w_s, b_s):
    b, t, _ = u.shape
    u = jax.nn.gelu(u)
    v = layer_norm(jax.nn.gelu(v), ln_g, ln_b)
    c = min(CHUNK, t)
    nch = t // c
    ws = w_s[:, :c, :c] * jnp.tril(jnp.ones((c, c), w_s.dtype))
    vh = v.reshape(b, nch, c, MLP_GROUPS, MLP_GW)
    mixed = jnp.einsum('gij,bnjgc->bnigc', ws, vh) + b_s[:, :c].T[None, None, :, :, None]
    return u * mixed.reshape(b, t, MLP_W), v


def causal_conv(x, buf, w, bias):
    t = x.shape[1]
    xp = jnp.concatenate([buf.astype(x.dtype), x], axis=1)
    y = bias + xp[:, 0:t] * w[0]
    for j in range(1, CONV_W):
        y = y + xp[:, j:j + t] * w[j]
    return y, xp[:, -(CONV_W - 1):]


def rg_lru(xc, h0, wa, ba, wx, bx, lam):
    b, t, _ = xc.shape
    xh = xc.reshape(b, t, LRU_HEADS, LRU_BH)
    r = jax.nn.sigmoid((jnp.einsum('bthi,hij->bthj', xh, wa).reshape(b, t, LRU_W) + ba).astype(jnp.float32))
    i = jax.nn.sigmoid((jnp.einsum('bthi,hij->bthj', xh, wx).reshape(b, t, LRU_W) + bx).astype(jnp.float32))
    log_a = -LRU_C * r * jax.nn.softplus(-lam.astype(jnp.float32))
    a = jnp.exp(log_a)
    inp = jnp.sqrt(-jnp.expm1(2.0 * log_a)) * (i * xc.astype(jnp.float32))

    def step(h, au):
        a_t, u_t = au
        h = a_t * h + u_t
        return h, h

    h_last, hs = lax.scan(step, h0, (a.swapaxes(0, 1), inp.swapaxes(0, 1)))
    return hs.swapaxes(0, 1).astype(xc.dtype), h_last


def mixer_layer(x, lw, past, w_buf):
    b, t, _ = x.shape
    h = rms_norm(x, lw['norm_g'])
    z = h @ lw['w_in']
    (q, kc, vc, ks, vs, kw, vw, g_nsa, gate_nsa, u, v, gate_mlp, xl, gate_lru, g_merge) = _split_in(z)

    def kvh(a):
        return a.reshape(b, t, N_KV_HEADS, HEAD_DIM)

    q = rms_norm(q.reshape(b, t, N_KV_HEADS, Q_PER_KV, HEAD_DIM), lw['q_norm_g'])
    kc, vc, vs, vw = kvh(kc), kvh(vc), kvh(vs), kvh(vw)
    ks = rms_norm(kvh(ks), lw['k_norm_g'][1])
    kw = rms_norm(kvh(kw), lw['k_norm_g'][2])

    if past is None:
        t_pos = jnp.arange(t)
        kc_all, vc_all, ks_all, vs_all = kc, vc, ks, vs
    else:
        t_pos = PAST_LEN + jnp.arange(t)
        kc_all = jnp.concatenate([past['cmp_k'], kc], axis=1)
        vc_all = jnp.concatenate([past['cmp_v'], vc], axis=1)
        ks_all = jnp.concatenate([past['sel_k'], ks], axis=1)
        vs_all = jnp.concatenate([past['sel_v'], vs], axis=1)

    o_cmp, p_cmp = cmp_branch(q, kc_all, vc_all, t_pos, lw['cmp_pe_k'], lw['cmp_pe_v'],
                              lw['w_cmp_k'], lw['w_cmp_v'], lw['k_norm_g'][0])
    idx = select_blocks(p_cmp, t_pos, -(-ks_all.shape[1] // SEL_BLK))
    o_sel = sel_branch(q, ks_all, vs_all, idx, t_pos)
    if past is None:
        o_win = win_branch_band(q, kw, vw)
        fpad = ((0, 0), (max(w_buf - t, 0), 0), (0, 0), (0, 0))
        win_k_new = jnp.pad(kw, fpad)[:, -w_buf:]
        win_v_new = jnp.pad(vw, fpad)[:, -w_buf:]
    else:
        kwin = jnp.concatenate([past['win_k'], kw], axis=1)
        vwin = jnp.concatenate([past['win_v'], vw], axis=1)
        k_pos = PAST_LEN - w_buf + jnp.arange(w_buf + t)
        o_win = win_branch_dense(q, kwin, vwin, t_pos, k_pos)
        win_k_new = kwin[:, -w_buf:]
        win_v_new = vwin[:, -w_buf:]
    g = jax.nn.sigmoid(g_nsa.astype(jnp.float32)).astype(x.dtype).reshape(b, t, 3, N_KV_HEADS, Q_PER_KV, 1)
    o_nsa = (g[:, :, 0] * o_cmp + g[:, :, 1] * o_sel + g[:, :, 2] * o_win).reshape(b, t, NSA_W)
    o_nsa = o_nsa * jax.nn.silu(gate_nsa)

    o_mlp, v_rows = chunk_mlp(u, v, lw['mlp_ln_g'], lw['mlp_ln_b'], lw['w_s'], lw['b_s'])
    o_mlp = o_mlp * jax.nn.silu(gate_mlp)

    if past is None:
        buf = jnp.zeros((b, CONV_W - 1, LRU_W), x.dtype)
        h0 = jnp.zeros((b, LRU_W), jnp.float32)
    else:
        buf = past['lru_conv']
        h0 = past['lru_h'].astype(jnp.float32)
    xconv, conv_new = causal_conv(xl, buf, lw['conv_w'], lw['conv_b'])
    o_lru, h_new = rg_lru(xconv, h0, lw['lru_wa'], lw['lru_ba'], lw['lru_wx'], lw['lru_bx'], lw['lru_lambda'])
    o_lru = o_lru * jax.nn.silu(gate_lru)

    gm = jax.nn.sigmoid(g_merge.astype(jnp.float32)).astype(x.dtype).reshape(b, t, N_BRANCH, D_MODEL)
    m = (gm[:, :, 0] * (o_nsa @ lw['w_br_nsa'])
         + gm[:, :, 1] * (o_mlp @ lw['w_br_mlp'])
         + gm[:, :, 2] * (o_lru @ lw['w_br_lru']))
    y = x + m @ lw['w_out']
    state = {'cmp_k': kc, 'cmp_v': vc, 'sel_k': ks, 'sel_v': vs,
             'win_k': win_k_new, 'win_v': win_v_new,
             'lru_h': h_new.astype(x.dtype), 'lru_conv': conv_new, 'mlp_v': v_rows}
    return y, state


def setup_inputs(seed: int = 0) -> dict:
    key = jax.random.key(seed)
    kit = iter(list(jax.random.split(key, 48)))

    def nrm(shape, scale):
        return jax.random.normal(next(kit), shape, jnp.float32) * scale

    n_pages = PAST_LEN // PAGE_SIZE
    n_used = DEC_BATCH * n_pages
    n_pool = n_used + max(1, n_used // 4)
    w_buf = min(WINDOW, PAST_LEN)
    page_table = jax.random.permutation(next(kit), n_pool)[:n_used].reshape(DEC_BATCH, n_pages).astype(jnp.int32)
    pool = (DEPTH, n_pool, PAGE_SIZE, N_KV_HEADS, HEAD_DIM)
    a_pow = jax.random.uniform(next(kit), (DEPTH, LRU_W), jnp.float32, 0.9, 0.999)
    a_base = a_pow ** (1.0 / LRU_C)
    lam = jnp.log(a_base) - jnp.log1p(-a_base)
    return {
        'x_prompt': nrm((BATCH, SEQ, D_MODEL), 1.0),
        'x_sample': nrm((DEC_BATCH, DEC_SEQ, D_MODEL), 1.0),
        'cache_cmp_k': nrm(pool, 1.0),
        'cache_cmp_v': nrm(pool, 1.0),
        'cache_sel_k': nrm(pool, 1.0),
        'cache_sel_v': nrm(pool, 1.0),
        'state_win_k': nrm((DEPTH, DEC_BATCH, w_buf, N_KV_HEADS, HEAD_DIM), 1.0),
        'state_win_v': nrm((DEPTH, DEC_BATCH, w_buf, N_KV_HEADS, HEAD_DIM), 1.0),
        'state_lru_h': nrm((DEPTH, DEC_BATCH, LRU_W), 0.5),
        'state_lru_conv': nrm((DEPTH, DEC_BATCH, CONV_W - 1, LRU_W), 1.0),
        'page_table': page_table,
        'norm_g': 1.0 + nrm((DEPTH, D_MODEL), 0.02),
        'w_in': nrm((DEPTH, D_MODEL, N_IN), D_MODEL ** -0.5),
        'q_norm_g': 1.0 + nrm((DEPTH, HEAD_DIM), 0.02),
        'k_norm_g': 1.0 + nrm((DEPTH, 3, HEAD_DIM), 0.02),
        'cmp_pe_k': nrm((DEPTH, CMP_BLK, HEAD_DIM), 0.1),
        'cmp_pe_v': nrm((DEPTH, CMP_BLK, HEAD_DIM), 0.1),
        'w_cmp_k': nrm((DEPTH, CMP_BLK, HEAD_DIM, HEAD_DIM), (CMP_BLK * HEAD_DIM) ** -0.5),
        'w_cmp_v': nrm((DEPTH, CMP_BLK, HEAD_DIM, HEAD_DIM), (CMP_BLK * HEAD_DIM) ** -0.5),
        'mlp_ln_g': 1.0 + nrm((DEPTH, MLP_W), 0.02),
        'mlp_ln_b': nrm((DEPTH, MLP_W), 0.02),
        'w_s': nrm((DEPTH, MLP_GROUPS, CHUNK, CHUNK), CHUNK ** -0.5),
        'b_s': 1.0 + nrm((DEPTH, MLP_GROUPS, CHUNK), 0.02),
        'conv_w': nrm((DEPTH, CONV_W, LRU_W), CONV_W ** -0.5),
        'conv_b': nrm((DEPTH, LRU_W), 0.02),
        'lru_wa': nrm((DEPTH, LRU_HEADS, LRU_BH, LRU_BH), LRU_BH ** -0.5),
        'lru_ba': nrm((DEPTH, LRU_W), 0.1),
        'lru_wx': nrm((DEPTH, LRU_HEADS, LRU_BH, LRU_BH), LRU_BH ** -0.5),
        'lru_bx': nrm((DEPTH, LRU_W), 0.1),
        'lru_lambda': lam,
        'w_br_nsa': nrm((DEPTH, NSA_W, D_MODEL), NSA_W ** -0.5),
        'w_br_mlp': nrm((DEPTH, MLP_W, D_MODEL), MLP_W ** -0.5),
        'w_br_lru': nrm((DEPTH, LRU_W, D_MODEL), LRU_W ** -0.5),
        'w_out': nrm((DEPTH, D_MODEL, D_MODEL), D_MODEL ** -0.5),
    }


def reference(x_prompt, x_sample, cache_cmp_k, cache_cmp_v, cache_sel_k, cache_sel_v,
              state_win_k, state_win_v, state_lru_h, state_lru_conv, page_table,
              norm_g, w_in, q_norm_g, k_norm_g, cmp_pe_k, cmp_pe_v, w_cmp_k, w_cmp_v,
              mlp_ln_g, mlp_ln_b, w_s, b_s, conv_w, conv_b, lru_wa, lru_ba, lru_wx, lru_bx,
              lru_lambda, w_br_nsa, w_br_mlp, w_br_lru, w_out):
    n_pages = PAST_LEN // PAGE_SIZE
    w_buf = state_win_k.shape[2]
    n_seq = page_table.shape[0]

    def gather(pool):
        return pool[page_table].reshape(n_seq, n_pages * PAGE_SIZE, N_KV_HEADS, HEAD_DIM)

    yp, ys = x_prompt, x_sample
    p_st, s_st = [], []
    for l in range(DEPTH):
        lw = {'norm_g': norm_g[l], 'w_in': w_in[l], 'q_norm_g': q_norm_g[l], 'k_norm_g': k_norm_g[l],
              'cmp_pe_k': cmp_pe_k[l], 'cmp_pe_v': cmp_pe_v[l], 'w_cmp_k': w_cmp_k[l], 'w_cmp_v': w_cmp_v[l],
              'mlp_ln_g': mlp_ln_g[l], 'mlp_ln_b': mlp_ln_b[l], 'w_s': w_s[l], 'b_s': b_s[l],
              'conv_w': conv_w[l], 'conv_b': conv_b[l], 'lru_wa': lru_wa[l], 'lru_ba': lru_ba[l],
              'lru_wx': lru_wx[l], 'lru_bx': lru_bx[l], 'lru_lambda': lru_lambda[l],
              'w_br_nsa': w_br_nsa[l], 'w_br_mlp': w_br_mlp[l], 'w_br_lru': w_br_lru[l], 'w_out': w_out[l]}
        yp, sp = mixer_layer(yp, lw, None, w_buf)
        past = {'cmp_k': gather(cache_cmp_k[l]), 'cmp_v': gather(cache_cmp_v[l]),
                'sel_k': gather(cache_sel_k[l]), 'sel_v': gather(cache_sel_v[l]),
                'win_k': state_win_k[l], 'win_v': state_win_v[l],
                'lru_h': state_lru_h[l], 'lru_conv': state_lru_conv[l]}
        ys, ss = mixer_layer(ys, lw, past, w_buf)
        p_st.append(sp)
        s_st.append(ss)

    def stk(states, name):
        return jnp.stack([st[name] for st in states])

    return (yp, ys,
            stk(p_st, 'cmp_k'), stk(p_st, 'cmp_v'), stk(p_st, 'sel_k'), stk(p_st, 'sel_v'),
            stk(p_st, 'win_k'), stk(p_st, 'win_v'), stk(p_st, 'lru_h'), stk(p_st, 'lru_conv'),
            stk(s_st, 'cmp_k'), stk(s_st, 'cmp_v'), stk(s_st, 'sel_k'), stk(s_st, 'sel_v'),
            stk(s_st, 'win_k'), stk(s_st, 'win_v'), stk(s_st, 'lru_h'), stk(s_st, 'lru_conv'),
            stk(s_st, 'mlp_v'))
```

```python
import functools

import numpy as np
import jax
import jax.numpy as jnp
from jax import lax
from jax.experimental import pallas as pl
from jax.experimental.pallas import tpu as pltpu

N_HEADS = 16
HEAD_DIM = 128
N_KV_HEADS = 4
Q_PER_KV = N_HEADS // N_KV_HEADS
NSA_W = N_HEADS * HEAD_DIM
KV_W = N_KV_HEADS * HEAD_DIM
CMP_BLK = 32
CMP_STRIDE = 16
SEL_BLK = 64
N_SEL = 16
WINDOW = 512
CHUNK = 128
MLP_GROUPS = 8
LRU_HEADS = 8
CONV_W = 4
LRU_C = 8.0
N_BRANCH = 3
EPS = 1e-6
BIG = 1e9
NEG = -1e30
SCALE = HEAD_DIM ** -0.5

LANES = 128
VMEM_LIMIT = 56 * 1024 * 1024

BF16 = jnp.bfloat16
F32 = jnp.float32


def _cparams(sem):
    return pltpu.CompilerParams(dimension_semantics=sem, vmem_limit_bytes=VMEM_LIMIT)


def _gelu(x):
    return 0.5 * x * (1.0 + jnp.tanh(0.7978845608028654 * (x + 0.044715 * (x * x * x))))


def _sigmoid(x):
    return 1.0 / (1.0 + jnp.exp(-x))


def _silu(x):
    return x * _sigmoid(x)


def _dot(a, b):
    return jnp.dot(a, b, preferred_element_type=F32)


def _dot_nt(a, b):
    return lax.dot_general(a, b, (((1,), (1,)), ((), ())), preferred_element_type=F32)


def _rms_kernel(x_ref, g_ref, o_ref):
    x = x_ref[...]
    ms = jnp.mean(x * x, axis=-1, keepdims=True)
    o_ref[...] = ((x * lax.rsqrt(ms + EPS)) * g_ref[...]).astype(o_ref.dtype)


def rms_rows(x, g, tm):
    m, d = x.shape
    return pl.pallas_call(
        _rms_kernel,
        grid=(m // tm,),
        in_specs=[pl.BlockSpec((tm, d), lambda i: (i, 0)),
                  pl.BlockSpec((1, d), lambda i: (0, 0))],
        out_specs=pl.BlockSpec((tm, d), lambda i: (i, 0)),
        out_shape=jax.ShapeDtypeStruct((m, d), BF16),
        compiler_params=_cparams(("parallel",)),
        name="rms_rows",
    )(x, g.reshape(1, d))


def _proj_kernel(a_ref, w_ref, *rest, act):
    acc = _dot(a_ref[...], w_ref[...])
    if act == "norm":
        gain_ref, flag_ref, of_ref, ob_ref = rest
        tn = acc.shape[1]
        for c in range(tn // LANES):
            sl = slice(c * LANES, (c + 1) * LANES)
            z = acc[:, sl]
            ms = jnp.mean(z * z, axis=-1, keepdims=True)
            zn = (z * lax.rsqrt(ms + EPS)) * gain_ref[:, sl]
            z = jnp.where(flag_ref[:, sl] > 0.5, zn, z)
            of_ref[:, sl] = z
            ob_ref[:, sl] = z.astype(ob_ref.dtype)
        return
    (o_ref,) = rest
    if act == "silu":
        acc = _silu(acc)
    elif act == "gelu":
        acc = _gelu(acc)
    elif act == "sigmoid":
        acc = _sigmoid(acc)
    o_ref[...] = acc.astype(o_ref.dtype)


def proj(a, w, col0, ncols, act, out_dtype, tm, tn, gain=None, flag=None):
    m, k = a.shape
    assert col0 % tn == 0 and ncols % tn == 0 and m % tm == 0
    jb = col0 // tn
    in_specs = [pl.BlockSpec((tm, k), lambda i, j: (i, 0)),
                pl.BlockSpec((k, tn), lambda i, j: (0, jb + j))]
    args = [a, w]
    if act == "norm":
        in_specs += [pl.BlockSpec((1, tn), lambda i, j: (0, j))] * 2
        args += [gain, flag]
        out_specs = [pl.BlockSpec((tm, tn), lambda i, j: (i, j))] * 2
        out_shape = [jax.ShapeDtypeStruct((m, ncols), F32), jax.ShapeDtypeStruct((m, ncols), BF16)]
    else:
        out_specs = pl.BlockSpec((tm, tn), lambda i, j: (i, j))
        out_shape = jax.ShapeDtypeStruct((m, ncols), out_dtype)
    return pl.pallas_call(
        functools.partial(_proj_kernel, act=act),
        grid=(m // tm, ncols // tn),
        in_specs=in_specs,
        out_specs=out_specs,
        out_shape=out_shape,
        compiler_params=_cparams(("parallel", "arbitrary")),
        name="proj_" + act,
    )(*args)


def _merge_kernel(a1, a2, a3, w1, w2, w3, g1, g2, g3, o_ref):
    m = g1[...].astype(F32) * _dot(a1[...], w1[...])
    m = m + g2[...].astype(F32) * _dot(a2[...], w2[...])
    m = m + g3[...].astype(F32) * _dot(a3[...], w3[...])
    o_ref[...] = m.astype(o_ref.dtype)


def merge(a_list, w_list, gm, tm, tn):
    m = a_list[0].shape[0]
    n = w_list[0].shape[1]
    nb = n // tn
    in_specs = [pl.BlockSpec((tm, a.shape[1]), lambda i, j: (i, 0)) for a in a_list]
    in_specs += [pl.BlockSpec((w.shape[0], tn), lambda i, j: (0, j)) for w in w_list]
    in_specs += [pl.BlockSpec((tm, tn), functools.partial(lambda i, j, b: (i, b * nb + j), b=b))
                 for b in range(N_BRANCH)]
    return pl.pallas_call(
        _merge_kernel,
        grid=(m // tm, nb),
        in_specs=in_specs,
        out_specs=pl.BlockSpec((tm, tn), lambda i, j: (i, j)),
        out_shape=jax.ShapeDtypeStruct((m, n), BF16),
        compiler_params=_cparams(("parallel", "arbitrary")),
        name="merge",
    )(*a_list, *w_list, gm, gm, gm)


def _resid_kernel(a_ref, w_ref, x_ref, o_ref):
    o_ref[...] = x_ref[...] + _dot(a_ref[...], w_ref[...])


def out_proj(a, w, x, tm, tn):
    m, k = a.shape
    n = w.shape[1]
    return pl.pallas_call(
        _resid_kernel,
        grid=(m // tm, n // tn),
        in_specs=[pl.BlockSpec((tm, k), lambda i, j: (i, 0)),
                  pl.BlockSpec((k, tn), lambda i, j: (0, j)),
                  pl.BlockSpec((tm, tn), lambda i, j: (i, j))],
        out_specs=pl.BlockSpec((tm, tn), lambda i, j: (i, j)),
        out_shape=jax.ShapeDtypeStruct((m, n), F32),
        compiler_params=_cparams(("parallel", "arbitrary")),
        name="out_proj",
    )(a, w, x)


def _softmax_rows(s, mask):
    s = jnp.where(mask, s, NEG)
    m = jnp.max(s, axis=-1, keepdims=True)
    e = jnp.where(mask, jnp.exp(s - m), 0.0)
    d = jnp.sum(e, axis=-1, keepdims=True)
    return e / jnp.where(d > 0.0, d, 1.0)


def _compress_combine(acc, const, nvalid):
    nsub = acc.shape[0]
    top = acc[:, :LANES]
    bot = pltpu.roll(acc[:, LANES:], nsub - 1, 0)
    row = lax.broadcasted_iota(jnp.int32, (nsub, LANES), 0)
    return jnp.where(row < nvalid, top + bot + const, 0.0)


def _pe_const(pe_ref, w_ref):
    return jnp.dot(pe_ref[...], w_ref[...], precision=lax.Precision.HIGHEST,
                   preferred_element_type=F32)[0:1, :]


def _cmp_prompt_kernel(k_ref, v_ref, wk_ref, wv_ref, pek_ref, pev_ref, wkf_ref, wvf_ref, g_ref,
                       ko_ref, vo_ref):
    nsub = ko_ref.shape[0]
    acck = jnp.zeros((nsub, 2 * HEAD_DIM), F32)
    accv = jnp.zeros((nsub, 2 * HEAD_DIM), F32)
    for l in range(CMP_STRIDE):
        xk = k_ref[pl.ds(l, nsub, stride=CMP_STRIDE), :].astype(BF16)
        xv = v_ref[pl.ds(l, nsub, stride=CMP_STRIDE), :].astype(BF16)
        acck = acck + _dot(xk, wk_ref[l])
        accv = accv + _dot(xv, wv_ref[l])
    kc = _compress_combine(acck, _pe_const(pek_ref, wkf_ref), nsub - 1)
    ms = jnp.mean(kc * kc, axis=-1, keepdims=True)
    ko_ref[...] = (kc * lax.rsqrt(ms + EPS)) * g_ref[...]
    vo_ref[...] = _compress_combine(accv, _pe_const(pev_ref, wvf_ref), nsub - 1)


def _cmp_weights(w_cmp, pe):
    wcat = jnp.concatenate([w_cmp[:CMP_STRIDE], w_cmp[CMP_STRIDE:]], axis=2).astype(BF16)
    pe8 = jnp.broadcast_to(pe.reshape(1, CMP_BLK * HEAD_DIM), (8, CMP_BLK * HEAD_DIM))
    return wcat, pe8, w_cmp.reshape(CMP_BLK * HEAD_DIM, HEAD_DIM)


def compress_prompt(zf, b, t, wk, wv, gk):
    nsub = t // CMP_STRIDE
    wkc, pek, wkf = wk
    wvc, pev, wvf = wv
    full = lambda a: pl.BlockSpec(a.shape, lambda i, g: (0,) * a.ndim)
    head = lambda off: pl.BlockSpec((t, HEAD_DIM), lambda i, g: (i, off // HEAD_DIM + g))
    out = jax.ShapeDtypeStruct((b * N_KV_HEADS, nsub, HEAD_DIM), F32)
    return pl.pallas_call(
        _cmp_prompt_kernel,
        grid=(b, N_KV_HEADS),
        in_specs=[head(NSA_W), head(NSA_W + KV_W),
                  full(wkc), full(wvc), full(pek), full(pev), full(wkf), full(wvf), full(gk)],
        out_specs=[pl.BlockSpec((None, nsub, HEAD_DIM), lambda i, g: (i * N_KV_HEADS + g, 0, 0))] * 2,
        out_shape=[out, out],
        compiler_params=_cparams(("parallel", "parallel")),
        name="cmp_prompt",
    )(zf, zf, wkc, wvc, pek, pev, wkf, wvf, gk)


def _sel_matrix(nc_pad, nc, ns_pad, ns):
    c0 = np.arange(nc_pad) * CMP_STRIDE
    s0 = np.arange(ns_pad) * SEL_BLK
    m = (c0[:, None] < s0[None, :] + SEL_BLK) & (c0[:, None] + CMP_BLK > s0[None, :])
    m &= (np.arange(nc_pad)[:, None] < nc) & (np.arange(ns_pad)[None, :] < ns)
    return jnp.asarray(m.astype(np.float32))


def _select_mask(imp, tpos, ns):
    lane = lax.broadcasted_iota(jnp.int32, imp.shape, 1)
    cur = tpos // SEL_BLK
    forced = (lane == 0) | (lane == cur) | (lane == cur - 1)
    future = lane > cur
    impm = jnp.where(future, -BIG, jnp.where(forced, BIG, imp))
    rank = jnp.zeros(imp.shape, F32)
    for s2 in range(ns):
        col = impm[:, s2:s2 + 1]
        before = (col > impm) | ((col == impm) & (lane > s2))
        rank = rank + jnp.where(before, 1.0, 0.0)
    return jnp.where(rank < float(min(N_SEL, ns)), 1.0, 0.0)


def _attn_prompt_kernel(q_ref, kc_ref, vc_ref, ks_ref, vs_ref, kw_ref, vw_ref, gn_ref, gate_ref,
                        msel_ref, e_ref, o_ref, *, tq, kvc, ns):
    g = pl.program_id(1)
    t0 = pl.program_id(2) * tq
    rows = Q_PER_KV * tq
    q4 = jnp.concatenate([q_ref[:, h * HEAD_DIM:(h + 1) * HEAD_DIM] for h in range(Q_PER_KV)], axis=0)
    tpos = t0 + lax.broadcasted_iota(jnp.int32, (tq, 1), 0)
    rep = lambda a: jnp.concatenate([a] * Q_PER_KV, axis=0)

    ncp = kc_ref.shape[0]
    kc = kc_ref[...].astype(BF16)
    vc = vc_ref[...].astype(BF16)
    endpos = lax.broadcasted_iota(jnp.int32, (1, ncp), 1) * CMP_STRIDE + (CMP_BLK - 1)
    p = _softmax_rows(_dot_nt(q4, kc) * SCALE, rep(endpos <= tpos))
    o_cmp = _dot(p.astype(BF16), vc)
    psum = p[0:tq]
    for h in range(1, Q_PER_KV):
        psum = psum + p[h * tq:(h + 1) * tq]
    imp = jnp.dot(psum, msel_ref[...], precision=lax.Precision.HIGHEST, preferred_element_type=F32)
    sel = _select_mask(imp, tpos, ns).astype(BF16)

    def body(c, carry):
        m, l, acc = carry
        k0 = pl.multiple_of(c * kvc, kvc)
        k = ks_ref[pl.ds(k0, kvc), :]
        v = vs_ref[pl.ds(k0, kvc), :]
        kpos = k0 + lax.broadcasted_iota(jnp.int32, (1, kvc), 1)
        mask = rep((_dot(sel, e_ref[c]) > 0.5) & (kpos <= tpos))
        s = jnp.where(mask, _dot_nt(q4, k) * SCALE, NEG)
        m_new = jnp.maximum(m, jnp.max(s, axis=-1, keepdims=True))
        alpha = jnp.exp(m - m_new)
        e = jnp.where(mask, jnp.exp(s - m_new), 0.0)
        l = alpha * l + jnp.sum(e, axis=-1, keepdims=True)
        acc = alpha * acc + _dot(e.astype(BF16), v)
        return m_new, l, acc

    nch = (t0 + tq + kvc - 1) // kvc
    m, l, acc = lax.fori_loop(0, nch, body, (jnp.full((rows, 1), NEG, F32), jnp.zeros((rows, 1), F32),
                                             jnp.zeros((rows, HEAD_DIM), F32)))
    o_sel = acc / jnp.where(l > 0.0, l, 1.0)

    nwin = WINDOW + tq
    w0 = pl.multiple_of(jnp.maximum(t0 - WINDOW, 0), tq)
    kpos = w0 + lax.broadcasted_iota(jnp.int32, (1, nwin), 1)
    diff = tpos - kpos
    pw = _softmax_rows(_dot_nt(q4, kw_ref[pl.ds(w0, nwin), :]) * SCALE, rep((diff >= 0) & (diff <= WINDOW)))
    o_win = _dot(pw.astype(BF16), vw_ref[pl.ds(w0, nwin), :])

    gs = _sigmoid(gn_ref[...])
    lane = lax.broadcasted_iota(jnp.int32, gs.shape, 1)
    for h in range(Q_PER_KV):
        r = slice(h * tq, (h + 1) * tq)
        o = None
        for br, ob in enumerate((o_cmp, o_sel, o_win)):
            cidx = br * N_HEADS + g * Q_PER_KV + h
            gcol = jnp.sum(jnp.where(lane == cidx, gs, 0.0), axis=-1, keepdims=True)
            o = gcol * ob[r] if o is None else o + gcol * ob[r]
        hs = slice(h * HEAD_DIM, (h + 1) * HEAD_DIM)
        o_ref[:, hs] = (o * gate_ref[:, hs].astype(F32)).astype(o_ref.dtype)


def attn_prompt(zb, kcmp, vcmp, gn, gate, b, t):
    tq, kvc = 128, 512
    assert t // CMP_STRIDE == LANES and t % kvc == 0 and t >= WINDOW + tq
    nq = t // tq
    ns = -(-t // SEL_BLK)
    nc = t // CMP_STRIDE - CMP_BLK // CMP_STRIDE + 1
    msel = _sel_matrix(LANES, nc, LANES, ns)
    kk = np.arange(t)
    e3 = (kk[None, :] // SEL_BLK == np.arange(LANES)[:, None]).astype(np.float32)
    e3 = jnp.asarray(e3.reshape(LANES, t // kvc, kvc).transpose(1, 0, 2), BF16)
    hb = lambda off: (lambda bi, g, i: (bi, off // HEAD_DIM + g))
    kv_spec = lambda off: pl.BlockSpec((t, HEAD_DIM), hb(off))
    cm_spec = pl.BlockSpec((None, t // CMP_STRIDE, HEAD_DIM), lambda bi, g, i: (bi * N_KV_HEADS + g, 0, 0))
    row4 = pl.BlockSpec((tq, Q_PER_KV * HEAD_DIM), lambda bi, g, i: (bi * nq + i, g))
    return pl.pallas_call(
        functools.partial(_attn_prompt_kernel, tq=tq, kvc=kvc, ns=ns),
        grid=(b, N_KV_HEADS, nq),
        in_specs=[row4, cm_spec, cm_spec,
                  kv_spec(NSA_W + 2 * KV_W), kv_spec(NSA_W + 3 * KV_W),
                  kv_spec(NSA_W + 4 * KV_W), kv_spec(NSA_W + 5 * KV_W),
                  pl.BlockSpec((tq, LANES), lambda bi, g, i: (bi * nq + i, 0)),
                  row4,
                  pl.BlockSpec(msel.shape, lambda bi, g, i: (0, 0)),
                  pl.BlockSpec(e3.shape, lambda bi, g, i: (0, 0, 0))],
        out_specs=row4,
        out_shape=jax.ShapeDtypeStruct((b * t, NSA_W), BF16),
        compiler_params=_cparams(("parallel", "parallel", "arbitrary")),
        name="attn_prompt",
    )(zb, kcmp, vcmp, zb, zb, zb, zb, gn, gate, msel, e3)


def _layer_norm(v, g, b):
    vc = v - jnp.mean(v, axis=-1, keepdims=True)
    var = jnp.mean(vc * vc, axis=-1, keepdims=True)
    return vc * lax.rsqrt(var + EPS) * g + b


def _mlp_prompt_kernel(u_ref, v_ref, gate_ref, lg_ref, lb_ref, ws_ref, bst_ref, o_ref):
    tm = u_ref.shape[0]
    gw = ws_ref.shape[1]
    vb = _layer_norm(v_ref[...], lg_ref[...], lb_ref[...]).astype(BF16)
    row = lax.broadcasted_iota(jnp.int32, (CHUNK, CHUNK), 0)
    col = lax.broadcasted_iota(jnp.int32, (CHUNK, CHUNK), 1)
    for gi in range(MLP_GROUPS):
        cs = slice(gi * gw, (gi + 1) * gw)
        wsg = jnp.where(row >= col, ws_ref[gi], 0.0).astype(BF16)
        bias = bst_ref[:, gi:gi + 1]
        for ch in range(tm // CHUNK):
            rs = slice(ch * CHUNK, (ch + 1) * CHUNK)
            mixed = _dot(wsg, vb[rs, cs]) + bias
            o_ref[rs, cs] = (u_ref[rs, cs] * mixed * gate_ref[rs, cs].astype(F32)).astype(o_ref.dtype)


def mlp_prompt(uv, gate, ln_g, ln_b, w_s, b_s, tm):
    m, w2 = uv.shape
    w = w2 // 2
    assert w // MLP_GROUPS == CHUNK == w_s.shape[1]
    full = lambda a: pl.BlockSpec(a.shape, lambda i: (0,) * a.ndim)
    bst = b_s.T
    return pl.pallas_call(
        _mlp_prompt_kernel,
        grid=(m // tm,),
        in_specs=[pl.BlockSpec((tm, w), lambda i: (i, 0)), pl.BlockSpec((tm, w), lambda i: (i, 1)),
                  pl.BlockSpec((tm, w), lambda i: (i, 0)),
                  pl.BlockSpec((1, w), lambda i: (0, 0)), pl.BlockSpec((1, w), lambda i: (0, 0)),
                  full(w_s), full(bst)],
        out_specs=pl.BlockSpec((tm, w), lambda i: (i, 0)),
        out_shape=jax.ShapeDtypeStruct((m, w), BF16),
        compiler_params=_cparams(("parallel",)),
        name="mlp_prompt",
    )(uv, uv, gate, ln_g.reshape(1, w), ln_b.reshape(1, w), w_s, bst)


def _softplus(x):
    return jnp.maximum(x, 0.0) + jnp.log1p(jnp.exp(-jnp.abs(x)))


def _lru_gates(xc, wa_ref, wx_ref, ba, bx, lam):
    nh = xc.shape[1] // HEAD_DIM
    rs, is_ = [], []
    for hh in range(nh):
        xh = xc[:, hh * HEAD_DIM:(hh + 1) * HEAD_DIM].astype(BF16)
        rs.append(_dot(xh, wa_ref[hh]))
        is_.append(_dot(xh, wx_ref[hh]))
    r = _sigmoid(jnp.concatenate(rs, axis=1) + ba)
    i = _sigmoid(jnp.concatenate(is_, axis=1) + bx)
    log_a = -LRU_C * r * _softplus(-lam)
    th = jnp.tanh(log_a)
    return jnp.exp(log_a), jnp.sqrt(-2.0 * th / (1.0 - th)) * (i * xc)


def _lru_prompt_kernel(x_ref, gate_ref, cw_ref, cb_ref, wa_ref, wx_ref, ba_ref, bx_ref, lam_ref,
                       o_ref, h_ref, a_scr, u_scr):
    t, wb = x_ref.shape
    x = x_ref[...]
    row = lax.broadcasted_iota(jnp.int32, (t, wb), 0)
    xc = cb_ref[...] + x * cw_ref[CONV_W - 1:CONV_W, :]
    for d in range(1, CONV_W):
        xs = jnp.where(row >= d, pltpu.roll(x, d, 0), 0.0)
        xc = xc + xs * cw_ref[CONV_W - 1 - d:CONV_W - d, :]
    a, u = _lru_gates(xc, wa_ref, wx_ref, ba_ref[...], bx_ref[...], lam_ref[...])
    a_scr[...] = a
    u_scr[...] = u
    row8 = lax.broadcasted_iota(jnp.int32, (8, wb), 0)

    def body(bi, h):
        r0 = pl.multiple_of(bi * 8, 8)
        a8 = a_scr[pl.ds(r0, 8), :]
        u8 = u_scr[pl.ds(r0, 8), :]
        for d in (1, 2, 4):
            a_sh = jnp.where(row8 >= d, pltpu.roll(a8, d, 0), 1.0)
            u_sh = jnp.where(row8 >= d, pltpu.roll(u8, d, 0), 0.0)
            u8 = a8 * u_sh + u8
            a8 = a8 * a_sh
        h8 = a8 * h + u8
        u_scr[pl.ds(r0, 8), :] = h8
        return h8[7:8, :]

    h = lax.fori_loop(0, t // 8, body, jnp.zeros((1, wb), F32))
    h_ref[...] = h
    o_ref[...] = (u_scr[...] * gate_ref[...].astype(F32)).astype(o_ref.dtype)


def lru_prompt(xl, gate, conv_w, conv_b, wa, wx, ba, bx, lam, b, t):
    w = xl.shape[1]
    wb = 512
    nh = wb // HEAD_DIM
    assert w % wb == 0 and t % 8 == 0 and wa.shape[1] == HEAD_DIM
    vec = lambda a: a.reshape(1, w)
    vspec = pl.BlockSpec((1, wb), lambda i, j: (0, j))
    blk = pl.BlockSpec((t, wb), lambda i, j: (i, j))
    hspec = pl.BlockSpec((nh, HEAD_DIM, HEAD_DIM), lambda i, j: (j, 0, 0))
    return pl.pallas_call(
        _lru_prompt_kernel,
        grid=(b, w // wb),
        in_specs=[blk, blk, pl.BlockSpec((CONV_W, wb), lambda i, j: (0, j)), vspec,
                  hspec, hspec, vspec, vspec, vspec],
        out_specs=[blk, pl.BlockSpec((None, 1, wb), lambda i, j: (i, 0, j))],
        out_shape=[jax.ShapeDtypeStruct((b * t, w), BF16), jax.ShapeDtypeStruct((b, 1, w), F32)],
        scratch_shapes=[pltpu.VMEM((t, wb), F32), pltpu.VMEM((t, wb), F32)],
        compiler_params=_cparams(("parallel", "parallel")),
        name="lru_prompt",
    )(xl, gate, conv_w, vec(conv_b), wa.astype(BF16), wx.astype(BF16), vec(ba), vec(bx), vec(lam))


ATT_W = NSA_W + 6 * KV_W
SEG = 512


def _pack_w_in(w_in, d_model, mlp_w, lru_w):
    gn0 = ATT_W + N_BRANCH * N_HEADS
    pad = jnp.zeros((w_in.shape[0], SEG - N_BRANCH * N_HEADS), w_in.dtype)
    w = jnp.concatenate([w_in[:, :gn0], pad, w_in[:, gn0:]], axis=1).astype(BF16)
    sizes = (("att", ATT_W), ("gn", SEG), ("gate_nsa", NSA_W), ("uv", 2 * mlp_w), ("gate_mlp", mlp_w),
             ("xl", lru_w), ("gate_lru", lru_w), ("gm", N_BRANCH * d_model))
    offs, acc = {}, 0
    for name, size in sizes:
        offs[name] = (acc, size)
        acc += size
    assert acc == w.shape[1]
    return w, offs


def _norm_vectors(q_g, k_g):
    one = jnp.ones((KV_W,), F32)
    zero = jnp.zeros((KV_W,), F32)
    gain = jnp.concatenate([jnp.tile(q_g, N_HEADS), one, one, jnp.tile(k_g[1], N_KV_HEADS), one,
                            jnp.tile(k_g[2], N_KV_HEADS), one])
    flag = jnp.concatenate([jnp.ones((NSA_W,), F32), zero, zero, one, zero, one, zero])
    return gain.reshape(1, ATT_W), flag.reshape(1, ATT_W)


def _in_proj(x2d, lw, tm):
    h = rms_rows(x2d, lw["norm_g"], min(tm, 512))
    w, offs = lw["w_pack"]
    p = lambda name, act, dt, tn=SEG, n=None, **kw: proj(
        h, w, offs[name][0], n or offs[name][1], act, dt, tm, tn, **kw)
    zf, zb = p("att", "norm", None, gain=lw["gain"], flag=lw["flag"])
    return dict(zf=zf, zb=zb, gn=p("gn", "none", F32, tn=LANES, n=LANES),
                gate_nsa=p("gate_nsa", "silu", BF16), uv=p("uv", "gelu", F32),
                gate_mlp=p("gate_mlp", "silu", BF16), xl=p("xl", "none", F32),
                gate_lru=p("gate_lru", "silu", BF16), gm=p("gm", "sigmoid", BF16))


def prompt_layer(x2d, lw, b, t, w_buf):
    tm = 1024
    z = _in_proj(x2d, lw, tm)
    kcmp, vcmp = compress_prompt(z["zf"], b, t, lw["cmp_k"], lw["cmp_v"], lw["k_norm_g"][0:1])
    o_nsa = attn_prompt(z["zb"], kcmp, vcmp, z["gn"], z["gate_nsa"], b, t)
    o_mlp = mlp_prompt(z["uv"], z["gate_mlp"], lw["mlp_ln_g"], lw["mlp_ln_b"], lw["w_s"], lw["b_s"], 512)
    o_lru, h_last = lru_prompt(z["xl"], z["gate_lru"], lw["conv_w"], lw["conv_b"], lw["lru_wa"], lw["lru_wx"],
                               lw["lru_ba"], lw["lru_bx"], lw["lru_lambda"], b, t)
    m = merge([o_nsa, o_mlp, o_lru], lw["w_br"], z["gm"], tm, SEG)
    y = out_proj(m, lw["w_out"], x2d, tm, SEG)
    zf = z["zf"].reshape(b, t, ATT_W)
    kv = lambda i: zf[:, :, NSA_W + i * KV_W:NSA_W + (i + 1) * KV_W].reshape(b, t, N_KV_HEADS, HEAD_DIM)
    xl = z["xl"].reshape(b, t, -1)
    assert t >= w_buf and t >= CONV_W - 1
    state = dict(cmp_k=kv(0), cmp_v=kv(1), sel_k=kv(2), sel_v=kv(3),
                 win_k=kv(4)[:, t - w_buf:], win_v=kv(5)[:, t - w_buf:],
                 lru_h=h_last.reshape(b, -1), lru_conv=xl[:, t - (CONV_W - 1):])
    return y, state


PAGES_PER_STEP = 32


def _cmp_paged_kernel(pt_ref, *refs, pg):
    k_refs, v_refs = refs[:pg], refs[pg:2 * pg]
    wk_ref, wv_ref, ko_ref, vo_ref = refs[2 * pg:]
    rows = k_refs[0].shape[0]
    nsp = rows // N_KV_HEADS // CMP_STRIDE
    stride = N_KV_HEADS * CMP_STRIDE
    for g in range(N_KV_HEADS):
        acck = jnp.zeros((pg * nsp, 2 * HEAD_DIM), F32)
        accv = jnp.zeros((pg * nsp, 2 * HEAD_DIM), F32)
        for l in range(CMP_STRIDE):
            st = N_KV_HEADS * l + g
            xk = jnp.concatenate([r[pl.ds(st, nsp, stride=stride), :] for r in k_refs], axis=0)
            xv = jnp.concatenate([r[pl.ds(st, nsp, stride=stride), :] for r in v_refs], axis=0)
            acck = acck + _dot(xk.astype(BF16), wk_ref[l])
            accv = accv + _dot(xv.astype(BF16), wv_ref[l])
        ko_ref[g] = acck
        vo_ref[g] = accv


def compress_paged(pool_k, pool_v, page_table, wkc, wvc, page0):
    b, n_pages = page_table.shape
    rows = pool_k.shape[1]
    pg = PAGES_PER_STEP
    assert n_pages % pg == 0
    nsp = rows // N_KV_HEADS // CMP_STRIDE
    page_spec = lambda p: pl.BlockSpec((None, rows, HEAD_DIM), lambda i, c, pt: (page0 + pt[i, c * pg + p], 0, 0))
    full = lambda a: pl.BlockSpec(a.shape, lambda i, c, pt: (0,) * a.ndim)
    out = jax.ShapeDtypeStruct((b, N_KV_HEADS, n_pages * nsp, 2 * HEAD_DIM), F32)
    ospec = pl.BlockSpec((None, N_KV_HEADS, pg * nsp, 2 * HEAD_DIM), lambda i, c, pt: (i, 0, c, 0))
    return pl.pallas_call(
        functools.partial(_cmp_paged_kernel, pg=pg),
        grid_spec=pltpu.PrefetchScalarGridSpec(
            num_scalar_prefetch=1,
            grid=(b, n_pages // pg),
            in_specs=[page_spec(p) for p in range(pg)] * 2 + [full(wkc), full(wvc)],
            out_specs=[ospec, ospec]),
        out_shape=[out, out],
        compiler_params=_cparams(("parallel", "arbitrary")),
        name="cmp_paged",
    )(page_table, *([pool_k] * pg), *([pool_v] * pg), wkc, wvc)


def _cmp_sample_kernel(tbk_ref, tbv_ref, q_ref, pek_ref, pev_ref, wkf_ref, wvf_ref, gk_ref, msel_ref,
                       o_ref, idx_ref, *, tpos, nc, ns):
    ck = _pe_const(pek_ref, wkf_ref)
    cv = _pe_const(pev_ref, wvf_ref)
    q = q_ref[...].astype(BF16)
    nh = q.shape[0]
    nsub = tbk_ref.shape[1]
    hrow = lax.broadcasted_iota(jnp.int32, (nh, 1), 0)
    blk = lax.broadcasted_iota(jnp.int32, (1, nsub), 1)
    cmask = (blk * CMP_STRIDE + (CMP_BLK - 1) <= tpos) & (blk < nc)
    row8 = lax.broadcasted_iota(jnp.int32, (8, nsub), 0)
    o = jnp.zeros((nh, HEAD_DIM), F32)
    psum = jnp.zeros((8, nsub), F32)
    for g in range(N_KV_HEADS):
        kc = _compress_combine(tbk_ref[g], ck, nc)
        ms = jnp.mean(kc * kc, axis=-1, keepdims=True)
        kc = (kc * lax.rsqrt(ms + EPS)) * gk_ref[...]
        vc = _compress_combine(tbv_ref[g], cv, nc)
        p = _softmax_rows(_dot_nt(q, kc.astype(BF16)) * SCALE, cmask & (hrow // Q_PER_KV == g))
        o = o + _dot(p.astype(BF16), vc.astype(BF16))
        psum = jnp.where(row8 == g, jnp.sum(p, axis=0, keepdims=True), psum)
    o_ref[...] = o
    imp = jnp.dot(psum, msel_ref[...], precision=lax.Precision.HIGHEST, preferred_element_type=F32)
    lane = lax.broadcasted_iota(jnp.int32, imp.shape, 1)
    cur = tpos // SEL_BLK
    forced = (lane == 0) | (lane == cur) | (lane == cur - 1)
    impm = jnp.where(lane > cur, -BIG, jnp.where(forced, BIG, imp))
    impm = jnp.where(lane < ns, impm, -jnp.inf)
    lane_f = lane.astype(F32)
    out_lane = lax.broadcasted_iota(jnp.int32, idx_ref.shape, 1)
    idxs = jnp.zeros(idx_ref.shape, F32)
    for j in range(min(N_SEL, ns)):
        mx = jnp.max(impm, axis=-1, keepdims=True)
        am = jnp.min(jnp.where(impm == mx, lane_f, 1e9), axis=-1, keepdims=True)
        idxs = jnp.where(out_lane == j, am, idxs)
        impm = jnp.where(lane_f == am, -jnp.inf, impm)
    idx_ref[...] = idxs.astype(jnp.int32)


def cmp_sample(tbk, tbv, q3, ck, cv, gk, tpos):
    b, nkv, nsub, _ = tbk.shape
    nc = (tpos + 1) // CMP_STRIDE - CMP_BLK // CMP_STRIDE + 1
    ns = -(-(tpos + 1) // SEL_BLK)
    ns_pad = -(-ns // LANES) * LANES
    assert nc <= nsub
    msel = _sel_matrix(nsub, nc, ns_pad, ns)
    _, pek, wkf = ck
    _, pev, wvf = cv
    full = lambda a: pl.BlockSpec(a.shape, lambda i: (0,) * a.ndim)
    tb_spec = pl.BlockSpec((None, nkv, nsub, 2 * HEAD_DIM), lambda i: (i, 0, 0, 0))
    return pl.pallas_call(
        functools.partial(_cmp_sample_kernel, tpos=tpos, nc=nc, ns=ns),
        grid=(b,),
        in_specs=[tb_spec, tb_spec, pl.BlockSpec((None, N_HEADS, HEAD_DIM), lambda i: (i, 0, 0)),
                  full(pek), full(pev), full(wkf), full(wvf), full(gk), full(msel)],
        out_specs=[pl.BlockSpec((None, N_HEADS, HEAD_DIM), lambda i: (i, 0, 0)),
                   pl.BlockSpec((None, 8, LANES), lambda i: (i, 0, 0))],
        out_shape=[jax.ShapeDtypeStruct((b, N_HEADS, HEAD_DIM), F32),
                   jax.ShapeDtypeStruct((b, 8, LANES), jnp.int32)],
        compiler_params=_cparams(("parallel",)),
        name="cmp_sample",
    )(tbk, tbv, q3, pek, pev, wkf, wvf, gk, msel)


def _attend_with_new(q, k_all, v_all, mask, k_new, v_new, new_ok):
    s = jnp.where(mask, _dot_nt(q.astype(BF16), k_all) * SCALE, NEG)
    s_new = jnp.where(new_ok, jnp.sum(q * k_new, axis=-1, keepdims=True) * SCALE, NEG)
    m = jnp.maximum(jnp.max(s, axis=-1, keepdims=True), s_new)
    e = jnp.where(mask, jnp.exp(s - m), 0.0)
    e_new = jnp.where(new_ok, jnp.exp(s_new - m), 0.0)
    d = jnp.sum(e, axis=-1, keepdims=True) + e_new
    o = _dot(e.astype(BF16), v_all) + e_new * v_new
    return o / jnp.where(d > 0.0, d, 1.0)


def _attn_sample_kernel(pt_ref, idx_ref, *refs, nblk, tpos, past_len, w_buf):
    k_refs, v_refs = refs[:nblk], refs[nblk:2 * nblk]
    (q_ref, ocmp_ref, kw_ref, vw_ref, ksn_ref, vsn_ref, kwn_ref, vwn_ref, gn_ref, gate_ref,
     o_ref) = refs[2 * nblk:]
    b = pl.program_id(0)
    g = pl.program_id(1)
    q = q_ref[...]
    rb = k_refs[0].shape[0]
    r = lax.broadcasted_iota(jnp.int32, (1, rb), 1)
    tok, hd = r // N_KV_HEADS, r % N_KV_HEADS
    masks = []
    new_sel = False
    for j in range(nblk):
        s = idx_ref[b, g * nblk + j]
        kpos = s * SEL_BLK + tok
        masks.append((hd == g) & (kpos <= tpos) & (kpos < past_len))
        new_sel = new_sel | (s == past_len // SEL_BLK)
    k_all = jnp.concatenate([kr[...].astype(BF16) for kr in k_refs], axis=0)
    v_all = jnp.concatenate([vr[...].astype(BF16) for vr in v_refs], axis=0)
    o_sel = _attend_with_new(q, k_all, v_all, jnp.concatenate(masks, axis=1),
                             ksn_ref[...], vsn_ref[...], new_sel & (past_len <= tpos))
    rw = lax.broadcasted_iota(jnp.int32, (1, kw_ref.shape[0]), 1)
    diff = tpos - (past_len - w_buf + rw // N_KV_HEADS)
    wmask = (rw % N_KV_HEADS == g) & (diff >= 0) & (diff <= WINDOW)
    o_win = _attend_with_new(q, kw_ref[...].astype(BF16), vw_ref[...].astype(BF16), wmask,
                             kwn_ref[...], vwn_ref[...], tpos - past_len <= WINDOW)
    gs = _sigmoid(gn_ref[...])
    o = gs[:, 0:1] * ocmp_ref[...] + gs[:, 1:2] * o_sel + gs[:, 2:3] * o_win
    o_ref[...] = o * gate_ref[...]


def attn_sample(pool_k, pool_v, page_table, idx, q4, ocmp4, win_k, win_v, ks_new, vs_new, kw_new, vw_new,
                gn4, gate4, past_len, w_buf, page_size, page0, seq0):
    b, n_pages = page_table.shape
    nblk = idx.shape[1] // N_KV_HEADS
    rb = pool_k.shape[1]
    halves = page_size // SEL_BLK
    tpos = past_len

    def blk_map(j):
        def f(i, g, pt, ix):
            s = ix[i, g * nblk + j]
            page = pt[i, jnp.minimum(s // halves, n_pages - 1)]
            return ((page0 + page) * halves + s % halves, 0, 0)
        return f

    blk_spec = lambda j: pl.BlockSpec((None, rb, HEAD_DIM), blk_map(j))
    per_bg = lambda a: pl.BlockSpec((None, None) + a.shape[2:], lambda i, g, pt, ix: (i, g, 0, 0))
    per_b = lambda a: pl.BlockSpec((None,) + a.shape[1:], lambda i, g, pt, ix: (seq0 + i, 0, 0))
    small = (q4, ocmp4)
    news = (ks_new, vs_new, kw_new, vw_new, gn4, gate4)
    return pl.pallas_call(
        functools.partial(_attn_sample_kernel, nblk=nblk, tpos=tpos, past_len=past_len, w_buf=w_buf),
        grid_spec=pltpu.PrefetchScalarGridSpec(
            num_scalar_prefetch=2,
            grid=(b, N_KV_HEADS),
            in_specs=[blk_spec(j) for j in range(nblk)] * 2 + [per_bg(a) for a in small]
            + [per_b(win_k), per_b(win_v)] + [per_bg(a) for a in news],
            out_specs=per_bg(q4)),
        out_shape=jax.ShapeDtypeStruct(q4.shape, F32),
        compiler_params=_cparams(("parallel", "arbitrary")),
        name="attn_sample",
    )(page_table, idx, *([pool_k] * nblk), *([pool_v] * nblk), q4, ocmp4, win_k, win_v, *news)


def _point_sample_kernel(u_ref, v_ref, gmlp_ref, lg_ref, lb_ref, ws0_ref, bs0_ref,
                         x_ref, glru_ref, buf_ref, h0_ref, cw_ref, cb_ref, wa_ref, wx_ref, ba_ref, bx_ref,
                         lam_ref, omlp_ref, vrow_ref, olru_ref, hnew_ref):
    vn = _layer_norm(v_ref[...], lg_ref[...], lb_ref[...])
    vrow_ref[...] = vn
    mixed = ws0_ref[...] * vn + bs0_ref[...]
    omlp_ref[...] = (u_ref[...] * mixed * gmlp_ref[...].astype(F32)).astype(omlp_ref.dtype)
    x = x_ref[...]
    xc = cb_ref[...] + x * cw_ref[CONV_W - 1:CONV_W, :]
    for j in range(CONV_W - 1):
        xc = xc + buf_ref[j] * cw_ref[j:j + 1, :]
    a, u = _lru_gates(xc, wa_ref, wx_ref, ba_ref[...], bx_ref[...], lam_ref[...])
    h = a * h0_ref[...] + u
    hnew_ref[...] = h
    olru_ref[...] = (h * glru_ref[...].astype(F32)).astype(olru_ref.dtype)


def point_sample(uv, gate_mlp, xl, gate_lru, buf_t, h0, lw):
    b, w2 = uv.shape
    w = w2 // 2
    gw = w // MLP_GROUPS
    vec = lambda a: a.reshape(1, -1)
    ws0 = vec(jnp.repeat(lw["w_s"][:, 0, 0], gw))
    bs0 = vec(jnp.repeat(lw["b_s"][:, 0], gw))
    args = (uv[:, :w], uv[:, w:], gate_mlp, vec(lw["mlp_ln_g"]), vec(lw["mlp_ln_b"]), ws0, bs0,
            xl, gate_lru, buf_t, h0, lw["conv_w"], vec(lw["conv_b"]), lw["lru_wa"].astype(BF16),
            lw["lru_wx"].astype(BF16), vec(lw["lru_ba"]), vec(lw["lru_bx"]), vec(lw["lru_lambda"]))
    lw_ = xl.shape[1]
    return pl.pallas_call(
        _point_sample_kernel,
        out_shape=[jax.ShapeDtypeStruct((b, w), BF16), jax.ShapeDtypeStruct((b, w), F32),
                   jax.ShapeDtypeStruct((b, lw_), BF16), jax.ShapeDtypeStruct((b, lw_), F32)],
        compiler_params=pltpu.CompilerParams(vmem_limit_bytes=VMEM_LIMIT),
        name="point_sample",
    )(*args)


def sample_layer(x2d, lw, l, caches, page_table, past_len, w_buf):
    b = x2d.shape[0]
    z = _in_proj(x2d, lw, b)
    zf = z["zf"]
    seg = lambda i: zf[:, NSA_W + i * KV_W:NSA_W + (i + 1) * KV_W]
    kvh = lambda a: a.reshape(b, N_KV_HEADS, 1, HEAD_DIM)
    n_pool, page_size = caches["cmp_k"].shape[1:3]
    page0 = l * n_pool
    pool3 = lambda a: a.reshape(-1, page_size * N_KV_HEADS, HEAD_DIM)
    halves = lambda a: a.reshape(-1, SEL_BLK * N_KV_HEADS, HEAD_DIM)
    tbk, tbv = compress_paged(pool3(caches["cmp_k"]), pool3(caches["cmp_v"]), page_table,
                              lw["cmp_k"][0], lw["cmp_v"][0], page0)
    q3 = zf[:, :NSA_W].reshape(b, N_HEADS, HEAD_DIM)
    o_cmp, idx = cmp_sample(tbk, tbv, q3, lw["cmp_k"], lw["cmp_v"], lw["k_norm_g"][0:1], past_len)
    n_sel = min(N_SEL, -(-(past_len + 1) // SEL_BLK))
    idx = idx[:, :N_KV_HEADS, :n_sel].reshape(b, N_KV_HEADS * n_sel)
    four = lambda a: a.reshape(b, N_KV_HEADS, Q_PER_KV, -1)
    gn4 = z["gn"][:, :N_BRANCH * N_HEADS].reshape(b, N_BRANCH, N_KV_HEADS, Q_PER_KV).transpose(0, 2, 3, 1)
    win3 = lambda a: a.reshape(-1, w_buf * N_KV_HEADS, HEAD_DIM)
    o_nsa = attn_sample(halves(caches["sel_k"]), halves(caches["sel_v"]), page_table, idx,
                        four(q3), four(o_cmp), win3(caches["win_k"]), win3(caches["win_v"]),
                        kvh(seg(2)), kvh(seg(3)), kvh(seg(4)), kvh(seg(5)),
                        gn4, four(z["gate_nsa"].astype(F32)), past_len, w_buf, page_size, page0, l * b)
    o_mlp, v_rows, o_lru, h_new = point_sample(
        z["uv"], z["gate_mlp"], z["xl"], z["gate_lru"], caches["lru_conv"].transpose(1, 0, 2),
        caches["lru_h"].astype(F32), lw)
    m = merge([o_nsa.reshape(b, NSA_W).astype(BF16), o_mlp, o_lru], lw["w_br"], z["gm"], b, SEG)
    y = out_proj(m, lw["w_out"], x2d, b, SEG)
    tok = lambda a: a.reshape(b, 1, N_KV_HEADS, HEAD_DIM)
    state = dict(cmp_k=tok(seg(0)), cmp_v=tok(seg(1)), sel_k=tok(seg(2)), sel_v=tok(seg(3)),
                 win_k=jnp.concatenate([caches["win_k"][l], tok(seg(4))], axis=1)[:, -w_buf:],
                 win_v=jnp.concatenate([caches["win_v"][l], tok(seg(5))], axis=1)[:, -w_buf:],
                 lru_h=h_new, lru_conv=jnp.concatenate([caches["lru_conv"], z["xl"][:, None]], axis=1)[:, 1:],
                 mlp_v=v_rows[:, None])
    return y, state


def _layer_weights(l, p):
    d_model = p["w_in"].shape[1]
    mlp_w = p["mlp_ln_g"].shape[1]
    lru_w = p["lru_lambda"].shape[1]
    names = ("norm_g", "q_norm_g", "k_norm_g", "mlp_ln_g", "mlp_ln_b", "w_s", "b_s", "conv_w", "conv_b",
             "lru_wa", "lru_ba", "lru_wx", "lru_bx", "lru_lambda")
    lw = {n: p[n][l] for n in names}
    lw["w_pack"] = _pack_w_in(p["w_in"][l], d_model, mlp_w, lru_w)
    lw["gain"], lw["flag"] = _norm_vectors(p["q_norm_g"][l], p["k_norm_g"][l])
    lw["cmp_k"] = _cmp_weights(p["w_cmp_k"][l], p["cmp_pe_k"][l])
    lw["cmp_v"] = _cmp_weights(p["w_cmp_v"][l], p["cmp_pe_v"][l])
    lw["w_br"] = [p[n][l].astype(BF16) for n in ("w_br_nsa", "w_br_mlp", "w_br_lru")]
    lw["w_out"] = p["w_out"][l].astype(BF16)
    return lw


def kernel(x_prompt, x_sample, cache_cmp_k, cache_cmp_v, cache_sel_k, cache_sel_v, state_win_k, state_win_v,
           state_lru_h, state_lru_conv, page_table, norm_g, w_in, q_norm_g, k_norm_g, cmp_pe_k, cmp_pe_v,
           w_cmp_k, w_cmp_v, mlp_ln_g, mlp_ln_b, w_s, b_s, conv_w, conv_b, lru_wa, lru_ba, lru_wx, lru_bx,
           lru_lambda, w_br_nsa, w_br_mlp, w_br_lru, w_out):
    params = dict(norm_g=norm_g, w_in=w_in, q_norm_g=q_norm_g, k_norm_g=k_norm_g, cmp_pe_k=cmp_pe_k,
                  cmp_pe_v=cmp_pe_v, w_cmp_k=w_cmp_k, w_cmp_v=w_cmp_v, mlp_ln_g=mlp_ln_g, mlp_ln_b=mlp_ln_b,
                  w_s=w_s, b_s=b_s, conv_w=conv_w, conv_b=conv_b, lru_wa=lru_wa, lru_ba=lru_ba, lru_wx=lru_wx,
                  lru_bx=lru_bx, lru_lambda=lru_lambda, w_br_nsa=w_br_nsa, w_br_mlp=w_br_mlp,
                  w_br_lru=w_br_lru, w_out=w_out)
    depth = w_in.shape[0]
    b, t, d = x_prompt.shape
    bs, ts, _ = x_sample.shape
    assert ts == 1
    w_buf = state_win_k.shape[2]
    past_len = page_table.shape[1] * cache_cmp_k.shape[2]
    yp = x_prompt.reshape(b * t, d)
    ys = x_sample.reshape(bs * ts, d)
    p_st, s_st = [], []
    for l in range(depth):
        lw = _layer_weights(l, params)
        yp, sp = prompt_layer(yp, lw, b, t, w_buf)
        caches = dict(cmp_k=cache_cmp_k, cmp_v=cache_cmp_v, sel_k=cache_sel_k, sel_v=cache_sel_v,
                      win_k=state_win_k, win_v=state_win_v, lru_h=state_lru_h[l],
                      lru_conv=state_lru_conv[l])
        ys, ss = sample_layer(ys, lw, l, caches, page_table, past_len, w_buf)
        p_st.append(sp)
        s_st.append(ss)
    stk = lambda sts, name: jnp.stack([st[name] for st in sts])
    names = ("cmp_k", "cmp_v", "sel_k", "sel_v", "win_k", "win_v", "lru_h", "lru_conv")
    return ((yp.reshape(b, t, d), ys.reshape(bs, ts, d))
            + tuple(stk(p_st, n) for n in names)
            + tuple(stk(s_st, n) for n in names + ("mlp_v",)))
```

```python
import functools

import numpy as np
import jax
import jax.numpy as jnp
from jax import lax
from jax.experimental import pallas as pl
from jax.experimental.pallas import tpu as pltpu

N_HEADS = 16
HEAD_DIM = 128
N_KV_HEADS = 4
Q_PER_KV = N_HEADS // N_KV_HEADS
NSA_W = N_HEADS * HEAD_DIM
KV_W = N_KV_HEADS * HEAD_DIM
CMP_BLK = 32
CMP_STRIDE = 16
SEL_BLK = 64
N_SEL = 16
WINDOW = 512
CHUNK = 128
MLP_GROUPS = 8
LRU_HEADS = 8
CONV_W = 4
LRU_C = 8.0
N_BRANCH = 3
EPS = 1e-6
BIG = 1e9
NEG = -1e30
SCALE = HEAD_DIM ** -0.5

LANES = 128
VMEM_LIMIT = 56 * 1024 * 1024

BF16 = jnp.bfloat16
F32 = jnp.float32


def _cparams(sem):
    return pltpu.CompilerParams(dimension_semantics=sem, vmem_limit_bytes=VMEM_LIMIT)


def _gelu(x):
    return 0.5 * x * (1.0 + jnp.tanh(0.7978845608028654 * (x + 0.044715 * (x * x * x))))


def _sigmoid(x):
    return 1.0 / (1.0 + jnp.exp(-x))


def _silu(x):
    return x * _sigmoid(x)


def _dot(a, b):
    return jnp.dot(a, b, preferred_element_type=F32)


def _dot_nt(a, b):
    return lax.dot_general(a, b, (((1,), (1,)), ((), ())), preferred_element_type=F32)


def _rms_kernel(x_ref, g_ref, o_ref):
    x = x_ref[...]
    ms = jnp.mean(x * x, axis=-1, keepdims=True)
    o_ref[...] = ((x * lax.rsqrt(ms + EPS)) * g_ref[...]).astype(o_ref.dtype)


def rms_rows(x, g, tm):
    m, d = x.shape
    return pl.pallas_call(
        _rms_kernel,
        grid=(m // tm,),
        in_specs=[pl.BlockSpec((tm, d), lambda i: (i, 0)),
                  pl.BlockSpec((1, d), lambda i: (0, 0))],
        out_specs=pl.BlockSpec((tm, d), lambda i: (i, 0)),
        out_shape=jax.ShapeDtypeStruct((m, d), BF16),
        compiler_params=_cparams(("parallel",)),
        name="rms_rows",
    )(x, g.reshape(1, d))


def _proj_kernel(a_ref, w_ref, *rest, act):
    acc = _dot(a_ref[...], w_ref[...])
    if act == "norm":
        gain_ref, flag_ref, of_ref, ob_ref = rest
        tn = acc.shape[1]
        for c in range(tn // LANES):
            sl = slice(c * LANES, (c + 1) * LANES)
            z = acc[:, sl]
            ms = jnp.mean(z * z, axis=-1, keepdims=True)
            zn = (z * lax.rsqrt(ms + EPS)) * gain_ref[:, sl]
            z = jnp.where(flag_ref[:, sl] > 0.5, zn, z)
            of_ref[:, sl] = z
            ob_ref[:, sl] = z.astype(ob_ref.dtype)
        return
    (o_ref,) = rest
    if act == "silu":
        acc = _silu(acc)
    elif act == "gelu":
        acc = _gelu(acc)
    elif act == "sigmoid":
        acc = _sigmoid(acc)
    o_ref[...] = acc.astype(o_ref.dtype)


def proj(a, w, col0, ncols, act, out_dtype, tm, tn, gain=None, flag=None):
    m, k = a.shape
    assert col0 % tn == 0 and ncols % tn == 0 and m % tm == 0
    jb = col0 // tn
    in_specs = [pl.BlockSpec((tm, k), lambda i, j: (i, 0)),
                pl.BlockSpec((k, tn), lambda i, j: (0, jb + j))]
    args = [a, w]
    if act == "norm":
        in_specs += [pl.BlockSpec((1, tn), lambda i, j: (0, j))] * 2
        args += [gain, flag]
        out_specs = [pl.BlockSpec((tm, tn), lambda i, j: (i, j))] * 2
        out_shape = [jax.ShapeDtypeStruct((m, ncols), F32), jax.ShapeDtypeStruct((m, ncols), BF16)]
    else:
        out_specs = pl.BlockSpec((tm, tn), lambda i, j: (i, j))
        out_shape = jax.ShapeDtypeStruct((m, ncols), out_dtype)
    return pl.pallas_call(
        functools.partial(_proj_kernel, act=act),
        grid=(m // tm, ncols // tn),
        in_specs=in_specs,
        out_specs=out_specs,
        out_shape=out_shape,
        compiler_params=_cparams(("parallel", "arbitrary")),
        name="proj_" + act,
    )(*args)


def _merge_kernel(a1, a2, a3, w1, w2, w3, g1, g2, g3, o_ref):
    m = g1[...].astype(F32) * _dot(a1[...], w1[...])
    m = m + g2[...].astype(F32) * _dot(a2[...], w2[...])
    m = m + g3[...].astype(F32) * _dot(a3[...], w3[...])
    o_ref[...] = m.astype(o_ref.dtype)


def merge(a_list, w_list, gm, tm, tn):
    m = a_list[0].shape[0]
    n = w_list[0].shape[1]
    nb = n // tn
    in_specs = [pl.BlockSpec((tm, a.shape[1]), lambda i, j: (i, 0)) for a in a_list]
    in_specs += [pl.BlockSpec((w.shape[0], tn), lambda i, j: (0, j)) for w in w_list]
    in_specs += [pl.BlockSpec((tm, tn), functools.partial(lambda i, j, b: (i, b * nb + j), b=b))
                 for b in range(N_BRANCH)]
    return pl.pallas_call(
        _merge_kernel,
        grid=(m // tm, nb),
        in_specs=in_specs,
        out_specs=pl.BlockSpec((tm, tn), lambda i, j: (i, j)),
        out_shape=jax.ShapeDtypeStruct((m, n), BF16),
        compiler_params=_cparams(("parallel", "arbitrary")),
        name="merge",
    )(*a_list, *w_list, gm, gm, gm)


def _resid_kernel(a_ref, w_ref, x_ref, o_ref):
    o_ref[...] = x_ref[...] + _dot(a_ref[...], w_ref[...])


def out_proj(a, w, x, tm, tn):
    m, k = a.shape
    n = w.shape[1]
    return pl.pallas_call(
        _resid_kernel,
        grid=(m // tm, n // tn),
        in_specs=[pl.BlockSpec((tm, k), lambda i, j: (i, 0)),
                  pl.BlockSpec((k, tn), lambda i, j: (0, j)),
                  pl.BlockSpec((tm, tn), lambda i, j: (i, j))],
        out_specs=pl.BlockSpec((tm, tn), lambda i, j: (i, j)),
        out_shape=jax.ShapeDtypeStruct((m, n), F32),
        compiler_params=_cparams(("parallel", "arbitrary")),
        name="out_proj",
    )(a, w, x)


def _softmax_rows(s, mask):
    s = jnp.where(mask, s, NEG)
    m = jnp.max(s, axis=-1, keepdims=True)
    e = jnp.where(mask, jnp.exp(s - m), 0.0)
    d = jnp.sum(e, axis=-1, keepdims=True)
    return e / jnp.where(d > 0.0, d, 1.0)


def _compress_combine(acc, const, nvalid):
    nsub = acc.shape[0]
    top = acc[:, :LANES]
    bot = pltpu.roll(acc[:, LANES:], nsub - 1, 0)
    row = lax.broadcasted_iota(jnp.int32, (nsub, LANES), 0)
    return jnp.where(row < nvalid, top + bot + const, 0.0)


def _pe_const(pe_ref, w_ref):
    return jnp.dot(pe_ref[...], w_ref[...], precision=lax.Precision.HIGHEST,
                   preferred_element_type=F32)[0:1, :]


def _cmp_prompt_kernel(k_ref, v_ref, wk_ref, wv_ref, pek_ref, pev_ref, wkf_ref, wvf_ref, g_ref,
                       ko_ref, vo_ref):
    nsub = ko_ref.shape[0]
    acck = jnp.zeros((nsub, 2 * HEAD_DIM), F32)
    accv = jnp.zeros((nsub, 2 * HEAD_DIM), F32)
    for l in range(CMP_STRIDE):
        xk = k_ref[pl.ds(l, nsub, stride=CMP_STRIDE), :].astype(BF16)
        xv = v_ref[pl.ds(l, nsub, stride=CMP_STRIDE), :].astype(BF16)
        acck = acck + _dot(xk, wk_ref[l])
        accv = accv + _dot(xv, wv_ref[l])
    kc = _compress_combine(acck, _pe_const(pek_ref, wkf_ref), nsub - 1)
    ms = jnp.mean(kc * kc, axis=-1, keepdims=True)
    ko_ref[...] = (kc * lax.rsqrt(ms + EPS)) * g_ref[...]
    vo_ref[...] = _compress_combine(accv, _pe_const(pev_ref, wvf_ref), nsub - 1)


def _cmp_weights(w_cmp, pe):
    wcat = jnp.concatenate([w_cmp[:CMP_STRIDE], w_cmp[CMP_STRIDE:]], axis=2).astype(BF16)
    pe8 = jnp.broadcast_to(pe.reshape(1, CMP_BLK * HEAD_DIM), (8, CMP_BLK * HEAD_DIM))
    return wcat, pe8, w_cmp.reshape(CMP_BLK * HEAD_DIM, HEAD_DIM)


def compress_prompt(zf, b, t, wk, wv, gk):
    nsub = t // CMP_STRIDE
    wkc, pek, wkf = wk
    wvc, pev, wvf = wv
    full = lambda a: pl.BlockSpec(a.shape, lambda i, g: (0,) * a.ndim)
    head = lambda off: pl.BlockSpec((t, HEAD_DIM), lambda i, g: (i, off // HEAD_DIM + g))
    out = jax.ShapeDtypeStruct((b * N_KV_HEADS, nsub, HEAD_DIM), F32)
    return pl.pallas_call(
        _cmp_prompt_kernel,
        grid=(b, N_KV_HEADS),
        in_specs=[head(NSA_W), head(NSA_W + KV_W),
                  full(wkc), full(wvc), full(pek), full(pev), full(wkf), full(wvf), full(gk)],
        out_specs=[pl.BlockSpec((None, nsub, HEAD_DIM), lambda i, g: (i * N_KV_HEADS + g, 0, 0))] * 2,
        out_shape=[out, out],
        compiler_params=_cparams(("parallel", "parallel")),
        name="cmp_prompt",
    )(zf, zf, wkc, wvc, pek, pev, wkf, wvf, gk)


def _sel_matrix(nc_pad, nc, ns_pad, ns):
    c0 = np.arange(nc_pad) * CMP_STRIDE
    s0 = np.arange(ns_pad) * SEL_BLK
    m = (c0[:, None] < s0[None, :] + SEL_BLK) & (c0[:, None] + CMP_BLK > s0[None, :])
    m &= (np.arange(nc_pad)[:, None] < nc) & (np.arange(ns_pad)[None, :] < ns)
    return jnp.asarray(m.astype(np.float32))


EXP_C = SCALE * 1.4426950408889634


def _select_mask_t(imp_t, tpos_row, ns):
    blk = lax.broadcasted_iota(jnp.int32, imp_t.shape, 0)
    cur = tpos_row // SEL_BLK
    forced = (blk == 0) | (blk == cur) | (blk == cur - 1)
    impm = jnp.where(blk > cur, -BIG, jnp.where(forced, BIG, imp_t))
    rank = jnp.zeros(imp_t.shape, F32)
    for s2 in range(ns):
        row = impm[s2:s2 + 1, :]
        gt = jnp.where(row > impm, 1.0, 0.0)
        ge = jnp.where(row >= impm, 1.0, 0.0)
        rank = rank + jnp.where(blk > s2, ge, gt)
    return jnp.where(rank < float(min(N_SEL, ns)), 1.0, 0.0)


def _softmax_bias(s3, bias, valid):
    sm = s3 + bias[None]
    m = jnp.max(sm, axis=-1, keepdims=True)
    e = jnp.exp2((sm - m) * EXP_C)
    inv = 1.0 / jnp.sum(e, axis=-1, keepdims=True)
    if valid is not None:
        inv = jnp.where(valid[None], inv, 0.0)
    return e * inv


def _attn_prompt_kernel(q_ref, kc_ref, vc_ref, ks_ref, vs_ref, kw_ref, vw_ref, gn_ref, gate_ref,
                        msel_ref, e_ref, o_ref, *, tq, kvc, ns):
    g = pl.program_id(1)
    t0 = pl.program_id(2) * tq
    nh = Q_PER_KV
    rows = nh * tq
    q4 = jnp.concatenate([q_ref[:, h * HEAD_DIM:(h + 1) * HEAD_DIM] for h in range(nh)], axis=0)
    tpos = t0 + lax.broadcasted_iota(jnp.int32, (tq, 1), 0)
    tpos_row = t0 + lax.broadcasted_iota(jnp.int32, (1, tq), 1)

    ncp = kc_ref.shape[0]
    endpos = lax.broadcasted_iota(jnp.int32, (1, ncp), 1) * CMP_STRIDE + (CMP_BLK - 1)
    bias_c = jnp.where(endpos <= tpos, 0.0, NEG)
    p3 = _softmax_bias(_dot_nt(q4, kc_ref[...].astype(BF16)).reshape(nh, tq, ncp), bias_c,
                       tpos >= CMP_BLK - 1)
    o_cmp = _dot(p3.reshape(rows, ncp).astype(BF16), vc_ref[...].astype(BF16))
    imp = jnp.dot(jnp.sum(p3, axis=0), msel_ref[...], precision=lax.Precision.HIGHEST,
                  preferred_element_type=F32)
    ns8 = -(-ns // 8) * 8
    sel_t = _select_mask_t(imp.T[:ns8], tpos_row, ns)
    sel = jnp.concatenate([sel_t, jnp.zeros((imp.shape[1] - ns8, tq), F32)], axis=0).T.astype(BF16)

    def body(c, carry):
        m, l, acc = carry
        k0 = pl.multiple_of(c * kvc, kvc)
        kpos = k0 + lax.broadcasted_iota(jnp.int32, (1, kvc), 1)
        bias = jnp.where(kpos <= tpos, (_dot(sel, e_ref[c]) - 1.0) * -NEG, NEG)
        sm = _dot_nt(q4, ks_ref[pl.ds(k0, kvc), :]).reshape(nh, tq, kvc) + bias[None]
        m_new = jnp.maximum(m, jnp.max(sm, axis=-1, keepdims=True))
        alpha = jnp.exp2((m - m_new) * EXP_C)
        e = jnp.exp2((sm - m_new) * EXP_C)
        l = alpha * l + jnp.sum(e, axis=-1, keepdims=True)
        pv = _dot(e.reshape(rows, kvc).astype(BF16), vs_ref[pl.ds(k0, kvc), :])
        return m_new, l, alpha * acc + pv.reshape(nh, tq, HEAD_DIM)

    nch = (t0 + tq + kvc - 1) // kvc
    m, l, acc = lax.fori_loop(0, nch, body, (jnp.full((nh, tq, 1), NEG, F32), jnp.zeros((nh, tq, 1), F32),
                                             jnp.zeros((nh, tq, HEAD_DIM), F32)))
    o_sel = (acc * jnp.where(l > 0.0, 1.0 / l, 0.0)).reshape(rows, HEAD_DIM)

    nwin = WINDOW + tq
    w0 = pl.multiple_of(jnp.maximum(t0 - WINDOW, 0), tq)
    diff = tpos - (w0 + lax.broadcasted_iota(jnp.int32, (1, nwin), 1))
    bias_w = jnp.where((diff >= 0) & (diff <= WINDOW), 0.0, NEG)
    pw = _softmax_bias(_dot_nt(q4, kw_ref[pl.ds(w0, nwin), :]).reshape(nh, tq, nwin), bias_w, None)
    o_win = _dot(pw.reshape(rows, nwin).astype(BF16), vw_ref[pl.ds(w0, nwin), :])

    gs = _sigmoid(gn_ref[...])
    lane = lax.broadcasted_iota(jnp.int32, gs.shape, 1)
    for h in range(nh):
        r = slice(h * tq, (h + 1) * tq)
        o = None
        for br, ob in enumerate((o_cmp, o_sel, o_win)):
            cidx = br * N_HEADS + g * nh + h
            gcol = jnp.sum(jnp.where(lane == cidx, gs, 0.0), axis=-1, keepdims=True)
            o = gcol * ob[r] if o is None else o + gcol * ob[r]
        hs = slice(h * HEAD_DIM, (h + 1) * HEAD_DIM)
        o_ref[:, hs] = (o * gate_ref[:, hs].astype(F32)).astype(o_ref.dtype)


def attn_prompt(zb, kcmp, vcmp, gn, gate, b, t):
    tq, kvc = 128, 512
    assert t // CMP_STRIDE == LANES and t % kvc == 0 and t >= WINDOW + tq
    nq = t // tq
    ns = -(-t // SEL_BLK)
    nc = t // CMP_STRIDE - CMP_BLK // CMP_STRIDE + 1
    msel = _sel_matrix(LANES, nc, LANES, ns)
    kk = np.arange(t)
    e3 = (kk[None, :] // SEL_BLK == np.arange(LANES)[:, None]).astype(np.float32)
    e3 = jnp.asarray(e3.reshape(LANES, t // kvc, kvc).transpose(1, 0, 2), BF16)
    hb = lambda off: (lambda bi, g, i: (bi, off // HEAD_DIM + g))
    kv_spec = lambda off: pl.BlockSpec((t, HEAD_DIM), hb(off))
    cm_spec = pl.BlockSpec((None, t // CMP_STRIDE, HEAD_DIM), lambda bi, g, i: (bi * N_KV_HEADS + g, 0, 0))
    row4 = pl.BlockSpec((tq, Q_PER_KV * HEAD_DIM), lambda bi, g, i: (bi * nq + i, g))
    return pl.pallas_call(
        functools.partial(_attn_prompt_kernel, tq=tq, kvc=kvc, ns=ns),
        grid=(b, N_KV_HEADS, nq),
        in_specs=[row4, cm_spec, cm_spec,
                  kv_spec(NSA_W + 2 * KV_W), kv_spec(NSA_W + 3 * KV_W),
                  kv_spec(NSA_W + 4 * KV_W), kv_spec(NSA_W + 5 * KV_W),
                  pl.BlockSpec((tq, LANES), lambda bi, g, i: (bi * nq + i, 0)),
                  row4,
                  pl.BlockSpec(msel.shape, lambda bi, g, i: (0, 0)),
                  pl.BlockSpec(e3.shape, lambda bi, g, i: (0, 0, 0))],
        out_specs=row4,
        out_shape=jax.ShapeDtypeStruct((b * t, NSA_W), BF16),
        compiler_params=_cparams(("parallel", "parallel", "arbitrary")),
        name="attn_prompt",
    )(zb, kcmp, vcmp, zb, zb, zb, zb, gn, gate, msel, e3)


def _layer_norm(v, g, b):
    vc = v - jnp.mean(v, axis=-1, keepdims=True)
    var = jnp.mean(vc * vc, axis=-1, keepdims=True)
    return vc * lax.rsqrt(var + EPS) * g + b


def _mlp_prompt_kernel(u_ref, v_ref, gate_ref, lg_ref, lb_ref, ws_ref, bst_ref, o_ref):
    tm = u_ref.shape[0]
    gw = ws_ref.shape[1]
    vb = _layer_norm(v_ref[...], lg_ref[...], lb_ref[...]).astype(BF16)
    row = lax.broadcasted_iota(jnp.int32, (CHUNK, CHUNK), 0)
    col = lax.broadcasted_iota(jnp.int32, (CHUNK, CHUNK), 1)
    for gi in range(MLP_GROUPS):
        cs = slice(gi * gw, (gi + 1) * gw)
        wsg = jnp.where(row >= col, ws_ref[gi], 0.0).astype(BF16)
        bias = bst_ref[:, gi:gi + 1]
        for ch in range(tm // CHUNK):
            rs = slice(ch * CHUNK, (ch + 1) * CHUNK)
            mixed = _dot(wsg, vb[rs, cs]) + bias
            o_ref[rs, cs] = (u_ref[rs, cs] * mixed * gate_ref[rs, cs].astype(F32)).astype(o_ref.dtype)


def mlp_prompt(uv, gate, ln_g, ln_b, w_s, b_s, tm):
    m, w2 = uv.shape
    w = w2 // 2
    assert w // MLP_GROUPS == CHUNK == w_s.shape[1]
    full = lambda a: pl.BlockSpec(a.shape, lambda i: (0,) * a.ndim)
    bst = b_s.T
    return pl.pallas_call(
        _mlp_prompt_kernel,
        grid=(m // tm,),
        in_specs=[pl.BlockSpec((tm, w), lambda i: (i, 0)), pl.BlockSpec((tm, w), lambda i: (i, 1)),
                  pl.BlockSpec((tm, w), lambda i: (i, 0)),
                  pl.BlockSpec((1, w), lambda i: (0, 0)), pl.BlockSpec((1, w), lambda i: (0, 0)),
                  full(w_s), full(bst)],
        out_specs=pl.BlockSpec((tm, w), lambda i: (i, 0)),
        out_shape=jax.ShapeDtypeStruct((m, w), BF16),
        compiler_params=_cparams(("parallel",)),
        name="mlp_prompt",
    )(uv, uv, gate, ln_g.reshape(1, w), ln_b.reshape(1, w), w_s, bst)


def _softplus(x):
    return jnp.maximum(x, 0.0) + jnp.log1p(jnp.exp(-jnp.abs(x)))


def _lru_gates(xc, wa_ref, wx_ref, ba, bx, lam):
    nh = xc.shape[1] // HEAD_DIM
    rs, is_ = [], []
    for hh in range(nh):
        xh = xc[:, hh * HEAD_DIM:(hh + 1) * HEAD_DIM].astype(BF16)
        rs.append(_dot(xh, wa_ref[hh]))
        is_.append(_dot(xh, wx_ref[hh]))
    r = _sigmoid(jnp.concatenate(rs, axis=1) + ba)
    i = _sigmoid(jnp.concatenate(is_, axis=1) + bx)
    log_a = -LRU_C * r * _softplus(-lam)
    th = jnp.tanh(log_a)
    return jnp.exp(log_a), jnp.sqrt(-2.0 * th / (1.0 - th)) * (i * xc)


def _lru_prompt_kernel(x_ref, gate_ref, cw_ref, cb_ref, wa_ref, wx_ref, ba_ref, bx_ref, lam_ref,
                       o_ref, h_ref, a_scr, u_scr):
    t, wb = x_ref.shape
    x = x_ref[...]
    row = lax.broadcasted_iota(jnp.int32, (t, wb), 0)
    xc = cb_ref[...] + x * cw_ref[CONV_W - 1:CONV_W, :]
    for d in range(1, CONV_W):
        xs = jnp.where(row >= d, pltpu.roll(x, d, 0), 0.0)
        xc = xc + xs * cw_ref[CONV_W - 1 - d:CONV_W - d, :]
    a, u = _lru_gates(xc, wa_ref, wx_ref, ba_ref[...], bx_ref[...], lam_ref[...])
    a_scr[...] = a
    u_scr[...] = u
    row8 = lax.broadcasted_iota(jnp.int32, (8, wb), 0)

    def body(bi, h):
        r0 = pl.multiple_of(bi * 8, 8)
        a8 = a_scr[pl.ds(r0, 8), :]
        u8 = u_scr[pl.ds(r0, 8), :]
        for d in (1, 2, 4):
            a_sh = jnp.where(row8 >= d, pltpu.roll(a8, d, 0), 1.0)
            u_sh = jnp.where(row8 >= d, pltpu.roll(u8, d, 0), 0.0)
            u8 = a8 * u_sh + u8
            a8 = a8 * a_sh
        h8 = a8 * h + u8
        u_scr[pl.ds(r0, 8), :] = h8
        return h8[7:8, :]

    h = lax.fori_loop(0, t // 8, body, jnp.zeros((1, wb), F32))
    h_ref[...] = h
    o_ref[...] = (u_scr[...] * gate_ref[...].astype(F32)).astype(o_ref.dtype)


def lru_prompt(xl, gate, conv_w, conv_b, wa, wx, ba, bx, lam, b, t):
    w = xl.shape[1]
    wb = 512
    nh = wb // HEAD_DIM
    assert w % wb == 0 and t % 8 == 0 and wa.shape[1] == HEAD_DIM
    vec = lambda a: a.reshape(1, w)
    vspec = pl.BlockSpec((1, wb), lambda i, j: (0, j))
    blk = pl.BlockSpec((t, wb), lambda i, j: (i, j))
    hspec = pl.BlockSpec((nh, HEAD_DIM, HEAD_DIM), lambda i, j: (j, 0, 0))
    return pl.pallas_call(
        _lru_prompt_kernel,
        grid=(b, w // wb),
        in_specs=[blk, blk, pl.BlockSpec((CONV_W, wb), lambda i, j: (0, j)), vspec,
                  hspec, hspec, vspec, vspec, vspec],
        out_specs=[blk, pl.BlockSpec((None, 1, wb), lambda i, j: (i, 0, j))],
        out_shape=[jax.ShapeDtypeStruct((b * t, w), BF16), jax.ShapeDtypeStruct((b, 1, w), F32)],
        scratch_shapes=[pltpu.VMEM((t, wb), F32), pltpu.VMEM((t, wb), F32)],
        compiler_params=_cparams(("parallel", "parallel")),
        name="lru_prompt",
    )(xl, gate, conv_w, vec(conv_b), wa.astype(BF16), wx.astype(BF16), vec(ba), vec(bx), vec(lam))


ATT_W = NSA_W + 6 * KV_W
SEG = 512


def _pack_kernel(a_ref, b_ref, o_ref, *, gn_tile, ngate):
    j = pl.program_id(1)

    @pl.when(j < gn_tile)
    def _():
        o_ref[...] = a_ref[...].astype(o_ref.dtype)

    @pl.when(j == gn_tile)
    def _():
        lane = lax.broadcasted_iota(jnp.int32, a_ref.shape, 1)
        o_ref[...] = jnp.where(lane < ngate, a_ref[...], 0.0).astype(o_ref.dtype)

    @pl.when(j > gn_tile)
    def _():
        x = jnp.concatenate([a_ref[...], b_ref[...]], axis=1)
        o_ref[...] = x[:, ngate:ngate + o_ref.shape[1]].astype(o_ref.dtype)


def _pack_w_in(w_in, l, d_model, mlp_w, lru_w):
    ngate = N_BRANCH * N_HEADS
    sizes = (("att", ATT_W), ("gn", SEG), ("gate_nsa", NSA_W), ("uv", 2 * mlp_w), ("gate_mlp", mlp_w),
             ("xl", lru_w), ("gate_lru", lru_w), ("gm", N_BRANCH * d_model))
    offs, acc = {}, 0
    for name, size in sizes:
        offs[name] = (acc, size)
        acc += size
    k, n_in = w_in.shape[1:]
    assert acc == n_in - ngate + SEG and ATT_W % SEG == 0 and ngate <= LANES
    gn_tile = ATT_W // SEG
    tr = 1024
    w = pl.pallas_call(
        functools.partial(_pack_kernel, gn_tile=gn_tile, ngate=ngate),
        grid=(k // tr, acc // SEG),
        in_specs=[pl.BlockSpec((None, tr, SEG), lambda r, j: (l, r, jnp.where(j > gn_tile, j - 1, j))),
                  pl.BlockSpec((None, tr, LANES), lambda r, j: (l, r, (SEG // LANES) * j))],
        out_specs=pl.BlockSpec((tr, SEG), lambda r, j: (r, j)),
        out_shape=jax.ShapeDtypeStruct((k, acc), BF16),
        compiler_params=_cparams(("parallel", "arbitrary")),
        name="pack_w",
    )(w_in, w_in)
    return w, offs


def _norm_vectors(q_g, k_g):
    one = jnp.ones((KV_W,), F32)
    zero = jnp.zeros((KV_W,), F32)
    gain = jnp.concatenate([jnp.tile(q_g, N_HEADS), one, one, jnp.tile(k_g[1], N_KV_HEADS), one,
                            jnp.tile(k_g[2], N_KV_HEADS), one])
    flag = jnp.concatenate([jnp.ones((NSA_W,), F32), zero, zero, one, zero, one, zero])
    return gain.reshape(1, ATT_W), flag.reshape(1, ATT_W)


def _in_proj(x2d, lw, tm):
    h = rms_rows(x2d, lw["norm_g"], min(tm, 512))
    w, offs = lw["w_pack"]
    p = lambda name, act, dt, tn=SEG, n=None, **kw: proj(
        h, w, offs[name][0], n or offs[name][1], act, dt, tm, tn, **kw)
    zf, zb = p("att", "norm", None, gain=lw["gain"], flag=lw["flag"])
    return dict(zf=zf, zb=zb, gn=p("gn", "none", F32, tn=LANES, n=LANES),
                gate_nsa=p("gate_nsa", "silu", BF16), uv=p("uv", "gelu", F32),
                gate_mlp=p("gate_mlp", "silu", BF16), xl=p("xl", "none", F32),
                gate_lru=p("gate_lru", "silu", BF16), gm=p("gm", "sigmoid", BF16))


def prompt_layer(x2d, lw, b, t, w_buf):
    tm = 1024
    z = _in_proj(x2d, lw, tm)
    kcmp, vcmp = compress_prompt(z["zf"], b, t, lw["cmp_k"], lw["cmp_v"], lw["k_norm_g"][0:1])
    o_nsa = attn_prompt(z["zb"], kcmp, vcmp, z["gn"], z["gate_nsa"], b, t)
    o_mlp = mlp_prompt(z["uv"], z["gate_mlp"], lw["mlp_ln_g"], lw["mlp_ln_b"], lw["w_s"], lw["b_s"], 512)
    o_lru, h_last = lru_prompt(z["xl"], z["gate_lru"], lw["conv_w"], lw["conv_b"], lw["lru_wa"], lw["lru_wx"],
                               lw["lru_ba"], lw["lru_bx"], lw["lru_lambda"], b, t)
    m = merge([o_nsa, o_mlp, o_lru], lw["w_br"], z["gm"], tm, SEG)
    y = out_proj(m, lw["w_out"], x2d, tm, SEG)
    zf = z["zf"].reshape(b, t, ATT_W)
    kv = lambda i: zf[:, :, NSA_W + i * KV_W:NSA_W + (i + 1) * KV_W].reshape(b, t, N_KV_HEADS, HEAD_DIM)
    xl = z["xl"].reshape(b, t, -1)
    assert t >= w_buf and t >= CONV_W - 1
    state = dict(cmp_k=kv(0), cmp_v=kv(1), sel_k=kv(2), sel_v=kv(3),
                 win_k=kv(4)[:, t - w_buf:], win_v=kv(5)[:, t - w_buf:],
                 lru_h=h_last.reshape(b, -1), lru_conv=xl[:, t - (CONV_W - 1):])
    return y, state


PAGES_PER_STEP = 32


def _cmp_paged_kernel(pt_ref, *refs, pg):
    k_refs, v_refs = refs[:pg], refs[pg:2 * pg]
    wk_ref, wv_ref, ko_ref, vo_ref, r_scr = refs[2 * pg:]
    rows = k_refs[0].shape[0]
    sub = N_KV_HEADS * CMP_STRIDE
    nsp = rows // sub
    for page_refs, w_ref, o_ref in ((k_refs, wk_ref, ko_ref), (v_refs, wv_ref, vo_ref)):
        blocks = [jnp.concatenate([r[pl.ds(sub * i + 8 * m, 8), :] for m in range(sub // 8)], axis=1)
                  for r in page_refs for i in range(nsp)]
        res = _dot(jnp.concatenate(blocks, axis=0).astype(BF16), w_ref[...])
        n = res.shape[0]
        both = res[:, :2 * HEAD_DIM] + pltpu.roll(res[:, 2 * HEAD_DIM:], n - N_KV_HEADS, 0)
        r_scr[0] = both[:, :HEAD_DIM]
        r_scr[1] = both[:, HEAD_DIM:]
        for g in range(N_KV_HEADS):
            o_ref[g] = jnp.concatenate([r_scr[0, pl.ds(g, n // 8, stride=8), :],
                                        r_scr[1, pl.ds(g, n // 8, stride=8), :]], axis=1)


def _cmp_paged_weights(wcat):
    return wcat.reshape(CMP_STRIDE // 2, 2, HEAD_DIM, 2 * HEAD_DIM).transpose(0, 2, 1, 3).reshape(
        CMP_STRIDE // 2 * HEAD_DIM, 4 * HEAD_DIM)


def compress_paged(pool_k, pool_v, page_table, wkc, wvc, page0):
    b, n_pages = page_table.shape
    rows = pool_k.shape[1]
    pg = PAGES_PER_STEP
    assert n_pages % pg == 0 and N_KV_HEADS * 2 == 8
    nsp = rows // N_KV_HEADS // CMP_STRIDE
    wkc, wvc = _cmp_paged_weights(wkc), _cmp_paged_weights(wvc)
    page_spec = lambda p: pl.BlockSpec((None, rows, HEAD_DIM), lambda i, c, pt: (page0 + pt[i, c * pg + p], 0, 0))
    full = lambda a: pl.BlockSpec(a.shape, lambda i, c, pt: (0,) * a.ndim)
    out = jax.ShapeDtypeStruct((b, N_KV_HEADS, n_pages * nsp, 2 * HEAD_DIM), F32)
    ospec = pl.BlockSpec((None, N_KV_HEADS, pg * nsp, 2 * HEAD_DIM), lambda i, c, pt: (i, 0, c, 0))
    return pl.pallas_call(
        functools.partial(_cmp_paged_kernel, pg=pg),
        grid_spec=pltpu.PrefetchScalarGridSpec(
            num_scalar_prefetch=1,
            grid=(b, n_pages // pg),
            in_specs=[page_spec(p) for p in range(pg)] * 2 + [full(wkc), full(wvc)],
            out_specs=[ospec, ospec],
            scratch_shapes=[pltpu.VMEM((2, pg * nsp * 8, HEAD_DIM), F32)]),
        out_shape=[out, out],
        compiler_params=_cparams(("parallel", "arbitrary")),
        name="cmp_paged",
    )(page_table, *([pool_k] * pg), *([pool_v] * pg), wkc, wvc)


def _cmp_sample_kernel(tbk_ref, tbv_ref, q_ref, pek_ref, pev_ref, wkf_ref, wvf_ref, gk_ref, msel_ref,
                       o_ref, idx_ref, *, tpos, nc, ns):
    ck = _pe_const(pek_ref, wkf_ref)
    cv = _pe_const(pev_ref, wvf_ref)
    q = q_ref[...].astype(BF16)
    nh = q.shape[0]
    nsub = tbk_ref.shape[1]
    hrow = lax.broadcasted_iota(jnp.int32, (nh, 1), 0)
    blk = lax.broadcasted_iota(jnp.int32, (1, nsub), 1)
    cmask = (blk * CMP_STRIDE + (CMP_BLK - 1) <= tpos) & (blk < nc)
    row8 = lax.broadcasted_iota(jnp.int32, (8, nsub), 0)
    o = jnp.zeros((nh, HEAD_DIM), F32)
    psum = jnp.zeros((8, nsub), F32)
    for g in range(N_KV_HEADS):
        kc = _compress_combine(tbk_ref[g], ck, nc)
        ms = jnp.mean(kc * kc, axis=-1, keepdims=True)
        kc = (kc * lax.rsqrt(ms + EPS)) * gk_ref[...]
        vc = _compress_combine(tbv_ref[g], cv, nc)
        p = _softmax_rows(_dot_nt(q, kc.astype(BF16)) * SCALE, cmask & (hrow // Q_PER_KV == g))
        o = o + _dot(p.astype(BF16), vc.astype(BF16))
        psum = jnp.where(row8 == g, jnp.sum(p, axis=0, keepdims=True), psum)
    o_ref[...] = o
    imp = jnp.dot(psum, msel_ref[...], precision=lax.Precision.HIGHEST, preferred_element_type=F32)
    lane = lax.broadcasted_iota(jnp.int32, imp.shape, 1)
    cur = tpos // SEL_BLK
    forced = (lane == 0) | (lane == cur) | (lane == cur - 1)
    impm = jnp.where(lane > cur, -BIG, jnp.where(forced, BIG, imp))
    impm = jnp.where(lane < ns, impm, -jnp.inf)
    lane_f = lane.astype(F32)
    out_lane = lax.broadcasted_iota(jnp.int32, idx_ref.shape, 1)
    idxs = jnp.zeros(idx_ref.shape, F32)
    for j in range(min(N_SEL, ns)):
        mx = jnp.max(impm, axis=-1, keepdims=True)
        am = jnp.min(jnp.where(impm == mx, lane_f, 1e9), axis=-1, keepdims=True)
        idxs = jnp.where(out_lane == j, am, idxs)
        impm = jnp.where(lane_f == am, -jnp.inf, impm)
    idx_ref[...] = idxs.astype(jnp.int32)


def cmp_sample(tbk, tbv, q3, ck, cv, gk, tpos):
    b, nkv, nsub, _ = tbk.shape
    nc = (tpos + 1) // CMP_STRIDE - CMP_BLK // CMP_STRIDE + 1
    ns = -(-(tpos + 1) // SEL_BLK)
    ns_pad = -(-ns // LANES) * LANES
    assert nc <= nsub
    msel = _sel_matrix(nsub, nc, ns_pad, ns)
    _, pek, wkf = ck
    _, pev, wvf = cv
    full = lambda a: pl.BlockSpec(a.shape, lambda i: (0,) * a.ndim)
    tb_spec = pl.BlockSpec((None, nkv, nsub, 2 * HEAD_DIM), lambda i: (i, 0, 0, 0))
    return pl.pallas_call(
        functools.partial(_cmp_sample_kernel, tpos=tpos, nc=nc, ns=ns),
        grid=(b,),
        in_specs=[tb_spec, tb_spec, pl.BlockSpec((None, N_HEADS, HEAD_DIM), lambda i: (i, 0, 0)),
                  full(pek), full(pev), full(wkf), full(wvf), full(gk), full(msel)],
        out_specs=[pl.BlockSpec((None, N_HEADS, HEAD_DIM), lambda i: (i, 0, 0)),
                   pl.BlockSpec((None, 8, LANES), lambda i: (i, 0, 0))],
        out_shape=[jax.ShapeDtypeStruct((b, N_HEADS, HEAD_DIM), F32),
                   jax.ShapeDtypeStruct((b, 8, LANES), jnp.int32)],
        compiler_params=_cparams(("parallel",)),
        name="cmp_sample",
    )(tbk, tbv, q3, pek, pev, wkf, wvf, gk, msel)


def _attend_with_new(q, k_all, v_all, mask, k_new, v_new, new_ok):
    s = jnp.where(mask, _dot_nt(q.astype(BF16), k_all) * SCALE, NEG)
    s_new = jnp.where(new_ok, jnp.sum(q * k_new, axis=-1, keepdims=True) * SCALE, NEG)
    m = jnp.maximum(jnp.max(s, axis=-1, keepdims=True), s_new)
    e = jnp.where(mask, jnp.exp(s - m), 0.0)
    e_new = jnp.where(new_ok, jnp.exp(s_new - m), 0.0)
    d = jnp.sum(e, axis=-1, keepdims=True) + e_new
    o = _dot(e.astype(BF16), v_all) + e_new * v_new
    return o / jnp.where(d > 0.0, d, 1.0)


def _attn_sample_kernel(pt_ref, idx_ref, *refs, nblk, tpos, past_len, w_buf):
    k_refs, v_refs = refs[:nblk], refs[nblk:2 * nblk]
    (q_ref, ocmp_ref, kw_ref, vw_ref, ksn_ref, vsn_ref, kwn_ref, vwn_ref, gn_ref, gate_ref,
     o_ref) = refs[2 * nblk:]
    b = pl.program_id(0)
    g = pl.program_id(1)
    q = q_ref[...]
    rb = k_refs[0].shape[0]
    r = lax.broadcasted_iota(jnp.int32, (1, rb), 1)
    tok, hd = r // N_KV_HEADS, r % N_KV_HEADS
    masks = []
    new_sel = False
    for j in range(nblk):
        s = idx_ref[b, g * nblk + j]
        kpos = s * SEL_BLK + tok
        masks.append((hd == g) & (kpos <= tpos) & (kpos < past_len))
        new_sel = new_sel | (s == past_len // SEL_BLK)
    k_all = jnp.concatenate([kr[...].astype(BF16) for kr in k_refs], axis=0)
    v_all = jnp.concatenate([vr[...].astype(BF16) for vr in v_refs], axis=0)
    o_sel = _attend_with_new(q, k_all, v_all, jnp.concatenate(masks, axis=1),
                             ksn_ref[...], vsn_ref[...], new_sel & (past_len <= tpos))
    rw = lax.broadcasted_iota(jnp.int32, (1, kw_ref.shape[0]), 1)
    diff = tpos - (past_len - w_buf + rw // N_KV_HEADS)
    wmask = (rw % N_KV_HEADS == g) & (diff >= 0) & (diff <= WINDOW)
    o_win = _attend_with_new(q, kw_ref[...].astype(BF16), vw_ref[...].astype(BF16), wmask,
                             kwn_ref[...], vwn_ref[...], tpos - past_len <= WINDOW)
    gs = _sigmoid(gn_ref[...])
    o = gs[:, 0:1] * ocmp_ref[...] + gs[:, 1:2] * o_sel + gs[:, 2:3] * o_win
    o_ref[...] = o * gate_ref[...]


def attn_sample(pool_k, pool_v, page_table, idx, q4, ocmp4, win_k, win_v, ks_new, vs_new, kw_new, vw_new,
                gn4, gate4, past_len, w_buf, page_size, page0, seq0):
    b, n_pages = page_table.shape
    nblk = idx.shape[1] // N_KV_HEADS
    rb = pool_k.shape[1]
    halves = page_size // SEL_BLK
    tpos = past_len

    def blk_map(j):
        def f(i, g, pt, ix):
            s = ix[i, g * nblk + j]
            page = pt[i, jnp.minimum(s // halves, n_pages - 1)]
            return ((page0 + page) * halves + s % halves, 0, 0)
        return f

    blk_spec = lambda j: pl.BlockSpec((None, rb, HEAD_DIM), blk_map(j))
    per_bg = lambda a: pl.BlockSpec((None, None) + a.shape[2:], lambda i, g, pt, ix: (i, g, 0, 0))
    per_b = lambda a: pl.BlockSpec((None,) + a.shape[1:], lambda i, g, pt, ix: (seq0 + i, 0, 0))
    small = (q4, ocmp4)
    news = (ks_new, vs_new, kw_new, vw_new, gn4, gate4)
    return pl.pallas_call(
        functools.partial(_attn_sample_kernel, nblk=nblk, tpos=tpos, past_len=past_len, w_buf=w_buf),
        grid_spec=pltpu.PrefetchScalarGridSpec(
            num_scalar_prefetch=2,
            grid=(b, N_KV_HEADS),
            in_specs=[blk_spec(j) for j in range(nblk)] * 2 + [per_bg(a) for a in small]
            + [per_b(win_k), per_b(win_v)] + [per_bg(a) for a in news],
            out_specs=per_bg(q4)),
        out_shape=jax.ShapeDtypeStruct(q4.shape, F32),
        compiler_params=_cparams(("parallel", "arbitrary")),
        name="attn_sample",
    )(page_table, idx, *([pool_k] * nblk), *([pool_v] * nblk), q4, ocmp4, win_k, win_v, *news)


def _point_sample_kernel(u_ref, v_ref, gmlp_ref, lg_ref, lb_ref, ws0_ref, bs0_ref,
                         x_ref, glru_ref, buf_ref, h0_ref, cw_ref, cb_ref, wa_ref, wx_ref, ba_ref, bx_ref,
                         lam_ref, omlp_ref, vrow_ref, olru_ref, hnew_ref):
    vn = _layer_norm(v_ref[...], lg_ref[...], lb_ref[...])
    vrow_ref[...] = vn
    mixed = ws0_ref[...] * vn + bs0_ref[...]
    omlp_ref[...] = (u_ref[...] * mixed * gmlp_ref[...].astype(F32)).astype(omlp_ref.dtype)
    x = x_ref[...]
    xc = cb_ref[...] + x * cw_ref[CONV_W - 1:CONV_W, :]
    for j in range(CONV_W - 1):
        xc = xc + buf_ref[j] * cw_ref[j:j + 1, :]
    a, u = _lru_gates(xc, wa_ref, wx_ref, ba_ref[...], bx_ref[...], lam_ref[...])
    h = a * h0_ref[...] + u
    hnew_ref[...] = h
    olru_ref[...] = (h * glru_ref[...].astype(F32)).astype(olru_ref.dtype)


def point_sample(uv, gate_mlp, xl, gate_lru, buf_t, h0, lw):
    b, w2 = uv.shape
    w = w2 // 2
    gw = w // MLP_GROUPS
    vec = lambda a: a.reshape(1, -1)
    ws0 = vec(jnp.repeat(lw["w_s"][:, 0, 0], gw))
    bs0 = vec(jnp.repeat(lw["b_s"][:, 0], gw))
    args = (uv[:, :w], uv[:, w:], gate_mlp, vec(lw["mlp_ln_g"]), vec(lw["mlp_ln_b"]), ws0, bs0,
            xl, gate_lru, buf_t, h0, lw["conv_w"], vec(lw["conv_b"]), lw["lru_wa"].astype(BF16),
            lw["lru_wx"].astype(BF16), vec(lw["lru_ba"]), vec(lw["lru_bx"]), vec(lw["lru_lambda"]))
    lw_ = xl.shape[1]
    return pl.pallas_call(
        _point_sample_kernel,
        out_shape=[jax.ShapeDtypeStruct((b, w), BF16), jax.ShapeDtypeStruct((b, w), F32),
                   jax.ShapeDtypeStruct((b, lw_), BF16), jax.ShapeDtypeStruct((b, lw_), F32)],
        compiler_params=pltpu.CompilerParams(vmem_limit_bytes=VMEM_LIMIT),
        name="point_sample",
    )(*args)


def sample_layer(x2d, lw, l, caches, page_table, past_len, w_buf):
    b = x2d.shape[0]
    z = _in_proj(x2d, lw, b)
    zf = z["zf"]
    seg = lambda i: zf[:, NSA_W + i * KV_W:NSA_W + (i + 1) * KV_W]
    kvh = lambda a: a.reshape(b, N_KV_HEADS, 1, HEAD_DIM)
    n_pool, page_size = caches["cmp_k"].shape[1:3]
    page0 = l * n_pool
    pool3 = lambda a: a.reshape(-1, page_size * N_KV_HEADS, HEAD_DIM)
    halves = lambda a: a.reshape(-1, SEL_BLK * N_KV_HEADS, HEAD_DIM)
    tbk, tbv = compress_paged(pool3(caches["cmp_k"]), pool3(caches["cmp_v"]), page_table,
                              lw["cmp_k"][0], lw["cmp_v"][0], page0)
    q3 = zf[:, :NSA_W].reshape(b, N_HEADS, HEAD_DIM)
    o_cmp, idx = cmp_sample(tbk, tbv, q3, lw["cmp_k"], lw["cmp_v"], lw["k_norm_g"][0:1], past_len)
    n_sel = min(N_SEL, -(-(past_len + 1) // SEL_BLK))
    idx = idx[:, :N_KV_HEADS, :n_sel].reshape(b, N_KV_HEADS * n_sel)
    four = lambda a: a.reshape(b, N_KV_HEADS, Q_PER_KV, -1)
    gn4 = z["gn"][:, :N_BRANCH * N_HEADS].reshape(b, N_BRANCH, N_KV_HEADS, Q_PER_KV).transpose(0, 2, 3, 1)
    win3 = lambda a: a.reshape(-1, w_buf * N_KV_HEADS, HEAD_DIM)
    o_nsa = attn_sample(halves(caches["sel_k"]), halves(caches["sel_v"]), page_table, idx,
                        four(q3), four(o_cmp), win3(caches["win_k"]), win3(caches["win_v"]),
                        kvh(seg(2)), kvh(seg(3)), kvh(seg(4)), kvh(seg(5)),
                        gn4, four(z["gate_nsa"].astype(F32)), past_len, w_buf, page_size, page0, l * b)
    o_mlp, v_rows, o_lru, h_new = point_sample(
        z["uv"], z["gate_mlp"], z["xl"], z["gate_lru"], caches["lru_conv"].transpose(1, 0, 2),
        caches["lru_h"].astype(F32), lw)
    m = merge([o_nsa.reshape(b, NSA_W).astype(BF16), o_mlp, o_lru], lw["w_br"], z["gm"], b, SEG)
    y = out_proj(m, lw["w_out"], x2d, b, SEG)
    tok = lambda a: a.reshape(b, 1, N_KV_HEADS, HEAD_DIM)
    state = dict(cmp_k=tok(seg(0)), cmp_v=tok(seg(1)), sel_k=tok(seg(2)), sel_v=tok(seg(3)),
                 win_k=jnp.concatenate([caches["win_k"][l], tok(seg(4))], axis=1)[:, -w_buf:],
                 win_v=jnp.concatenate([caches["win_v"][l], tok(seg(5))], axis=1)[:, -w_buf:],
                 lru_h=h_new, lru_conv=jnp.concatenate([caches["lru_conv"], z["xl"][:, None]], axis=1)[:, 1:],
                 mlp_v=v_rows[:, None])
    return y, state


def _layer_weights(l, p):
    d_model = p["w_in"].shape[1]
    mlp_w = p["mlp_ln_g"].shape[1]
    lru_w = p["lru_lambda"].shape[1]
    names = ("norm_g", "q_norm_g", "k_norm_g", "mlp_ln_g", "mlp_ln_b", "w_s", "b_s", "conv_w", "conv_b",
             "lru_wa", "lru_ba", "lru_wx", "lru_bx", "lru_lambda")
    lw = {n: p[n][l] for n in names}
    lw["w_pack"] = _pack_w_in(p["w_in"], l, d_model, mlp_w, lru_w)
    lw["gain"], lw["flag"] = _norm_vectors(p["q_norm_g"][l], p["k_norm_g"][l])
    lw["cmp_k"] = _cmp_weights(p["w_cmp_k"][l], p["cmp_pe_k"][l])
    lw["cmp_v"] = _cmp_weights(p["w_cmp_v"][l], p["cmp_pe_v"][l])
    lw["w_br"] = [p[n][l].astype(BF16) for n in ("w_br_nsa", "w_br_mlp", "w_br_lru")]
    lw["w_out"] = p["w_out"][l].astype(BF16)
    return lw


def kernel(x_prompt, x_sample, cache_cmp_k, cache_cmp_v, cache_sel_k, cache_sel_v, state_win_k, state_win_v,
           state_lru_h, state_lru_conv, page_table, norm_g, w_in, q_norm_g, k_norm_g, cmp_pe_k, cmp_pe_v,
           w_cmp_k, w_cmp_v, mlp_ln_g, mlp_ln_b, w_s, b_s, conv_w, conv_b, lru_wa, lru_ba, lru_wx, lru_bx,
           lru_lambda, w_br_nsa, w_br_mlp, w_br_lru, w_out):
    params = dict(norm_g=norm_g, w_in=w_in, q_norm_g=q_norm_g, k_norm_g=k_norm_g, cmp_pe_k=cmp_pe_k,
                  cmp_pe_v=cmp_pe_v, w_cmp_k=w_cmp_k, w_cmp_v=w_cmp_v, mlp_ln_g=mlp_ln_g, mlp_ln_b=mlp_ln_b,
                  w_s=w_s, b_s=b_s, conv_w=conv_w, conv_b=conv_b, lru_wa=lru_wa, lru_ba=lru_ba, lru_wx=lru_wx,
                  lru_bx=lru_bx, lru_lambda=lru_lambda, w_br_nsa=w_br_nsa, w_br_mlp=w_br_mlp,
                  w_br_lru=w_br_lru, w_out=w_out)
    depth = w_in.shape[0]
    b, t, d = x_prompt.shape
    bs, ts, _ = x_sample.shape
    assert ts == 1
    w_buf = state_win_k.shape[2]
    past_len = page_table.shape[1] * cache_cmp_k.shape[2]
    yp = x_prompt.reshape(b * t, d)
    ys = x_sample.reshape(bs * ts, d)
    p_st, s_st = [], []
    for l in range(depth):
        lw = _layer_weights(l, params)
        yp, sp = prompt_layer(yp, lw, b, t, w_buf)
        caches = dict(cmp_k=cache_cmp_k, cmp_v=cache_cmp_v, sel_k=cache_sel_k, sel_v=cache_sel_v,
                      win_k=state_win_k, win_v=state_win_v, lru_h=state_lru_h[l],
                      lru_conv=state_lru_conv[l])
        ys, ss = sample_layer(ys, lw, l, caches, page_table, past_len, w_buf)
        p_st.append(sp)
        s_st.append(ss)
    stk = lambda sts, name: jnp.stack([st[name] for st in sts])
    names = ("cmp_k", "cmp_v", "sel_k", "sel_v", "win_k", "win_v", "lru_h", "lru_conv")
    return ((yp.reshape(b, t, d), ys.reshape(bs, ts, d))
            + tuple(stk(p_st, n) for n in names)
            + tuple(stk(s_st, n) for n in names + ("mlp_v",)))
```

```python
import functools

import numpy as np
import jax
import jax.numpy as jnp
from jax import lax
from jax.experimental import pallas as pl
from jax.experimental.pallas import tpu as pltpu

N_HEADS = 16
HEAD_DIM = 128
N_KV_HEADS = 4
Q_PER_KV = N_HEADS // N_KV_HEADS
NSA_W = N_HEADS * HEAD_DIM
KV_W = N_KV_HEADS * HEAD_DIM
CMP_BLK = 32
CMP_STRIDE = 16
SEL_BLK = 64
N_SEL = 16
WINDOW = 512
CHUNK = 128
MLP_GROUPS = 8
LRU_HEADS = 8
CONV_W = 4
LRU_C = 8.0
N_BRANCH = 3
EPS = 1e-6
BIG = 1e9
NEG = -1e30
SCALE = HEAD_DIM ** -0.5

LANES = 128
VMEM_LIMIT = 56 * 1024 * 1024

BF16 = jnp.bfloat16
F32 = jnp.float32


def _cparams(sem):
    return pltpu.CompilerParams(dimension_semantics=sem, vmem_limit_bytes=VMEM_LIMIT)


def _gelu(x):
    return 0.5 * x * (1.0 + jnp.tanh(0.7978845608028654 * (x + 0.044715 * (x * x * x))))


def _sigmoid(x):
    return 1.0 / (1.0 + jnp.exp(-x))


def _silu(x):
    return x * _sigmoid(x)


def _dot(a, b):
    return jnp.dot(a, b, preferred_element_type=F32)


def _dot_nt(a, b):
    return lax.dot_general(a, b, (((1,), (1,)), ((), ())), preferred_element_type=F32)


def _rms_kernel(x_ref, g_ref, o_ref):
    x = x_ref[...]
    ms = jnp.mean(x * x, axis=-1, keepdims=True)
    o_ref[...] = ((x * lax.rsqrt(ms + EPS)) * g_ref[...]).astype(o_ref.dtype)


def rms_rows(x, g, tm):
    m, d = x.shape
    return pl.pallas_call(
        _rms_kernel,
        grid=(m // tm,),
        in_specs=[pl.BlockSpec((tm, d), lambda i: (i, 0)),
                  pl.BlockSpec((1, d), lambda i: (0, 0))],
        out_specs=pl.BlockSpec((tm, d), lambda i: (i, 0)),
        out_shape=jax.ShapeDtypeStruct((m, d), BF16),
        compiler_params=_cparams(("parallel",)),
        name="rms_rows",
    )(x, g.reshape(1, d))


def _proj_act(acc, act, gain_ref, flag_ref):
    if act == "silu":
        return _silu(acc)
    if act == "gelu":
        return _gelu(acc)
    if act == "sigmoid":
        return _sigmoid(acc)
    if act == "norm":
        outs = []
        for c in range(acc.shape[1] // LANES):
            sl = slice(c * LANES, (c + 1) * LANES)
            z = acc[:, sl]
            ms = jnp.mean(z * z, axis=-1, keepdims=True)
            zn = (z * lax.rsqrt(ms + EPS)) * gain_ref[:, sl]
            outs.append(jnp.where(flag_ref[:, sl] > 0.5, zn, z))
        return jnp.concatenate(outs, axis=1)
    return acc


def _proj_kernel(a_ref, a2_ref, w_ref, *rest, act, outs):
    if act == "norm":
        gain_ref, flag_ref = rest[:2]
        rest = rest[2:]
    else:
        gain_ref = flag_ref = None
    w = w_ref[...]
    z = _proj_act(_dot_nt(a_ref[...], w), act, gain_ref, flag_ref)
    for kind, o_ref in zip(outs, rest):
        if kind == "heads":
            nh = z.shape[1] // HEAD_DIM
            for g in range(nh):
                o_ref[pl.ds(g, z.shape[0], stride=nh), :] = z[:, g * HEAD_DIM:(g + 1) * HEAD_DIM]
        else:
            o_ref[...] = z.astype(o_ref.dtype)
    o2_ref = rest[-1]

    @pl.when(pl.program_id(0) == 0)
    def _():
        o2_ref[...] = _proj_act(_dot_nt(a2_ref[...], w), act, gain_ref, flag_ref).astype(o2_ref.dtype)


def proj(a, a2, wt, row0, nrows, act, outs, tm, tn, gain=None, flag=None):
    m, k = a.shape
    m2 = a2.shape[0]
    assert row0 % tn == 0 and nrows % tn == 0 and m % tm == 0
    jb = row0 // tn
    nj = nrows // tn
    in_specs = [pl.BlockSpec((tm, k), lambda i, j: (i, 0)),
                pl.BlockSpec((m2, k), lambda i, j: (0, 0)),
                pl.BlockSpec((tn, k), lambda i, j: (jb + j, 0))]
    args = [a, a2, wt]
    if act == "norm":
        in_specs += [pl.BlockSpec((1, tn), lambda i, j: (0, j))] * 2
        args += [gain, flag]
    nh = tn // HEAD_DIM
    out_specs, out_shape = [], []
    for kind in outs:
        if kind == "heads":
            out_specs.append(pl.BlockSpec((None, tm * nh, HEAD_DIM), lambda i, j: (j, i, 0)))
            out_shape.append(jax.ShapeDtypeStruct((nj, m * nh, HEAD_DIM), F32))
        else:
            out_specs.append(pl.BlockSpec((tm, tn), lambda i, j: (i, j)))
            out_shape.append(jax.ShapeDtypeStruct((m, nrows), kind))
    out_specs.append(pl.BlockSpec((m2, tn), lambda i, j: (0, jnp.where(i == 0, j, nj - 1))))
    out_shape.append(jax.ShapeDtypeStruct((m2, nrows), F32 if act == "norm" else outs[0]))
    return pl.pallas_call(
        functools.partial(_proj_kernel, act=act, outs=tuple(outs)),
        grid=(m // tm, nj),
        in_specs=in_specs,
        out_specs=out_specs,
        out_shape=out_shape,
        compiler_params=_cparams(("arbitrary", "arbitrary")),
        name="proj_" + act,
    )(*args)


def _merge_kernel(a1, a2, a3, w1, w2, w3, g1, g2, g3, o_ref):
    m = g1[...].astype(F32) * _dot(a1[...], w1[...])
    m = m + g2[...].astype(F32) * _dot(a2[...], w2[...])
    m = m + g3[...].astype(F32) * _dot(a3[...], w3[...])
    o_ref[...] = m.astype(o_ref.dtype)


def merge(a_list, w_list, l, gm, tm, tn):
    m = a_list[0].shape[0]
    n = w_list[0].shape[2]
    nb = n // tn
    in_specs = [pl.BlockSpec((tm, a.shape[1]), lambda i, j: (i, 0)) for a in a_list]
    in_specs += [pl.BlockSpec((None, w.shape[1], tn), lambda i, j: (l, 0, j)) for w in w_list]
    in_specs += [pl.BlockSpec((tm, tn), functools.partial(lambda i, j, b: (i, b * nb + j), b=b))
                 for b in range(N_BRANCH)]
    return pl.pallas_call(
        _merge_kernel,
        grid=(m // tm, nb),
        in_specs=in_specs,
        out_specs=pl.BlockSpec((tm, tn), lambda i, j: (i, j)),
        out_shape=jax.ShapeDtypeStruct((m, n), BF16),
        compiler_params=_cparams(("parallel", "arbitrary")),
        name="merge",
    )(*a_list, *w_list, gm, gm, gm)


def _resid_kernel(a_ref, w_ref, x_ref, o_ref):
    o_ref[...] = x_ref[...] + _dot(a_ref[...], w_ref[...])


def out_proj(a, w, l, x, tm, tn):
    m, k = a.shape
    n = w.shape[2]
    return pl.pallas_call(
        _resid_kernel,
        grid=(m // tm, n // tn),
        in_specs=[pl.BlockSpec((tm, k), lambda i, j: (i, 0)),
                  pl.BlockSpec((None, k, tn), lambda i, j: (l, 0, j)),
                  pl.BlockSpec((tm, tn), lambda i, j: (i, j))],
        out_specs=pl.BlockSpec((tm, tn), lambda i, j: (i, j)),
        out_shape=jax.ShapeDtypeStruct((m, n), F32),
        compiler_params=_cparams(("parallel", "arbitrary")),
        name="out_proj",
    )(a, w, x)


def _softmax_rows(s, mask):
    s = jnp.where(mask, s, NEG)
    m = jnp.max(s, axis=-1, keepdims=True)
    e = jnp.where(mask, jnp.exp(s - m), 0.0)
    d = jnp.sum(e, axis=-1, keepdims=True)
    return e / jnp.where(d > 0.0, d, 1.0)


def _compress_combine(acc, const, nvalid):
    nsub = acc.shape[0]
    top = acc[:, :LANES]
    bot = pltpu.roll(acc[:, LANES:], nsub - 1, 0)
    row = lax.broadcasted_iota(jnp.int32, (nsub, LANES), 0)
    return jnp.where(row < nvalid, top + bot + const, 0.0)


def _pe_const(pe_ref, w_ref):
    return jnp.dot(pe_ref[...], w_ref[...], precision=lax.Precision.HIGHEST,
                   preferred_element_type=F32)[0:1, :]


def _cmp_prompt_kernel(k_ref, v_ref, wk_ref, wv_ref, pek_ref, pev_ref, wkf_ref, wvf_ref, g_ref,
                       ko_ref, vo_ref):
    nsub = ko_ref.shape[0]
    acck = jnp.zeros((nsub, 2 * HEAD_DIM), F32)
    accv = jnp.zeros((nsub, 2 * HEAD_DIM), F32)
    for l in range(CMP_STRIDE):
        xk = k_ref[pl.ds(l, nsub, stride=CMP_STRIDE), :].astype(BF16)
        xv = v_ref[pl.ds(l, nsub, stride=CMP_STRIDE), :].astype(BF16)
        acck = acck + _dot(xk, wk_ref[l])
        accv = accv + _dot(xv, wv_ref[l])
    kc = _compress_combine(acck, _pe_const(pek_ref, wkf_ref), nsub - 1)
    ms = jnp.mean(kc * kc, axis=-1, keepdims=True)
    ko_ref[...] = (kc * lax.rsqrt(ms + EPS)) * g_ref[...]
    vo_ref[...] = _compress_combine(accv, _pe_const(pev_ref, wvf_ref), nsub - 1)


def _cmp_weights(w_cmp, pe):
    wcat = jnp.concatenate([w_cmp[:CMP_STRIDE], w_cmp[CMP_STRIDE:]], axis=2).astype(BF16)
    pe8 = jnp.broadcast_to(pe.reshape(1, CMP_BLK * HEAD_DIM), (8, CMP_BLK * HEAD_DIM))
    return wcat, pe8, w_cmp.reshape(CMP_BLK * HEAD_DIM, HEAD_DIM)


def compress_prompt(zf, b, t, wk, wv, gk):
    nsub = t // CMP_STRIDE
    wkc, pek, wkf = wk
    wvc, pev, wvf = wv
    full = lambda a: pl.BlockSpec(a.shape, lambda i, g: (0,) * a.ndim)
    head = lambda off: pl.BlockSpec((t, HEAD_DIM), lambda i, g: (i, off // HEAD_DIM + g))
    out = jax.ShapeDtypeStruct((b * N_KV_HEADS, nsub, HEAD_DIM), F32)
    return pl.pallas_call(
        _cmp_prompt_kernel,
        grid=(b, N_KV_HEADS),
        in_specs=[head(0), head(KV_W),
                  full(wkc), full(wvc), full(pek), full(pev), full(wkf), full(wvf), full(gk)],
        out_specs=[pl.BlockSpec((None, nsub, HEAD_DIM), lambda i, g: (i * N_KV_HEADS + g, 0, 0))] * 2,
        out_shape=[out, out],
        compiler_params=_cparams(("parallel", "parallel")),
        name="cmp_prompt",
    )(zf, zf, wkc, wvc, pek, pev, wkf, wvf, gk)


def _sel_matrix(nc_pad, nc, ns_pad, ns):
    c0 = np.arange(nc_pad) * CMP_STRIDE
    s0 = np.arange(ns_pad) * SEL_BLK
    m = (c0[:, None] < s0[None, :] + SEL_BLK) & (c0[:, None] + CMP_BLK > s0[None, :])
    m &= (np.arange(nc_pad)[:, None] < nc) & (np.arange(ns_pad)[None, :] < ns)
    return jnp.asarray(m.astype(np.float32))


EXP_C = SCALE * 1.4426950408889634


def _select_mask_t(imp_t, tpos_row, ns):
    blk = lax.broadcasted_iota(jnp.int32, imp_t.shape, 0)
    cur = tpos_row // SEL_BLK
    forced = (blk == 0) | (blk == cur) | (blk == cur - 1)
    impm = jnp.where(blk > cur, -BIG, jnp.where(forced, BIG, imp_t))
    rank = jnp.zeros(imp_t.shape, F32)
    for s2 in range(ns):
        row = impm[s2:s2 + 1, :]
        gt = jnp.where(row > impm, 1.0, 0.0)
        ge = jnp.where(row >= impm, 1.0, 0.0)
        rank = rank + jnp.where(blk > s2, ge, gt)
    return jnp.where(rank < float(min(N_SEL, ns)), 1.0, 0.0)


def _softmax_bias(s3, bias, valid):
    sm = s3 + bias[None]
    m = jnp.max(sm, axis=-1, keepdims=True)
    e = jnp.exp2((sm - m) * EXP_C)
    inv = 1.0 / jnp.sum(e, axis=-1, keepdims=True)
    if valid is not None:
        inv = jnp.where(valid[None], inv, 0.0)
    return e * inv


def _attn_prompt_kernel(q_ref, kc_ref, vc_ref, ks_ref, vs_ref, kw_ref, vw_ref, gn_ref, gate_ref,
                        msel_ref, e_ref, o_ref, *, tq, kvc, ns):
    g = pl.program_id(1)
    t0 = pl.program_id(2) * tq
    nh = Q_PER_KV
    rows = nh * tq
    q4 = jnp.concatenate([q_ref[:, h * HEAD_DIM:(h + 1) * HEAD_DIM] for h in range(nh)], axis=0)
    tpos = t0 + lax.broadcasted_iota(jnp.int32, (tq, 1), 0)
    tpos_row = t0 + lax.broadcasted_iota(jnp.int32, (1, tq), 1)

    ncp = kc_ref.shape[0]
    endpos = lax.broadcasted_iota(jnp.int32, (1, ncp), 1) * CMP_STRIDE + (CMP_BLK - 1)
    bias_c = jnp.where(endpos <= tpos, 0.0, NEG)
    p3 = _softmax_bias(_dot_nt(q4, kc_ref[...].astype(BF16)).reshape(nh, tq, ncp), bias_c,
                       tpos >= CMP_BLK - 1)
    o_cmp = _dot(p3.reshape(rows, ncp).astype(BF16), vc_ref[...].astype(BF16))
    imp = jnp.dot(jnp.sum(p3, axis=0), msel_ref[...], precision=lax.Precision.HIGHEST,
                  preferred_element_type=F32)
    ns8 = -(-ns // 8) * 8
    sel_t = _select_mask_t(imp.T[:ns8], tpos_row, ns)
    sel = jnp.concatenate([sel_t, jnp.zeros((imp.shape[1] - ns8, tq), F32)], axis=0).T.astype(BF16)

    def body(c, carry):
        m, l, acc = carry
        k0 = pl.multiple_of(c * kvc, kvc)
        kpos = k0 + lax.broadcasted_iota(jnp.int32, (1, kvc), 1)
        bias = jnp.where(kpos <= tpos, (_dot(sel, e_ref[c]) - 1.0) * -NEG, NEG)
        sm = _dot_nt(q4, ks_ref[pl.ds(k0, kvc), :]).reshape(nh, tq, kvc) + bias[None]
        m_new = jnp.maximum(m, jnp.max(sm, axis=-1, keepdims=True))
        alpha = jnp.exp2((m - m_new) * EXP_C)
        e = jnp.exp2((sm - m_new) * EXP_C)
        l = alpha * l + jnp.sum(e, axis=-1, keepdims=True)
        pv = _dot(e.reshape(rows, kvc).astype(BF16), vs_ref[pl.ds(k0, kvc), :])
        return m_new, l, alpha * acc + pv.reshape(nh, tq, HEAD_DIM)

    nch = (t0 + tq + kvc - 1) // kvc
    m, l, acc = lax.fori_loop(0, nch, body, (jnp.full((nh, tq, 1), NEG, F32), jnp.zeros((nh, tq, 1), F32),
                                             jnp.zeros((nh, tq, HEAD_DIM), F32)))
    o_sel = (acc * jnp.where(l > 0.0, 1.0 / l, 0.0)).reshape(rows, HEAD_DIM)

    nwin = WINDOW + tq
    w0 = pl.multiple_of(jnp.maximum(t0 - WINDOW, 0), tq)
    diff = tpos - (w0 + lax.broadcasted_iota(jnp.int32, (1, nwin), 1))
    bias_w = jnp.where((diff >= 0) & (diff <= WINDOW), 0.0, NEG)
    pw = _softmax_bias(_dot_nt(q4, kw_ref[pl.ds(w0, nwin), :]).reshape(nh, tq, nwin), bias_w, None)
    o_win = _dot(pw.reshape(rows, nwin).astype(BF16), vw_ref[pl.ds(w0, nwin), :])

    gs = _sigmoid(gn_ref[...])
    lane = lax.broadcasted_iota(jnp.int32, gs.shape, 1)
    for h in range(nh):
        r = slice(h * tq, (h + 1) * tq)
        o = None
        for br, ob in enumerate((o_cmp, o_sel, o_win)):
            cidx = br * N_HEADS + g * nh + h
            gcol = jnp.sum(jnp.where(lane == cidx, gs, 0.0), axis=-1, keepdims=True)
            o = gcol * ob[r] if o is None else o + gcol * ob[r]
        hs = slice(h * HEAD_DIM, (h + 1) * HEAD_DIM)
        o_ref[:, hs] = (o * gate_ref[:, hs].astype(F32)).astype(o_ref.dtype)


def attn_prompt(zq, zkv, kcmp, vcmp, gn, gate, b, t):
    tq, kvc = 128, 512
    assert t // CMP_STRIDE == LANES and t % kvc == 0 and t >= WINDOW + tq
    nq = t // tq
    ns = -(-t // SEL_BLK)
    nc = t // CMP_STRIDE - CMP_BLK // CMP_STRIDE + 1
    msel = _sel_matrix(LANES, nc, LANES, ns)
    kk = np.arange(t)
    e3 = (kk[None, :] // SEL_BLK == np.arange(LANES)[:, None]).astype(np.float32)
    e3 = jnp.asarray(e3.reshape(LANES, t // kvc, kvc).transpose(1, 0, 2), BF16)
    hb = lambda off: (lambda bi, g, i: (bi, off // HEAD_DIM + g))
    kv_spec = lambda off: pl.BlockSpec((t, HEAD_DIM), hb(off))
    cm_spec = pl.BlockSpec((None, t // CMP_STRIDE, HEAD_DIM), lambda bi, g, i: (bi * N_KV_HEADS + g, 0, 0))
    row4 = pl.BlockSpec((tq, Q_PER_KV * HEAD_DIM), lambda bi, g, i: (bi * nq + i, g))
    return pl.pallas_call(
        functools.partial(_attn_prompt_kernel, tq=tq, kvc=kvc, ns=ns),
        grid=(b, N_KV_HEADS, nq),
        in_specs=[row4, cm_spec, cm_spec,
                  kv_spec(2 * KV_W), kv_spec(3 * KV_W), kv_spec(4 * KV_W), kv_spec(5 * KV_W),
                  pl.BlockSpec((tq, LANES), lambda bi, g, i: (bi * nq + i, 0)),
                  row4,
                  pl.BlockSpec(msel.shape, lambda bi, g, i: (0, 0)),
                  pl.BlockSpec(e3.shape, lambda bi, g, i: (0, 0, 0))],
        out_specs=row4,
        out_shape=jax.ShapeDtypeStruct((b * t, NSA_W), BF16),
        compiler_params=_cparams(("parallel", "parallel", "arbitrary")),
        name="attn_prompt",
    )(zq, kcmp, vcmp, zkv, zkv, zkv, zkv, gn, gate, msel, e3)


def _layer_norm(v, g, b):
    vc = v - jnp.mean(v, axis=-1, keepdims=True)
    var = jnp.mean(vc * vc, axis=-1, keepdims=True)
    return vc * lax.rsqrt(var + EPS) * g + b


def _mlp_prompt_kernel(u_ref, v_ref, gate_ref, lg_ref, lb_ref, ws_ref, bst_ref, o_ref):
    tm = u_ref.shape[0]
    gw = ws_ref.shape[1]
    vb = _layer_norm(v_ref[...], lg_ref[...], lb_ref[...]).astype(BF16)
    row = lax.broadcasted_iota(jnp.int32, (CHUNK, CHUNK), 0)
    col = lax.broadcasted_iota(jnp.int32, (CHUNK, CHUNK), 1)
    for gi in range(MLP_GROUPS):
        cs = slice(gi * gw, (gi + 1) * gw)
        wsg = jnp.where(row >= col, ws_ref[gi], 0.0).astype(BF16)
        bias = bst_ref[:, gi:gi + 1]
        for ch in range(tm // CHUNK):
            rs = slice(ch * CHUNK, (ch + 1) * CHUNK)
            mixed = _dot(wsg, vb[rs, cs]) + bias
            o_ref[rs, cs] = (u_ref[rs, cs] * mixed * gate_ref[rs, cs].astype(F32)).astype(o_ref.dtype)


def mlp_prompt(uv, gate, ln_g, ln_b, w_s, b_s, tm):
    m, w2 = uv.shape
    w = w2 // 2
    assert w // MLP_GROUPS == CHUNK == w_s.shape[1]
    full = lambda a: pl.BlockSpec(a.shape, lambda i: (0,) * a.ndim)
    bst = b_s.T
    return pl.pallas_call(
        _mlp_prompt_kernel,
        grid=(m // tm,),
        in_specs=[pl.BlockSpec((tm, w), lambda i: (i, 0)), pl.BlockSpec((tm, w), lambda i: (i, 1)),
                  pl.BlockSpec((tm, w), lambda i: (i, 0)),
                  pl.BlockSpec((1, w), lambda i: (0, 0)), pl.BlockSpec((1, w), lambda i: (0, 0)),
                  full(w_s), full(bst)],
        out_specs=pl.BlockSpec((tm, w), lambda i: (i, 0)),
        out_shape=jax.ShapeDtypeStruct((m, w), BF16),
        compiler_params=_cparams(("parallel",)),
        name="mlp_prompt",
    )(uv, uv, gate, ln_g.reshape(1, w), ln_b.reshape(1, w), w_s, bst)


def _softplus(x):
    return jnp.maximum(x, 0.0) + jnp.log1p(jnp.exp(-jnp.abs(x)))


def _lru_gates(xc, wa_ref, wx_ref, ba, bx, lam):
    nh = xc.shape[1] // HEAD_DIM
    rs, is_ = [], []
    for hh in range(nh):
        xh = xc[:, hh * HEAD_DIM:(hh + 1) * HEAD_DIM].astype(BF16)
        rs.append(_dot(xh, wa_ref[hh]))
        is_.append(_dot(xh, wx_ref[hh]))
    r = _sigmoid(jnp.concatenate(rs, axis=1) + ba)
    i = _sigmoid(jnp.concatenate(is_, axis=1) + bx)
    log_a = -LRU_C * r * _softplus(-lam)
    th = jnp.tanh(log_a)
    return jnp.exp(log_a), jnp.sqrt(-2.0 * th / (1.0 - th)) * (i * xc)


def _lru_prompt_kernel(x_ref, gate_ref, cw_ref, cb_ref, wa_ref, wx_ref, ba_ref, bx_ref, lam_ref,
                       o_ref, h_ref, a_scr, u_scr):
    t, wb = x_ref.shape
    x = x_ref[...]
    row = lax.broadcasted_iota(jnp.int32, (t, wb), 0)
    xc = cb_ref[...] + x * cw_ref[CONV_W - 1:CONV_W, :]
    for d in range(1, CONV_W):
        xs = jnp.where(row >= d, pltpu.roll(x, d, 0), 0.0)
        xc = xc + xs * cw_ref[CONV_W - 1 - d:CONV_W - d, :]
    a, u = _lru_gates(xc, wa_ref, wx_ref, ba_ref[...], bx_ref[...], lam_ref[...])
    a_scr[...] = a
    u_scr[...] = u
    row8 = lax.broadcasted_iota(jnp.int32, (8, wb), 0)

    def body(bi, h):
        r0 = pl.multiple_of(bi * 8, 8)
        a8 = a_scr[pl.ds(r0, 8), :]
        u8 = u_scr[pl.ds(r0, 8), :]
        for d in (1, 2, 4):
            a_sh = jnp.where(row8 >= d, pltpu.roll(a8, d, 0), 1.0)
            u_sh = jnp.where(row8 >= d, pltpu.roll(u8, d, 0), 0.0)
            u8 = a8 * u_sh + u8
            a8 = a8 * a_sh
        h8 = a8 * h + u8
        u_scr[pl.ds(r0, 8), :] = h8
        return h8[7:8, :]

    h = lax.fori_loop(0, t // 8, body, jnp.zeros((1, wb), F32))
    h_ref[...] = h
    o_ref[...] = (u_scr[...] * gate_ref[...].astype(F32)).astype(o_ref.dtype)


def lru_prompt(xl, gate, conv_w, conv_b, wa, wx, ba, bx, lam, b, t):
    w = xl.shape[1]
    wb = 512
    nh = wb // HEAD_DIM
    assert w % wb == 0 and t % 8 == 0 and wa.shape[1] == HEAD_DIM
    vec = lambda a: a.reshape(1, w)
    vspec = pl.BlockSpec((1, wb), lambda i, j: (0, j))
    blk = pl.BlockSpec((t, wb), lambda i, j: (i, j))
    hspec = pl.BlockSpec((nh, HEAD_DIM, HEAD_DIM), lambda i, j: (j, 0, 0))
    return pl.pallas_call(
        _lru_prompt_kernel,
        grid=(b, w // wb),
        in_specs=[blk, blk, pl.BlockSpec((CONV_W, wb), lambda i, j: (0, j)), vspec,
                  hspec, hspec, vspec, vspec, vspec],
        out_specs=[blk, pl.BlockSpec((None, 1, wb), lambda i, j: (i, 0, j))],
        out_shape=[jax.ShapeDtypeStruct((b * t, w), BF16), jax.ShapeDtypeStruct((b, 1, w), F32)],
        scratch_shapes=[pltpu.VMEM((t, wb), F32), pltpu.VMEM((t, wb), F32)],
        compiler_params=_cparams(("parallel", "parallel")),
        name="lru_prompt",
    )(xl, gate, conv_w, vec(conv_b), wa.astype(BF16), wx.astype(BF16), vec(ba), vec(bx), vec(lam))


ATT_W = NSA_W + 6 * KV_W
SEG = 512


PACK_TAIL = 64


def _pack_kernel(a_ref, b_ref, o_ref, *, gn_tile, ngate):
    j = pl.program_id(0)

    @pl.when(j < gn_tile)
    def _():
        o_ref[...] = a_ref[...].astype(o_ref.dtype)

    @pl.when(j == gn_tile)
    def _():
        row = lax.broadcasted_iota(jnp.int32, a_ref.shape, 0)
        o_ref[...] = jnp.where(row < ngate, a_ref[...], 0.0).astype(o_ref.dtype)

    @pl.when(j > gn_tile)
    def _():
        o_ref[...] = jnp.concatenate([a_ref[ngate:, :], b_ref[:ngate, :]], axis=0).astype(o_ref.dtype)


def _pack_w_in(w_in, l, d_model, mlp_w, lru_w):
    ngate = N_BRANCH * N_HEADS
    sizes = (("att", ATT_W), ("gn", SEG), ("gate_nsa", NSA_W), ("uv", 2 * mlp_w), ("gate_mlp", mlp_w),
             ("xl", lru_w), ("gate_lru", lru_w), ("gm", N_BRANCH * d_model))
    offs, acc = {}, 0
    for name, size in sizes:
        offs[name] = (acc, size)
        acc += size
    wt = jnp.swapaxes(w_in, 1, 2)
    n_in, k = wt.shape[1:]
    assert acc == n_in - ngate + SEG and ATT_W % SEG == 0 and ngate <= PACK_TAIL and ngate % 16 == 0
    gn_tile = ATT_W // SEG
    w = pl.pallas_call(
        functools.partial(_pack_kernel, gn_tile=gn_tile, ngate=ngate),
        grid=(acc // SEG,),
        in_specs=[pl.BlockSpec((None, SEG, k), lambda j: (l, jnp.where(j > gn_tile, j - 1, j), 0)),
                  pl.BlockSpec((None, PACK_TAIL, k), lambda j: (l, (SEG // PACK_TAIL) * j, 0))],
        out_specs=pl.BlockSpec((SEG, k), lambda j: (j, 0)),
        out_shape=jax.ShapeDtypeStruct((acc, k), BF16),
        compiler_params=_cparams(("parallel",)),
        name="pack_w",
    )(wt, wt)
    return w, offs


def _norm_vectors(q_g, k_g):
    one = jnp.ones((KV_W,), F32)
    zero = jnp.zeros((KV_W,), F32)
    gain = jnp.concatenate([jnp.tile(q_g, N_HEADS), one, one, jnp.tile(k_g[1], N_KV_HEADS), one,
                            jnp.tile(k_g[2], N_KV_HEADS), one])
    flag = jnp.concatenate([jnp.ones((NSA_W,), F32), zero, zero, one, zero, one, zero])
    return gain.reshape(1, ATT_W), flag.reshape(1, ATT_W)


def _in_proj(x2d, xs2d, lw, tm):
    h = rms_rows(x2d, lw["norm_g"], min(tm, 512))
    hs = rms_rows(xs2d, lw["norm_g"], xs2d.shape[0])
    w, offs = lw["w_pack"]
    p = lambda name, act, dt, tn=SEG, n=None, off=0, **kw: proj(
        h, hs, w, offs[name][0] + off, n or offs[name][1], act, dt if isinstance(dt, tuple) else (dt,),
        tm, tn, **kw)
    gain, flag = lw["gain"], lw["flag"]
    zq, zqs = p("att", "norm", BF16, n=NSA_W, gain=gain[:, :NSA_W], flag=flag[:, :NSA_W])
    kvf, kvb, kvh, kvs = p("att", "norm", (F32, BF16, "heads"), n=ATT_W - NSA_W, off=NSA_W,
                           gain=gain[:, NSA_W:], flag=flag[:, NSA_W:])
    z, zs = dict(q=zq, kvf=kvf, kvb=kvb, kvh=kvh), dict(q=zqs, kv=kvs)
    for name, act, dt, kw in (("gn", "none", F32, dict(tn=LANES, n=LANES)), ("gate_nsa", "silu", BF16, {}),
                              ("uv", "gelu", F32, {}), ("gate_mlp", "silu", BF16, {}), ("xl", "none", F32, {}),
                              ("gate_lru", "silu", BF16, {}), ("gm", "sigmoid", BF16, {})):
        z[name], zs[name] = p(name, act, dt, **kw)
    return z, zs


def prompt_layer(x2d, z, lw, l, b, t, w_buf):
    tm = 1024
    kcmp, vcmp = compress_prompt(z["kvf"], b, t, lw["cmp_k"], lw["cmp_v"], lw["k_norm_g"][0:1])
    o_nsa = attn_prompt(z["q"], z["kvb"], kcmp, vcmp, z["gn"], z["gate_nsa"], b, t)
    o_mlp = mlp_prompt(z["uv"], z["gate_mlp"], lw["mlp_ln_g"], lw["mlp_ln_b"], lw["w_s"], lw["b_s"], 512)
    o_lru, h_last = lru_prompt(z["xl"], z["gate_lru"], lw["conv_w"], lw["conv_b"], lw["lru_wa"], lw["lru_wx"],
                               lw["lru_ba"], lw["lru_bx"], lw["lru_lambda"], b, t)
    m = merge([o_nsa, o_mlp, o_lru], lw["w_br"], l, z["gm"], tm, SEG)
    y = out_proj(m, lw["w_out"], l, x2d, tm, SEG)
    kv = lambda i: z["kvh"][i].reshape(b, t, N_KV_HEADS, HEAD_DIM)
    xl = z["xl"].reshape(b, t, -1)
    assert t >= w_buf and t >= CONV_W - 1
    state = dict(cmp_k=kv(0), cmp_v=kv(1), sel_k=kv(2), sel_v=kv(3),
                 win_k=kv(4)[:, t - w_buf:], win_v=kv(5)[:, t - w_buf:],
                 lru_h=h_last.reshape(b, -1), lru_conv=xl[:, t - (CONV_W - 1):])
    return y, state


PAGES_PER_STEP = 32


def _cmp_paged_kernel(pt_ref, *refs, pg):
    k_refs, v_refs = refs[:pg], refs[pg:2 * pg]
    wk_ref, wv_ref, ko_ref, vo_ref, r_scr = refs[2 * pg:]
    rows = k_refs[0].shape[0]
    sub = N_KV_HEADS * CMP_STRIDE
    nsp = rows // sub
    for page_refs, w_ref, o_ref in ((k_refs, wk_ref, ko_ref), (v_refs, wv_ref, vo_ref)):
        blocks = [jnp.concatenate([r[pl.ds(sub * i + 8 * m, 8), :] for m in range(sub // 8)], axis=1)
                  for r in page_refs for i in range(nsp)]
        res = _dot(jnp.concatenate(blocks, axis=0).astype(BF16), w_ref[...])
        n = res.shape[0]
        both = res[:, :2 * HEAD_DIM] + pltpu.roll(res[:, 2 * HEAD_DIM:], n - N_KV_HEADS, 0)
        r_scr[0] = both[:, :HEAD_DIM]
        r_scr[1] = both[:, HEAD_DIM:]
        for g in range(N_KV_HEADS):
            o_ref[g] = jnp.concatenate([r_scr[0, pl.ds(g, n // 8, stride=8), :],
                                        r_scr[1, pl.ds(g, n // 8, stride=8), :]], axis=1)


def _cmp_paged_weights(wcat):
    return wcat.reshape(CMP_STRIDE // 2, 2, HEAD_DIM, 2 * HEAD_DIM).transpose(0, 2, 1, 3).reshape(
        CMP_STRIDE // 2 * HEAD_DIM, 4 * HEAD_DIM)


def compress_paged(pool_k, pool_v, page_table, wkc, wvc, page0):
    b, n_pages = page_table.shape
    rows = pool_k.shape[1]
    pg = PAGES_PER_STEP
    assert n_pages % pg == 0 and N_KV_HEADS * 2 == 8
    nsp = rows // N_KV_HEADS // CMP_STRIDE
    wkc, wvc = _cmp_paged_weights(wkc), _cmp_paged_weights(wvc)
    page_spec = lambda p: pl.BlockSpec((None, rows, HEAD_DIM), lambda i, c, pt: (page0 + pt[i, c * pg + p], 0, 0))
    full = lambda a: pl.BlockSpec(a.shape, lambda i, c, pt: (0,) * a.ndim)
    out = jax.ShapeDtypeStruct((b, N_KV_HEADS, n_pages * nsp, 2 * HEAD_DIM), F32)
    ospec = pl.BlockSpec((None, N_KV_HEADS, pg * nsp, 2 * HEAD_DIM), lambda i, c, pt: (i, 0, c, 0))
    return pl.pallas_call(
        functools.partial(_cmp_paged_kernel, pg=pg),
        grid_spec=pltpu.PrefetchScalarGridSpec(
            num_scalar_prefetch=1,
            grid=(b, n_pages // pg),
            in_specs=[page_spec(p) for p in range(pg)] * 2 + [full(wkc), full(wvc)],
            out_specs=[ospec, ospec],
            scratch_shapes=[pltpu.VMEM((2, pg * nsp * 8, HEAD_DIM), F32)]),
        out_shape=[out, out],
        compiler_params=_cparams(("parallel", "arbitrary")),
        name="cmp_paged",
    )(page_table, *([pool_k] * pg), *([pool_v] * pg), wkc, wvc)


def _cmp_sample_kernel(tbk_ref, tbv_ref, q_ref, pek_ref, pev_ref, wkf_ref, wvf_ref, gk_ref, msel_ref,
                       o_ref, idx_ref, *, tpos, nc, ns):
    ck = _pe_const(pek_ref, wkf_ref)
    cv = _pe_const(pev_ref, wvf_ref)
    q = q_ref[...].astype(BF16)
    nh = q.shape[0]
    nsub = tbk_ref.shape[1]
    hrow = lax.broadcasted_iota(jnp.int32, (nh, 1), 0)
    blk = lax.broadcasted_iota(jnp.int32, (1, nsub), 1)
    cmask = (blk * CMP_STRIDE + (CMP_BLK - 1) <= tpos) & (blk < nc)
    row8 = lax.broadcasted_iota(jnp.int32, (8, nsub), 0)
    o = jnp.zeros((nh, HEAD_DIM), F32)
    psum = jnp.zeros((8, nsub), F32)
    for g in range(N_KV_HEADS):
        kc = _compress_combine(tbk_ref[g], ck, nc)
        ms = jnp.mean(kc * kc, axis=-1, keepdims=True)
        kc = (kc * lax.rsqrt(ms + EPS)) * gk_ref[...]
        vc = _compress_combine(tbv_ref[g], cv, nc)
        p = _softmax_rows(_dot_nt(q, kc.astype(BF16)) * SCALE, cmask & (hrow // Q_PER_KV == g))
        o = o + _dot(p.astype(BF16), vc.astype(BF16))
        psum = jnp.where(row8 == g, jnp.sum(p, axis=0, keepdims=True), psum)
    o_ref[...] = o
    imp = jnp.dot(psum, msel_ref[...], precision=lax.Precision.HIGHEST, preferred_element_type=F32)
    lane = lax.broadcasted_iota(jnp.int32, imp.shape, 1)
    cur = tpos // SEL_BLK
    forced = (lane == 0) | (lane == cur) | (lane == cur - 1)
    impm = jnp.where(lane > cur, -BIG, jnp.where(forced, BIG, imp))
    impm = jnp.where(lane < ns, impm, -jnp.inf)
    lane_f = lane.astype(F32)
    out_lane = lax.broadcasted_iota(jnp.int32, idx_ref.shape, 1)
    idxs = jnp.zeros(idx_ref.shape, F32)
    for j in range(min(N_SEL, ns)):
        mx = jnp.max(impm, axis=-1, keepdims=True)
        am = jnp.min(jnp.where(impm == mx, lane_f, 1e9), axis=-1, keepdims=True)
        idxs = jnp.where(out_lane == j, am, idxs)
        impm = jnp.where(lane_f == am, -jnp.inf, impm)
    idx_ref[...] = idxs.astype(jnp.int32)


def cmp_sample(tbk, tbv, q3, ck, cv, gk, tpos):
    b, nkv, nsub, _ = tbk.shape
    nc = (tpos + 1) // CMP_STRIDE - CMP_BLK // CMP_STRIDE + 1
    ns = -(-(tpos + 1) // SEL_BLK)
    ns_pad = -(-ns // LANES) * LANES
    assert nc <= nsub
    msel = _sel_matrix(nsub, nc, ns_pad, ns)
    _, pek, wkf = ck
    _, pev, wvf = cv
    full = lambda a: pl.BlockSpec(a.shape, lambda i: (0,) * a.ndim)
    tb_spec = pl.BlockSpec((None, nkv, nsub, 2 * HEAD_DIM), lambda i: (i, 0, 0, 0))
    return pl.pallas_call(
        functools.partial(_cmp_sample_kernel, tpos=tpos, nc=nc, ns=ns),
        grid=(b,),
        in_specs=[tb_spec, tb_spec, pl.BlockSpec((None, N_HEADS, HEAD_DIM), lambda i: (i, 0, 0)),
                  full(pek), full(pev), full(wkf), full(wvf), full(gk), full(msel)],
        out_specs=[pl.BlockSpec((None, N_HEADS, HEAD_DIM), lambda i: (i, 0, 0)),
                   pl.BlockSpec((None, 8, LANES), lambda i: (i, 0, 0))],
        out_shape=[jax.ShapeDtypeStruct((b, N_HEADS, HEAD_DIM), F32),
                   jax.ShapeDtypeStruct((b, 8, LANES), jnp.int32)],
        compiler_params=_cparams(("parallel",)),
        name="cmp_sample",
    )(tbk, tbv, q3, pek, pev, wkf, wvf, gk, msel)


def _attend_with_new(q, k_all, v_all, mask, k_new, v_new, new_ok):
    s = jnp.where(mask, _dot_nt(q.astype(BF16), k_all) * SCALE, NEG)
    s_new = jnp.where(new_ok, jnp.sum(q * k_new, axis=-1, keepdims=True) * SCALE, NEG)
    m = jnp.maximum(jnp.max(s, axis=-1, keepdims=True), s_new)
    e = jnp.where(mask, jnp.exp(s - m), 0.0)
    e_new = jnp.where(new_ok, jnp.exp(s_new - m), 0.0)
    d = jnp.sum(e, axis=-1, keepdims=True) + e_new
    o = _dot(e.astype(BF16), v_all) + e_new * v_new
    return o / jnp.where(d > 0.0, d, 1.0)


def _attn_sample_kernel(pt_ref, idx_ref, *refs, nblk, tpos, past_len, w_buf):
    k_refs, v_refs = refs[:nblk], refs[nblk:2 * nblk]
    (q_ref, ocmp_ref, kw_ref, vw_ref, ksn_ref, vsn_ref, kwn_ref, vwn_ref, gn_ref, gate_ref,
     o_ref) = refs[2 * nblk:]
    b = pl.program_id(0)
    g = pl.program_id(1)
    q = q_ref[...]
    rb = k_refs[0].shape[0]
    r = lax.broadcasted_iota(jnp.int32, (1, rb), 1)
    tok, hd = r // N_KV_HEADS, r % N_KV_HEADS
    masks = []
    new_sel = False
    for j in range(nblk):
        s = idx_ref[b, g * nblk + j]
        kpos = s * SEL_BLK + tok
        masks.append((hd == g) & (kpos <= tpos) & (kpos < past_len))
        new_sel = new_sel | (s == past_len // SEL_BLK)
    k_all = jnp.concatenate([kr[...].astype(BF16) for kr in k_refs], axis=0)
    v_all = jnp.concatenate([vr[...].astype(BF16) for vr in v_refs], axis=0)
    o_sel = _attend_with_new(q, k_all, v_all, jnp.concatenate(masks, axis=1),
                             ksn_ref[...], vsn_ref[...], new_sel & (past_len <= tpos))
    rw = lax.broadcasted_iota(jnp.int32, (1, kw_ref.shape[0]), 1)
    diff = tpos - (past_len - w_buf + rw // N_KV_HEADS)
    wmask = (rw % N_KV_HEADS == g) & (diff >= 0) & (diff <= WINDOW)
    o_win = _attend_with_new(q, kw_ref[...].astype(BF16), vw_ref[...].astype(BF16), wmask,
                             kwn_ref[...], vwn_ref[...], tpos - past_len <= WINDOW)
    gs = _sigmoid(gn_ref[...])
    o = gs[:, 0:1] * ocmp_ref[...] + gs[:, 1:2] * o_sel + gs[:, 2:3] * o_win
    o_ref[...] = o * gate_ref[...]


def attn_sample(pool_k, pool_v, page_table, idx, q4, ocmp4, win_k, win_v, ks_new, vs_new, kw_new, vw_new,
                gn4, gate4, past_len, w_buf, page_size, page0, seq0):
    b, n_pages = page_table.shape
    nblk = idx.shape[1] // N_KV_HEADS
    rb = pool_k.shape[1]
    halves = page_size // SEL_BLK
    tpos = past_len

    def blk_map(j):
        def f(i, g, pt, ix):
            s = ix[i, g * nblk + j]
            page = pt[i, jnp.minimum(s // halves, n_pages - 1)]
            return ((page0 + page) * halves + s % halves, 0, 0)
        return f

    blk_spec = lambda j: pl.BlockSpec((None, rb, HEAD_DIM), blk_map(j))
    per_bg = lambda a: pl.BlockSpec((None, None) + a.shape[2:], lambda i, g, pt, ix: (i, g, 0, 0))
    per_b = lambda a: pl.BlockSpec((None,) + a.shape[1:], lambda i, g, pt, ix: (seq0 + i, 0, 0))
    small = (q4, ocmp4)
    news = (ks_new, vs_new, kw_new, vw_new, gn4, gate4)
    return pl.pallas_call(
        functools.partial(_attn_sample_kernel, nblk=nblk, tpos=tpos, past_len=past_len, w_buf=w_buf),
        grid_spec=pltpu.PrefetchScalarGridSpec(
            num_scalar_prefetch=2,
            grid=(b, N_KV_HEADS),
            in_specs=[blk_spec(j) for j in range(nblk)] * 2 + [per_bg(a) for a in small]
            + [per_b(win_k), per_b(win_v)] + [per_bg(a) for a in news],
            out_specs=per_bg(q4)),
        out_shape=jax.ShapeDtypeStruct(q4.shape, F32),
        compiler_params=_cparams(("parallel", "arbitrary")),
        name="attn_sample",
    )(page_table, idx, *([pool_k] * nblk), *([pool_v] * nblk), q4, ocmp4, win_k, win_v, *news)


def _point_sample_kernel(u_ref, v_ref, gmlp_ref, lg_ref, lb_ref, ws0_ref, bs0_ref,
                         x_ref, glru_ref, buf_ref, h0_ref, cw_ref, cb_ref, wa_ref, wx_ref, ba_ref, bx_ref,
                         lam_ref, omlp_ref, vrow_ref, olru_ref, hnew_ref):
    vn = _layer_norm(v_ref[...], lg_ref[...], lb_ref[...])
    vrow_ref[...] = vn
    mixed = ws0_ref[...] * vn + bs0_ref[...]
    omlp_ref[...] = (u_ref[...] * mixed * gmlp_ref[...].astype(F32)).astype(omlp_ref.dtype)
    x = x_ref[...]
    xc = cb_ref[...] + x * cw_ref[CONV_W - 1:CONV_W, :]
    for j in range(CONV_W - 1):
        xc = xc + buf_ref[j] * cw_ref[j:j + 1, :]
    a, u = _lru_gates(xc, wa_ref, wx_ref, ba_ref[...], bx_ref[...], lam_ref[...])
    h = a * h0_ref[...] + u
    hnew_ref[...] = h
    olru_ref[...] = (h * glru_ref[...].astype(F32)).astype(olru_ref.dtype)


def point_sample(uv, gate_mlp, xl, gate_lru, buf_t, h0, lw):
    b, w2 = uv.shape
    w = w2 // 2
    gw = w // MLP_GROUPS
    vec = lambda a: a.reshape(1, -1)
    ws0 = vec(jnp.repeat(lw["w_s"][:, 0, 0], gw))
    bs0 = vec(jnp.repeat(lw["b_s"][:, 0], gw))
    args = (uv[:, :w], uv[:, w:], gate_mlp, vec(lw["mlp_ln_g"]), vec(lw["mlp_ln_b"]), ws0, bs0,
            xl, gate_lru, buf_t, h0, lw["conv_w"], vec(lw["conv_b"]), lw["lru_wa"].astype(BF16),
            lw["lru_wx"].astype(BF16), vec(lw["lru_ba"]), vec(lw["lru_bx"]), vec(lw["lru_lambda"]))
    lw_ = xl.shape[1]
    return pl.pallas_call(
        _point_sample_kernel,
        out_shape=[jax.ShapeDtypeStruct((b, w), BF16), jax.ShapeDtypeStruct((b, w), F32),
                   jax.ShapeDtypeStruct((b, lw_), BF16), jax.ShapeDtypeStruct((b, lw_), F32)],
        compiler_params=pltpu.CompilerParams(vmem_limit_bytes=VMEM_LIMIT),
        name="point_sample",
    )(*args)


def sample_layer(x2d, z, lw, l, caches, page_table, past_len, w_buf):
    b = x2d.shape[0]
    seg = lambda i: z["kv"][:, i * KV_W:(i + 1) * KV_W]
    kvh = lambda a: a.reshape(b, N_KV_HEADS, 1, HEAD_DIM)
    n_pool, page_size = caches["cmp_k"].shape[1:3]
    page0 = l * n_pool
    pool3 = lambda a: a.reshape(-1, page_size * N_KV_HEADS, HEAD_DIM)
    halves = lambda a: a.reshape(-1, SEL_BLK * N_KV_HEADS, HEAD_DIM)
    tbk, tbv = compress_paged(pool3(caches["cmp_k"]), pool3(caches["cmp_v"]), page_table,
                              lw["cmp_k"][0], lw["cmp_v"][0], page0)
    q3 = z["q"].reshape(b, N_HEADS, HEAD_DIM)
    o_cmp, idx = cmp_sample(tbk, tbv, q3, lw["cmp_k"], lw["cmp_v"], lw["k_norm_g"][0:1], past_len)
    n_sel = min(N_SEL, -(-(past_len + 1) // SEL_BLK))
    idx = idx[:, :N_KV_HEADS, :n_sel].reshape(b, N_KV_HEADS * n_sel)
    four = lambda a: a.reshape(b, N_KV_HEADS, Q_PER_KV, -1)
    gn4 = z["gn"][:, :N_BRANCH * N_HEADS].reshape(b, N_BRANCH, N_KV_HEADS, Q_PER_KV).transpose(0, 2, 3, 1)
    win3 = lambda a: a.reshape(-1, w_buf * N_KV_HEADS, HEAD_DIM)
    o_nsa = attn_sample(halves(caches["sel_k"]), halves(caches["sel_v"]), page_table, idx,
                        four(q3), four(o_cmp), win3(caches["win_k"]), win3(caches["win_v"]),
                        kvh(seg(2)), kvh(seg(3)), kvh(seg(4)), kvh(seg(5)),
                        gn4, four(z["gate_nsa"].astype(F32)), past_len, w_buf, page_size, page0, l * b)
    o_mlp, v_rows, o_lru, h_new = point_sample(
        z["uv"], z["gate_mlp"], z["xl"], z["gate_lru"], caches["lru_conv"].transpose(1, 0, 2),
        caches["lru_h"].astype(F32), lw)
    m = merge([o_nsa.reshape(b, NSA_W).astype(BF16), o_mlp, o_lru], lw["w_br"], l, z["gm"], b, SEG)
    y = out_proj(m, lw["w_out"], l, x2d, b, SEG)
    tok = lambda a: a.reshape(b, 1, N_KV_HEADS, HEAD_DIM)
    state = dict(cmp_k=tok(seg(0)), cmp_v=tok(seg(1)), sel_k=tok(seg(2)), sel_v=tok(seg(3)),
                 win_k=jnp.concatenate([caches["win_k"][l], tok(seg(4))], axis=1)[:, -w_buf:],
                 win_v=jnp.concatenate([caches["win_v"][l], tok(seg(5))], axis=1)[:, -w_buf:],
                 lru_h=h_new, lru_conv=jnp.concatenate([caches["lru_conv"], z["xl"][:, None]], axis=1)[:, 1:],
                 mlp_v=v_rows[:, None])
    return y, state


def _layer_weights(l, p):
    d_model = p["w_in"].shape[1]
    mlp_w = p["mlp_ln_g"].shape[1]
    lru_w = p["lru_lambda"].shape[1]
    names = ("norm_g", "q_norm_g", "k_norm_g", "mlp_ln_g", "mlp_ln_b", "w_s", "b_s", "conv_w", "conv_b",
             "lru_wa", "lru_ba", "lru_wx", "lru_bx", "lru_lambda")
    lw = {n: p[n][l] for n in names}
    lw["w_pack"] = _pack_w_in(p["w_in"], l, d_model, mlp_w, lru_w)
    lw["gain"], lw["flag"] = _norm_vectors(p["q_norm_g"][l], p["k_norm_g"][l])
    lw["cmp_k"] = _cmp_weights(p["w_cmp_k"][l], p["cmp_pe_k"][l])
    lw["cmp_v"] = _cmp_weights(p["w_cmp_v"][l], p["cmp_pe_v"][l])
    lw["w_br"] = [p[n] for n in ("w_br_nsa_bf", "w_br_mlp_bf", "w_br_lru_bf")]
    lw["w_out"] = p["w_out_bf"]
    return lw


def kernel(x_prompt, x_sample, cache_cmp_k, cache_cmp_v, cache_sel_k, cache_sel_v, state_win_k, state_win_v,
           state_lru_h, state_lru_conv, page_table, norm_g, w_in, q_norm_g, k_norm_g, cmp_pe_k, cmp_pe_v,
           w_cmp_k, w_cmp_v, mlp_ln_g, mlp_ln_b, w_s, b_s, conv_w, conv_b, lru_wa, lru_ba, lru_wx, lru_bx,
           lru_lambda, w_br_nsa, w_br_mlp, w_br_lru, w_out):
    params = dict(norm_g=norm_g, w_in=w_in, q_norm_g=q_norm_g, k_norm_g=k_norm_g, cmp_pe_k=cmp_pe_k,
                  cmp_pe_v=cmp_pe_v, w_cmp_k=w_cmp_k, w_cmp_v=w_cmp_v, mlp_ln_g=mlp_ln_g, mlp_ln_b=mlp_ln_b,
                  w_s=w_s, b_s=b_s, conv_w=conv_w, conv_b=conv_b, lru_wa=lru_wa, lru_ba=lru_ba, lru_wx=lru_wx,
                  lru_bx=lru_bx, lru_lambda=lru_lambda, w_br_nsa=w_br_nsa, w_br_mlp=w_br_mlp,
                  w_br_lru=w_br_lru, w_out=w_out)
    depth = w_in.shape[0]
    b, t, d = x_prompt.shape
    bs, ts, _ = x_sample.shape
    assert ts == 1
    w_buf = state_win_k.shape[2]
    past_len = page_table.shape[1] * cache_cmp_k.shape[2]
    yp = x_prompt.reshape(b * t, d)
    ys = x_sample.reshape(bs * ts, d)
    p_st, s_st = [], []
    for n in ("w_br_nsa", "w_br_mlp", "w_br_lru", "w_out"):
        params[n + "_bf"] = params[n].astype(BF16)
    for l in range(depth):
        lw = _layer_weights(l, params)
        z, zs = _in_proj(yp, ys, lw, 1024)
        yp, sp = prompt_layer(yp, z, lw, l, b, t, w_buf)
        caches = dict(cmp_k=cache_cmp_k, cmp_v=cache_cmp_v, sel_k=cache_sel_k, sel_v=cache_sel_v,
                      win_k=state_win_k, win_v=state_win_v, lru_h=state_lru_h[l],
                      lru_conv=state_lru_conv[l])
        ys, ss = sample_layer(ys, zs, lw, l, caches, page_table, past_len, w_buf)
        p_st.append(sp)
        s_st.append(ss)
    stk = lambda sts, name: jnp.stack([st[name] for st in sts])
    names = ("cmp_k", "cmp_v", "sel_k", "sel_v", "win_k", "win_v", "lru_h", "lru_conv")
    return ((yp.reshape(b, t, d), ys.reshape(bs, ts, d))
            + tuple(stk(p_st, n) for n in names)
            + tuple(stk(s_st, n) for n in names + ("mlp_v",)))
```

```python
import functools

import numpy as np
import jax
import jax.numpy as jnp
from jax import lax
from jax.experimental import pallas as pl
from jax.experimental.pallas import tpu as pltpu

N_HEADS = 16
HEAD_DIM = 128
N_KV_HEADS = 4
Q_PER_KV = N_HEADS // N_KV_HEADS
NSA_W = N_HEADS * HEAD_DIM
KV_W = N_KV_HEADS * HEAD_DIM
CMP_BLK = 32
CMP_STRIDE = 16
SEL_BLK = 64
N_SEL = 16
WINDOW = 512
CHUNK = 128
MLP_GROUPS = 8
LRU_HEADS = 8
CONV_W = 4
LRU_C = 8.0
N_BRANCH = 3
EPS = 1e-6
BIG = 1e9
NEG = -1e30
SCALE = HEAD_DIM ** -0.5

LANES = 128
VMEM_LIMIT = 56 * 1024 * 1024

BF16 = jnp.bfloat16
F32 = jnp.float32


def _cparams(sem):
    return pltpu.CompilerParams(dimension_semantics=sem, vmem_limit_bytes=VMEM_LIMIT)


def _gelu(x):
    return 0.5 * x * (1.0 + jnp.tanh(0.7978845608028654 * (x + 0.044715 * (x * x * x))))


def _sigmoid(x):
    return 1.0 / (1.0 + jnp.exp(-x))


def _silu(x):
    return x * _sigmoid(x)


def _dot(a, b):
    return jnp.dot(a, b, preferred_element_type=F32)


def _dot_nt(a, b):
    return lax.dot_general(a, b, (((1,), (1,)), ((), ())), preferred_element_type=F32)


def _rms_kernel(x_ref, g_ref, o_ref):
    x = x_ref[...]
    ms = jnp.mean(x * x, axis=-1, keepdims=True)
    o_ref[...] = ((x * lax.rsqrt(ms + EPS)) * g_ref[...]).astype(o_ref.dtype)


def rms_rows(x, g, tm):
    m, d = x.shape
    return pl.pallas_call(
        _rms_kernel,
        grid=(m // tm,),
        in_specs=[pl.BlockSpec((tm, d), lambda i: (i, 0)),
                  pl.BlockSpec((1, d), lambda i: (0, 0))],
        out_specs=pl.BlockSpec((tm, d), lambda i: (i, 0)),
        out_shape=jax.ShapeDtypeStruct((m, d), BF16),
        compiler_params=_cparams(("parallel",)),
        name="rms_rows",
    )(x, g.reshape(1, d))


def _row_tile(shape):
    return pl.BlockSpec(shape, lambda i, j: (i, 0), pipeline_mode=pl.Buffered(1))


def _proj_act(acc, act, gain_ref, flag_ref):
    if act == "silu":
        return _silu(acc)
    if act == "gelu":
        return _gelu(acc)
    if act == "sigmoid":
        return _sigmoid(acc)
    if act == "norm":
        outs = []
        for c in range(acc.shape[1] // LANES):
            sl = slice(c * LANES, (c + 1) * LANES)
            z = acc[:, sl]
            ms = jnp.mean(z * z, axis=-1, keepdims=True)
            zn = (z * lax.rsqrt(ms + EPS)) * gain_ref[:, sl]
            outs.append(jnp.where(flag_ref[:, sl] > 0.5, zn, z))
        return jnp.concatenate(outs, axis=1)
    return acc


def _proj_kernel(a_ref, a2_ref, w_ref, *rest, act, outs):
    if act == "norm":
        gain_ref, flag_ref = rest[:2]
        rest = rest[2:]
    else:
        gain_ref = flag_ref = None
    w = w_ref[...]
    z = _proj_act(_dot(a_ref[...], w), act, gain_ref, flag_ref)
    for kind, o_ref in zip(outs, rest):
        if kind == "heads":
            nh = z.shape[1] // HEAD_DIM
            for g in range(nh):
                o_ref[pl.ds(g, z.shape[0], stride=nh), :] = z[:, g * HEAD_DIM:(g + 1) * HEAD_DIM]
        else:
            o_ref[...] = z.astype(o_ref.dtype)
    o2_ref = rest[-1]

    @pl.when(pl.program_id(0) == 0)
    def _():
        o2_ref[...] = _proj_act(_dot(a2_ref[...], w), act, gain_ref, flag_ref).astype(o2_ref.dtype)


def proj(a, a2, w, row0, nrows, act, outs, tm, tn, gain=None, flag=None):
    m, k = a.shape
    m2 = a2.shape[0]
    assert row0 % tn == 0 and nrows % tn == 0 and m % tm == 0
    jb = row0 // tn
    nj = nrows // tn
    in_specs = [_row_tile((tm, k)),
                pl.BlockSpec((m2, k), lambda i, j: (0, 0)),
                pl.BlockSpec((k, tn), lambda i, j: (0, jb + j))]
    args = [a, a2, w]
    if act == "norm":
        in_specs += [pl.BlockSpec((1, tn), lambda i, j: (0, j))] * 2
        args += [gain, flag]
    nh = tn // HEAD_DIM
    out_specs, out_shape = [], []
    for kind in outs:
        if kind == "heads":
            out_specs.append(pl.BlockSpec((None, tm * nh, HEAD_DIM), lambda i, j: (j, i, 0)))
            out_shape.append(jax.ShapeDtypeStruct((nj, m * nh, HEAD_DIM), F32))
        else:
            out_specs.append(pl.BlockSpec((tm, tn), lambda i, j: (i, j)))
            out_shape.append(jax.ShapeDtypeStruct((m, nrows), kind))
    out_specs.append(pl.BlockSpec((m2, tn), lambda i, j: (0, jnp.where(i == 0, j, nj - 1))))
    out_shape.append(jax.ShapeDtypeStruct((m2, nrows), F32 if act == "norm" else outs[0]))
    return pl.pallas_call(
        functools.partial(_proj_kernel, act=act, outs=tuple(outs)),
        grid=(m // tm, nj),
        in_specs=in_specs,
        out_specs=out_specs,
        out_shape=out_shape,
        compiler_params=_cparams(("arbitrary", "arbitrary")),
        name="proj_" + act,
    )(*args)


def _merge_kernel(a1, a2, a3, w1, w2, w3, g1, g2, g3, o_ref):
    m = g1[...].astype(F32) * _dot(a1[...], w1[...])
    m = m + g2[...].astype(F32) * _dot(a2[...], w2[...])
    m = m + g3[...].astype(F32) * _dot(a3[...], w3[...])
    o_ref[...] = m.astype(o_ref.dtype)


def merge(a_list, w_list, l, gm, tm, tn):
    m = a_list[0].shape[0]
    n = w_list[0].shape[2]
    nb = n // tn
    in_specs = [_row_tile((tm, a.shape[1])) for a in a_list]
    in_specs += [pl.BlockSpec((None, w.shape[1], tn), lambda i, j: (l, 0, j)) for w in w_list]
    in_specs += [pl.BlockSpec((tm, tn), functools.partial(lambda i, j, b: (i, b * nb + j), b=b))
                 for b in range(N_BRANCH)]
    return pl.pallas_call(
        _merge_kernel,
        grid=(m // tm, nb),
        in_specs=in_specs,
        out_specs=pl.BlockSpec((tm, tn), lambda i, j: (i, j)),
        out_shape=jax.ShapeDtypeStruct((m, n), BF16),
        compiler_params=_cparams(("parallel", "arbitrary")),
        name="merge",
    )(*a_list, *w_list, gm, gm, gm)


def _resid_kernel(a_ref, w_ref, x_ref, o_ref):
    o_ref[...] = x_ref[...] + _dot(a_ref[...], w_ref[...])


def out_proj(a, w, l, x, tm, tn):
    m, k = a.shape
    n = w.shape[2]
    return pl.pallas_call(
        _resid_kernel,
        grid=(m // tm, n // tn),
        in_specs=[_row_tile((tm, k)),
                  pl.BlockSpec((None, k, tn), lambda i, j: (l, 0, j)),
                  pl.BlockSpec((tm, tn), lambda i, j: (i, j))],
        out_specs=pl.BlockSpec((tm, tn), lambda i, j: (i, j)),
        out_shape=jax.ShapeDtypeStruct((m, n), F32),
        compiler_params=_cparams(("parallel", "arbitrary")),
        name="out_proj",
    )(a, w, x)


def _softmax_rows(s, mask):
    s = jnp.where(mask, s, NEG)
    m = jnp.max(s, axis=-1, keepdims=True)
    e = jnp.where(mask, jnp.exp(s - m), 0.0)
    d = jnp.sum(e, axis=-1, keepdims=True)
    return e / jnp.where(d > 0.0, d, 1.0)


def _compress_combine(acc, const, nvalid):
    nsub = acc.shape[0]
    top = acc[:, :LANES]
    bot = pltpu.roll(acc[:, LANES:], nsub - 1, 0)
    row = lax.broadcasted_iota(jnp.int32, (nsub, LANES), 0)
    return jnp.where(row < nvalid, top + bot + const, 0.0)


def _pe_const(pe_ref, w_ref):
    return jnp.dot(pe_ref[...], w_ref[...], precision=lax.Precision.HIGHEST,
                   preferred_element_type=F32)[0:1, :]


def _cmp_prompt_kernel(k_ref, v_ref, wk_ref, wv_ref, pek_ref, pev_ref, wkf_ref, wvf_ref, g_ref,
                       ko_ref, vo_ref):
    nsub = ko_ref.shape[0]
    acck = jnp.zeros((nsub, 2 * HEAD_DIM), F32)
    accv = jnp.zeros((nsub, 2 * HEAD_DIM), F32)
    for l in range(CMP_STRIDE):
        xk = k_ref[pl.ds(l, nsub, stride=CMP_STRIDE), :].astype(BF16)
        xv = v_ref[pl.ds(l, nsub, stride=CMP_STRIDE), :].astype(BF16)
        acck = acck + _dot(xk, wk_ref[l])
        accv = accv + _dot(xv, wv_ref[l])
    kc = _compress_combine(acck, _pe_const(pek_ref, wkf_ref), nsub - 1)
    ms = jnp.mean(kc * kc, axis=-1, keepdims=True)
    ko_ref[...] = (kc * lax.rsqrt(ms + EPS)) * g_ref[...]
    vo_ref[...] = _compress_combine(accv, _pe_const(pev_ref, wvf_ref), nsub - 1)


def _cmp_weights(w_cmp, pe):
    wcat = jnp.concatenate([w_cmp[:CMP_STRIDE], w_cmp[CMP_STRIDE:]], axis=2).astype(BF16)
    pe8 = jnp.broadcast_to(pe.reshape(1, CMP_BLK * HEAD_DIM), (8, CMP_BLK * HEAD_DIM))
    return wcat, pe8, w_cmp.reshape(CMP_BLK * HEAD_DIM, HEAD_DIM)


def compress_prompt(zf, b, t, wk, wv, gk):
    nsub = t // CMP_STRIDE
    wkc, pek, wkf = wk
    wvc, pev, wvf = wv
    full = lambda a: pl.BlockSpec(a.shape, lambda i, g: (0,) * a.ndim)
    head = lambda off: pl.BlockSpec((t, HEAD_DIM), lambda i, g: (i, off // HEAD_DIM + g))
    out = jax.ShapeDtypeStruct((b * N_KV_HEADS, nsub, HEAD_DIM), F32)
    return pl.pallas_call(
        _cmp_prompt_kernel,
        grid=(b, N_KV_HEADS),
        in_specs=[head(0), head(KV_W),
                  full(wkc), full(wvc), full(pek), full(pev), full(wkf), full(wvf), full(gk)],
        out_specs=[pl.BlockSpec((None, nsub, HEAD_DIM), lambda i, g: (i * N_KV_HEADS + g, 0, 0))] * 2,
        out_shape=[out, out],
        compiler_params=_cparams(("parallel", "parallel")),
        name="cmp_prompt",
    )(zf, zf, wkc, wvc, pek, pev, wkf, wvf, gk)


def _sel_matrix(nc_pad, nc, ns_pad, ns):
    c0 = np.arange(nc_pad) * CMP_STRIDE
    s0 = np.arange(ns_pad) * SEL_BLK
    m = (c0[:, None] < s0[None, :] + SEL_BLK) & (c0[:, None] + CMP_BLK > s0[None, :])
    m &= (np.arange(nc_pad)[:, None] < nc) & (np.arange(ns_pad)[None, :] < ns)
    return jnp.asarray(m.astype(np.float32))


EXP_C = SCALE * 1.4426950408889634


def _select_mask_t(imp_t, tpos_row, ns):
    blk = lax.broadcasted_iota(jnp.int32, imp_t.shape, 0)
    cur = tpos_row // SEL_BLK
    forced = (blk == 0) | (blk == cur) | (blk == cur - 1)
    impm = jnp.where(blk > cur, -BIG, jnp.where(forced, BIG, imp_t))
    rank = jnp.zeros(imp_t.shape, F32)
    for s2 in range(ns):
        row = impm[s2:s2 + 1, :]
        gt = jnp.where(row > impm, 1.0, 0.0)
        ge = jnp.where(row >= impm, 1.0, 0.0)
        rank = rank + jnp.where(blk > s2, ge, gt)
    return jnp.where(rank < float(min(N_SEL, ns)), 1.0, 0.0)


def _softmax_bias(s3, bias, valid):
    sm = s3 + bias[None]
    m = jnp.max(sm, axis=-1, keepdims=True)
    e = jnp.exp2((sm - m) * EXP_C)
    inv = 1.0 / jnp.sum(e, axis=-1, keepdims=True)
    if valid is not None:
        inv = jnp.where(valid[None], inv, 0.0)
    return e * inv


def _attn_prompt_kernel(q_ref, kc_ref, vc_ref, ks_ref, vs_ref, kw_ref, vw_ref, gn_ref, gate_ref,
                        msel_ref, e_ref, o_ref, *, tq, kvc, ns):
    g = pl.program_id(1)
    t0 = pl.program_id(2) * tq
    nh = Q_PER_KV
    rows = nh * tq
    q4 = jnp.concatenate([q_ref[:, h * HEAD_DIM:(h + 1) * HEAD_DIM] for h in range(nh)], axis=0)
    tpos = t0 + lax.broadcasted_iota(jnp.int32, (tq, 1), 0)
    tpos_row = t0 + lax.broadcasted_iota(jnp.int32, (1, tq), 1)

    ncp = kc_ref.shape[0]
    endpos = lax.broadcasted_iota(jnp.int32, (1, ncp), 1) * CMP_STRIDE + (CMP_BLK - 1)
    bias_c = jnp.where(endpos <= tpos, 0.0, NEG)
    p3 = _softmax_bias(_dot_nt(q4, kc_ref[...].astype(BF16)).reshape(nh, tq, ncp), bias_c,
                       tpos >= CMP_BLK - 1)
    o_cmp = _dot(p3.reshape(rows, ncp).astype(BF16), vc_ref[...].astype(BF16))
    imp = jnp.dot(jnp.sum(p3, axis=0), msel_ref[...], precision=lax.Precision.HIGHEST,
                  preferred_element_type=F32)
    ns8 = -(-ns // 8) * 8
    sel_t = _select_mask_t(imp.T[:ns8], tpos_row, ns)
    sel = jnp.concatenate([sel_t, jnp.zeros((imp.shape[1] - ns8, tq), F32)], axis=0).T.astype(BF16)

    def body(c, carry):
        m, l, acc = carry
        k0 = pl.multiple_of(c * kvc, kvc)
        kpos = k0 + lax.broadcasted_iota(jnp.int32, (1, kvc), 1)
        bias = jnp.where(kpos <= tpos, (_dot(sel, e_ref[c]) - 1.0) * -NEG, NEG)
        sm = _dot_nt(q4, ks_ref[pl.ds(k0, kvc), :]).reshape(nh, tq, kvc) + bias[None]
        m_new = jnp.maximum(m, jnp.max(sm, axis=-1, keepdims=True))
        alpha = jnp.exp2((m - m_new) * EXP_C)
        e = jnp.exp2((sm - m_new) * EXP_C)
        l = alpha * l + jnp.sum(e, axis=-1, keepdims=True)
        pv = _dot(e.reshape(rows, kvc).astype(BF16), vs_ref[pl.ds(k0, kvc), :])
        return m_new, l, alpha * acc + pv.reshape(nh, tq, HEAD_DIM)

    nch = (t0 + tq + kvc - 1) // kvc
    m, l, acc = lax.fori_loop(0, nch, body, (jnp.full((nh, tq, 1), NEG, F32), jnp.zeros((nh, tq, 1), F32),
                                             jnp.zeros((nh, tq, HEAD_DIM), F32)))
    o_sel = (acc * jnp.where(l > 0.0, 1.0 / l, 0.0)).reshape(rows, HEAD_DIM)

    nwin = WINDOW + tq
    w0 = pl.multiple_of(jnp.maximum(t0 - WINDOW, 0), tq)
    diff = tpos - (w0 + lax.broadcasted_iota(jnp.int32, (1, nwin), 1))
    bias_w = jnp.where((diff >= 0) & (diff <= WINDOW), 0.0, NEG)
    pw = _softmax_bias(_dot_nt(q4, kw_ref[pl.ds(w0, nwin), :]).reshape(nh, tq, nwin), bias_w, None)
    o_win = _dot(pw.reshape(rows, nwin).astype(BF16), vw_ref[pl.ds(w0, nwin), :])

    gs = _sigmoid(gn_ref[...])
    lane = lax.broadcasted_iota(jnp.int32, gs.shape, 1)
    for h in range(nh):
        r = slice(h * tq, (h + 1) * tq)
        o = None
        for br, ob in enumerate((o_cmp, o_sel, o_win)):
            cidx = br * N_HEADS + g * nh + h
            gcol = jnp.sum(jnp.where(lane == cidx, gs, 0.0), axis=-1, keepdims=True)
            o = gcol * ob[r] if o is None else o + gcol * ob[r]
        hs = slice(h * HEAD_DIM, (h + 1) * HEAD_DIM)
        o_ref[:, hs] = (o * gate_ref[:, hs].astype(F32)).astype(o_ref.dtype)


def attn_prompt(zq, zkv, kcmp, vcmp, gn, gate, b, t):
    tq, kvc = 128, 512
    assert t // CMP_STRIDE == LANES and t % kvc == 0 and t >= WINDOW + tq
    nq = t // tq
    ns = -(-t // SEL_BLK)
    nc = t // CMP_STRIDE - CMP_BLK // CMP_STRIDE + 1
    msel = _sel_matrix(LANES, nc, LANES, ns)
    kk = np.arange(t)
    e3 = (kk[None, :] // SEL_BLK == np.arange(LANES)[:, None]).astype(np.float32)
    e3 = jnp.asarray(e3.reshape(LANES, t // kvc, kvc).transpose(1, 0, 2), BF16)
    hb = lambda off: (lambda bi, g, i: (bi, off // HEAD_DIM + g))
    kv_spec = lambda off: pl.BlockSpec((t, HEAD_DIM), hb(off))
    cm_spec = pl.BlockSpec((None, t // CMP_STRIDE, HEAD_DIM), lambda bi, g, i: (bi * N_KV_HEADS + g, 0, 0))
    row4 = pl.BlockSpec((tq, Q_PER_KV * HEAD_DIM), lambda bi, g, i: (bi * nq + i, g))
    return pl.pallas_call(
        functools.partial(_attn_prompt_kernel, tq=tq, kvc=kvc, ns=ns),
        grid=(b, N_KV_HEADS, nq),
        in_specs=[row4, cm_spec, cm_spec,
                  kv_spec(2 * KV_W), kv_spec(3 * KV_W), kv_spec(4 * KV_W), kv_spec(5 * KV_W),
                  pl.BlockSpec((tq, LANES), lambda bi, g, i: (bi * nq + i, 0)),
                  row4,
                  pl.BlockSpec(msel.shape, lambda bi, g, i: (0, 0)),
                  pl.BlockSpec(e3.shape, lambda bi, g, i: (0, 0, 0))],
        out_specs=row4,
        out_shape=jax.ShapeDtypeStruct((b * t, NSA_W), BF16),
        compiler_params=_cparams(("parallel", "parallel", "arbitrary")),
        name="attn_prompt",
    )(zq, kcmp, vcmp, zkv, zkv, zkv, zkv, gn, gate, msel, e3)


def _layer_norm(v, g, b):
    vc = v - jnp.mean(v, axis=-1, keepdims=True)
    var = jnp.mean(vc * vc, axis=-1, keepdims=True)
    return vc * lax.rsqrt(var + EPS) * g + b


def _mlp_prompt_kernel(u_ref, v_ref, gate_ref, lg_ref, lb_ref, ws_ref, bst_ref, o_ref):
    tm = u_ref.shape[0]
    gw = ws_ref.shape[1]
    vb = _layer_norm(v_ref[...], lg_ref[...], lb_ref[...]).astype(BF16)
    row = lax.broadcasted_iota(jnp.int32, (CHUNK, CHUNK), 0)
    col = lax.broadcasted_iota(jnp.int32, (CHUNK, CHUNK), 1)
    for gi in range(MLP_GROUPS):
        cs = slice(gi * gw, (gi + 1) * gw)
        wsg = jnp.where(row >= col, ws_ref[gi], 0.0).astype(BF16)
        bias = bst_ref[:, gi:gi + 1]
        for ch in range(tm // CHUNK):
            rs = slice(ch * CHUNK, (ch + 1) * CHUNK)
            mixed = _dot(wsg, vb[rs, cs]) + bias
            o_ref[rs, cs] = (u_ref[rs, cs] * mixed * gate_ref[rs, cs].astype(F32)).astype(o_ref.dtype)


def mlp_prompt(uv, gate, ln_g, ln_b, w_s, b_s, tm):
    m, w2 = uv.shape
    w = w2 // 2
    assert w // MLP_GROUPS == CHUNK == w_s.shape[1]
    full = lambda a: pl.BlockSpec(a.shape, lambda i: (0,) * a.ndim)
    bst = b_s.T
    return pl.pallas_call(
        _mlp_prompt_kernel,
        grid=(m // tm,),
        in_specs=[pl.BlockSpec((tm, w), lambda i: (i, 0)), pl.BlockSpec((tm, w), lambda i: (i, 1)),
                  pl.BlockSpec((tm, w), lambda i: (i, 0)),
                  pl.BlockSpec((1, w), lambda i: (0, 0)), pl.BlockSpec((1, w), lambda i: (0, 0)),
                  full(w_s), full(bst)],
        out_specs=pl.BlockSpec((tm, w), lambda i: (i, 0)),
        out_shape=jax.ShapeDtypeStruct((m, w), BF16),
        compiler_params=_cparams(("parallel",)),
        name="mlp_prompt",
    )(uv, uv, gate, ln_g.reshape(1, w), ln_b.reshape(1, w), w_s, bst)


def _softplus(x):
    return jnp.maximum(x, 0.0) + jnp.log1p(jnp.exp(-jnp.abs(x)))


def _lru_gates(xc, wa_ref, wx_ref, ba, bx, lam):
    nh = xc.shape[1] // HEAD_DIM
    rs, is_ = [], []
    for hh in range(nh):
        xh = xc[:, hh * HEAD_DIM:(hh + 1) * HEAD_DIM].astype(BF16)
        rs.append(_dot(xh, wa_ref[hh]))
        is_.append(_dot(xh, wx_ref[hh]))
    r = _sigmoid(jnp.concatenate(rs, axis=1) + ba)
    i = _sigmoid(jnp.concatenate(is_, axis=1) + bx)
    log_a = -LRU_C * r * _softplus(-lam)
    th = jnp.tanh(log_a)
    return jnp.exp(log_a), jnp.sqrt(-2.0 * th / (1.0 - th)) * (i * xc)


def _lru_prompt_kernel(x_ref, gate_ref, cw_ref, cb_ref, wa_ref, wx_ref, ba_ref, bx_ref, lam_ref,
                       o_ref, h_ref, a_scr, u_scr):
    t, wb = x_ref.shape
    x = x_ref[...]
    row = lax.broadcasted_iota(jnp.int32, (t, wb), 0)
    xc = cb_ref[...] + x * cw_ref[CONV_W - 1:CONV_W, :]
    for d in range(1, CONV_W):
        xs = jnp.where(row >= d, pltpu.roll(x, d, 0), 0.0)
        xc = xc + xs * cw_ref[CONV_W - 1 - d:CONV_W - d, :]
    a, u = _lru_gates(xc, wa_ref, wx_ref, ba_ref[...], bx_ref[...], lam_ref[...])
    a_scr[...] = a
    u_scr[...] = u
    row8 = lax.broadcasted_iota(jnp.int32, (8, wb), 0)

    def body(bi, h):
        r0 = pl.multiple_of(bi * 8, 8)
        a8 = a_scr[pl.ds(r0, 8), :]
        u8 = u_scr[pl.ds(r0, 8), :]
        for d in (1, 2, 4):
            a_sh = jnp.where(row8 >= d, pltpu.roll(a8, d, 0), 1.0)
            u_sh = jnp.where(row8 >= d, pltpu.roll(u8, d, 0), 0.0)
            u8 = a8 * u_sh + u8
            a8 = a8 * a_sh
        h8 = a8 * h + u8
        u_scr[pl.ds(r0, 8), :] = h8
        return h8[7:8, :]

    h = lax.fori_loop(0, t // 8, body, jnp.zeros((1, wb), F32))
    h_ref[...] = h
    o_ref[...] = (u_scr[...] * gate_ref[...].astype(F32)).astype(o_ref.dtype)


def lru_prompt(xl, gate, conv_w, conv_b, wa, wx, ba, bx, lam, b, t):
    w = xl.shape[1]
    wb = 512
    nh = wb // HEAD_DIM
    assert w % wb == 0 and t % 8 == 0 and wa.shape[1] == HEAD_DIM
    vec = lambda a: a.reshape(1, w)
    vspec = pl.BlockSpec((1, wb), lambda i, j: (0, j))
    blk = pl.BlockSpec((t, wb), lambda i, j: (i, j))
    hspec = pl.BlockSpec((nh, HEAD_DIM, HEAD_DIM), lambda i, j: (j, 0, 0))
    return pl.pallas_call(
        _lru_prompt_kernel,
        grid=(b, w // wb),
        in_specs=[blk, blk, pl.BlockSpec((CONV_W, wb), lambda i, j: (0, j)), vspec,
                  hspec, hspec, vspec, vspec, vspec],
        out_specs=[blk, pl.BlockSpec((None, 1, wb), lambda i, j: (i, 0, j))],
        out_shape=[jax.ShapeDtypeStruct((b * t, w), BF16), jax.ShapeDtypeStruct((b, 1, w), F32)],
        scratch_shapes=[pltpu.VMEM((t, wb), F32), pltpu.VMEM((t, wb), F32)],
        compiler_params=_cparams(("parallel", "parallel")),
        name="lru_prompt",
    )(xl, gate, conv_w, vec(conv_b), wa.astype(BF16), wx.astype(BF16), vec(ba), vec(bx), vec(lam))


ATT_W = NSA_W + 6 * KV_W
SEG = 512


PACK_TAIL = 64


def _pack_kernel(a_ref, b_ref, o_ref, *, gn_tile, ngate):
    j = pl.program_id(0)

    @pl.when(j < gn_tile)
    def _():
        o_ref[...] = a_ref[...].T.astype(o_ref.dtype)

    @pl.when(j == gn_tile)
    def _():
        row = lax.broadcasted_iota(jnp.int32, a_ref.shape, 0)
        o_ref[...] = jnp.where(row < ngate, a_ref[...], 0.0).T.astype(o_ref.dtype)

    @pl.when(j > gn_tile)
    def _():
        o_ref[...] = jnp.concatenate([a_ref[ngate:, :], b_ref[:ngate, :]], axis=0).T.astype(o_ref.dtype)


def _pack_w_in(w_in, l, d_model, mlp_w, lru_w):
    ngate = N_BRANCH * N_HEADS
    sizes = (("att", ATT_W), ("gn", SEG), ("gate_nsa", NSA_W), ("uv", 2 * mlp_w), ("gate_mlp", mlp_w),
             ("xl", lru_w), ("gate_lru", lru_w), ("gm", N_BRANCH * d_model))
    offs, acc = {}, 0
    for name, size in sizes:
        offs[name] = (acc, size)
        acc += size
    wt = jnp.swapaxes(w_in, 1, 2)
    n_in, k = wt.shape[1:]
    assert acc == n_in - ngate + SEG and ATT_W % SEG == 0 and ngate <= PACK_TAIL and ngate % 16 == 0
    gn_tile = ATT_W // SEG
    w = pl.pallas_call(
        functools.partial(_pack_kernel, gn_tile=gn_tile, ngate=ngate),
        grid=(acc // SEG,),
        in_specs=[pl.BlockSpec((None, SEG, k), lambda j: (l, jnp.where(j > gn_tile, j - 1, j), 0)),
                  pl.BlockSpec((None, PACK_TAIL, k), lambda j: (l, (SEG // PACK_TAIL) * j, 0))],
        out_specs=pl.BlockSpec((k, SEG), lambda j: (0, j)),
        out_shape=jax.ShapeDtypeStruct((k, acc), BF16),
        compiler_params=_cparams(("parallel",)),
        name="pack_w",
    )(wt, wt)
    return w, offs


def _norm_vectors(q_g, k_g):
    one = jnp.ones((KV_W,), F32)
    zero = jnp.zeros((KV_W,), F32)
    gain = jnp.concatenate([jnp.tile(q_g, N_HEADS), one, one, jnp.tile(k_g[1], N_KV_HEADS), one,
                            jnp.tile(k_g[2], N_KV_HEADS), one])
    flag = jnp.concatenate([jnp.ones((NSA_W,), F32), zero, zero, one, zero, one, zero])
    return gain.reshape(1, ATT_W), flag.reshape(1, ATT_W)


def _in_proj(x2d, xs2d, lw, tm):
    h = rms_rows(x2d, lw["norm_g"], min(tm, 512))
    hs = rms_rows(xs2d, lw["norm_g"], xs2d.shape[0])
    w, offs = lw["w_pack"]
    p = lambda name, act, dt, tn=SEG, n=None, off=0, tm=tm, **kw: proj(
        h, hs, w, offs[name][0] + off, n or offs[name][1], act, dt if isinstance(dt, tuple) else (dt,),
        tm, tn, **kw)
    gain, flag = lw["gain"], lw["flag"]
    zq, zqs = p("att", "norm", BF16, n=NSA_W, gain=gain[:, :NSA_W], flag=flag[:, :NSA_W])
    kvf, kvb, kvh, kvs = p("att", "norm", (F32, BF16, "heads"), n=ATT_W - NSA_W, off=NSA_W, tm=tm // 2,
                           gain=gain[:, NSA_W:], flag=flag[:, NSA_W:])
    z, zs = dict(q=zq, kvf=kvf, kvb=kvb, kvh=kvh), dict(q=zqs, kv=kvs)
    for name, act, dt, kw in (("gn", "none", F32, dict(tn=LANES, n=LANES)), ("gate_nsa", "silu", BF16, {}),
                              ("uv", "gelu", F32, {}), ("gate_mlp", "silu", BF16, {}), ("xl", "none", F32, {}),
                              ("gate_lru", "silu", BF16, {}), ("gm", "sigmoid", BF16, {})):
        z[name], zs[name] = p(name, act, dt, **kw)
    return z, zs


ROW_TILE = 2048


def prompt_layer(x2d, z, lw, l, b, t, w_buf):
    tm = ROW_TILE
    kcmp, vcmp = compress_prompt(z["kvf"], b, t, lw["cmp_k"], lw["cmp_v"], lw["k_norm_g"][0:1])
    o_nsa = attn_prompt(z["q"], z["kvb"], kcmp, vcmp, z["gn"], z["gate_nsa"], b, t)
    o_mlp = mlp_prompt(z["uv"], z["gate_mlp"], lw["mlp_ln_g"], lw["mlp_ln_b"], lw["w_s"], lw["b_s"], 512)
    o_lru, h_last = lru_prompt(z["xl"], z["gate_lru"], lw["conv_w"], lw["conv_b"], lw["lru_wa"], lw["lru_wx"],
                               lw["lru_ba"], lw["lru_bx"], lw["lru_lambda"], b, t)
    m = merge([o_nsa, o_mlp, o_lru], lw["w_br"], l, z["gm"], tm, SEG)
    y = out_proj(m, lw["w_out"], l, x2d, tm, SEG)
    kv = lambda i: z["kvh"][i].reshape(b, t, N_KV_HEADS, HEAD_DIM)
    xl = z["xl"].reshape(b, t, -1)
    assert t >= w_buf and t >= CONV_W - 1
    state = dict(cmp_k=kv(0), cmp_v=kv(1), sel_k=kv(2), sel_v=kv(3),
                 win_k=kv(4)[:, t - w_buf:], win_v=kv(5)[:, t - w_buf:],
                 lru_h=h_last.reshape(b, -1), lru_conv=xl[:, t - (CONV_W - 1):])
    return y, state


PAGES_PER_STEP = 32


def _cmp_paged_kernel(pt_ref, *refs, pg):
    k_refs, v_refs = refs[:pg], refs[pg:2 * pg]
    wk_ref, wv_ref, ko_ref, vo_ref, r_scr = refs[2 * pg:]
    rows = k_refs[0].shape[0]
    sub = N_KV_HEADS * CMP_STRIDE
    nsp = rows // sub
    for page_refs, w_ref, o_ref in ((k_refs, wk_ref, ko_ref), (v_refs, wv_ref, vo_ref)):
        blocks = [jnp.concatenate([r[pl.ds(sub * i + 8 * m, 8), :] for m in range(sub // 8)], axis=1)
                  for r in page_refs for i in range(nsp)]
        res = _dot(jnp.concatenate(blocks, axis=0).astype(BF16), w_ref[...])
        n = res.shape[0]
        both = res[:, :2 * HEAD_DIM] + pltpu.roll(res[:, 2 * HEAD_DIM:], n - N_KV_HEADS, 0)
        r_scr[0] = both[:, :HEAD_DIM]
        r_scr[1] = both[:, HEAD_DIM:]
        for g in range(N_KV_HEADS):
            o_ref[g] = jnp.concatenate([r_scr[0, pl.ds(g, n // 8, stride=8), :],
                                        r_scr[1, pl.ds(g, n // 8, stride=8), :]], axis=1)


def _cmp_paged_weights(wcat):
    return wcat.reshape(CMP_STRIDE // 2, 2, HEAD_DIM, 2 * HEAD_DIM).transpose(0, 2, 1, 3).reshape(
        CMP_STRIDE // 2 * HEAD_DIM, 4 * HEAD_DIM)


def compress_paged(pool_k, pool_v, page_table, wkc, wvc, page0):
    b, n_pages = page_table.shape
    rows = pool_k.shape[1]
    pg = PAGES_PER_STEP
    assert n_pages % pg == 0 and N_KV_HEADS * 2 == 8
    nsp = rows // N_KV_HEADS // CMP_STRIDE
    wkc, wvc = _cmp_paged_weights(wkc), _cmp_paged_weights(wvc)
    page_spec = lambda p: pl.BlockSpec((None, rows, HEAD_DIM), lambda i, c, pt: (page0 + pt[i, c * pg + p], 0, 0))
    full = lambda a: pl.BlockSpec(a.shape, lambda i, c, pt: (0,) * a.ndim)
    out = jax.ShapeDtypeStruct((b, N_KV_HEADS, n_pages * nsp, 2 * HEAD_DIM), F32)
    ospec = pl.BlockSpec((None, N_KV_HEADS, pg * nsp, 2 * HEAD_DIM), lambda i, c, pt: (i, 0, c, 0))
    return pl.pallas_call(
        functools.partial(_cmp_paged_kernel, pg=pg),
        grid_spec=pltpu.PrefetchScalarGridSpec(
            num_scalar_prefetch=1,
            grid=(b, n_pages // pg),
            in_specs=[page_spec(p) for p in range(pg)] * 2 + [full(wkc), full(wvc)],
            out_specs=[ospec, ospec],
            scratch_shapes=[pltpu.VMEM((2, pg * nsp * 8, HEAD_DIM), F32)]),
        out_shape=[out, out],
        compiler_params=_cparams(("parallel", "arbitrary")),
        name="cmp_paged",
    )(page_table, *([pool_k] * pg), *([pool_v] * pg), wkc, wvc)


def _cmp_sample_kernel(tbk_ref, tbv_ref, q_ref, pek_ref, pev_ref, wkf_ref, wvf_ref, gk_ref, msel_ref,
                       o_ref, idx_ref, *, tpos, nc, ns):
    ck = _pe_const(pek_ref, wkf_ref)
    cv = _pe_const(pev_ref, wvf_ref)
    q = q_ref[...].astype(BF16)
    nh = q.shape[0]
    nsub = tbk_ref.shape[1]
    hrow = lax.broadcasted_iota(jnp.int32, (nh, 1), 0)
    blk = lax.broadcasted_iota(jnp.int32, (1, nsub), 1)
    cmask = (blk * CMP_STRIDE + (CMP_BLK - 1) <= tpos) & (blk < nc)
    row8 = lax.broadcasted_iota(jnp.int32, (8, nsub), 0)
    o = jnp.zeros((nh, HEAD_DIM), F32)
    psum = jnp.zeros((8, nsub), F32)
    for g in range(N_KV_HEADS):
        kc = _compress_combine(tbk_ref[g], ck, nc)
        ms = jnp.mean(kc * kc, axis=-1, keepdims=True)
        kc = (kc * lax.rsqrt(ms + EPS)) * gk_ref[...]
        vc = _compress_combine(tbv_ref[g], cv, nc)
        p = _softmax_rows(_dot_nt(q, kc.astype(BF16)) * SCALE, cmask & (hrow // Q_PER_KV == g))
        o = o + _dot(p.astype(BF16), vc.astype(BF16))
        psum = jnp.where(row8 == g, jnp.sum(p, axis=0, keepdims=True), psum)
    o_ref[...] = o
    imp = jnp.dot(psum, msel_ref[...], precision=lax.Precision.HIGHEST, preferred_element_type=F32)
    lane = lax.broadcasted_iota(jnp.int32, imp.shape, 1)
    cur = tpos // SEL_BLK
    forced = (lane == 0) | (lane == cur) | (lane == cur - 1)
    impm = jnp.where(lane > cur, -BIG, jnp.where(forced, BIG, imp))
    impm = jnp.where(lane < ns, impm, -jnp.inf)
    lane_f = lane.astype(F32)
    out_lane = lax.broadcasted_iota(jnp.int32, idx_ref.shape, 1)
    idxs = jnp.zeros(idx_ref.shape, F32)
    for j in range(min(N_SEL, ns)):
        mx = jnp.max(impm, axis=-1, keepdims=True)
        am = jnp.min(jnp.where(impm == mx, lane_f, 1e9), axis=-1, keepdims=True)
        idxs = jnp.where(out_lane == j, am, idxs)
        impm = jnp.where(lane_f == am, -jnp.inf, impm)
    idx_ref[...] = idxs.astype(jnp.int32)


def cmp_sample(tbk, tbv, q3, ck, cv, gk, tpos):
    b, nkv, nsub, _ = tbk.shape
    nc = (tpos + 1) // CMP_STRIDE - CMP_BLK // CMP_STRIDE + 1
    ns = -(-(tpos + 1) // SEL_BLK)
    ns_pad = -(-ns // LANES) * LANES
    assert nc <= nsub
    msel = _sel_matrix(nsub, nc, ns_pad, ns)
    _, pek, wkf = ck
    _, pev, wvf = cv
    full = lambda a: pl.BlockSpec(a.shape, lambda i: (0,) * a.ndim)
    tb_spec = pl.BlockSpec((None, nkv, nsub, 2 * HEAD_DIM), lambda i: (i, 0, 0, 0))
    return pl.pallas_call(
        functools.partial(_cmp_sample_kernel, tpos=tpos, nc=nc, ns=ns),
        grid=(b,),
        in_specs=[tb_spec, tb_spec, pl.BlockSpec((None, N_HEADS, HEAD_DIM), lambda i: (i, 0, 0)),
                  full(pek), full(pev), full(wkf), full(wvf), full(gk), full(msel)],
        out_specs=[pl.BlockSpec((None, N_HEADS, HEAD_DIM), lambda i: (i, 0, 0)),
                   pl.BlockSpec((None, 8, LANES), lambda i: (i, 0, 0))],
        out_shape=[jax.ShapeDtypeStruct((b, N_HEADS, HEAD_DIM), F32),
                   jax.ShapeDtypeStruct((b, 8, LANES), jnp.int32)],
        compiler_params=_cparams(("parallel",)),
        name="cmp_sample",
    )(tbk, tbv, q3, pek, pev, wkf, wvf, gk, msel)


def _attend_with_new(q, k_all, v_all, mask, k_new, v_new, new_ok):
    s = jnp.where(mask, _dot_nt(q.astype(BF16), k_all) * SCALE, NEG)
    s_new = jnp.where(new_ok, jnp.sum(q * k_new, axis=-1, keepdims=True) * SCALE, NEG)
    m = jnp.maximum(jnp.max(s, axis=-1, keepdims=True), s_new)
    e = jnp.where(mask, jnp.exp(s - m), 0.0)
    e_new = jnp.where(new_ok, jnp.exp(s_new - m), 0.0)
    d = jnp.sum(e, axis=-1, keepdims=True) + e_new
    o = _dot(e.astype(BF16), v_all) + e_new * v_new
    return o / jnp.where(d > 0.0, d, 1.0)


def _attn_sample_kernel(pt_ref, idx_ref, *refs, nblk, tpos, past_len, w_buf):
    k_refs, v_refs = refs[:nblk], refs[nblk:2 * nblk]
    (q_ref, ocmp_ref, kw_ref, vw_ref, ksn_ref, vsn_ref, kwn_ref, vwn_ref, gn_ref, gate_ref,
     o_ref) = refs[2 * nblk:]
    b = pl.program_id(0)
    g = pl.program_id(1)
    q = q_ref[...]
    rb = k_refs[0].shape[0]
    r = lax.broadcasted_iota(jnp.int32, (1, rb), 1)
    tok, hd = r // N_KV_HEADS, r % N_KV_HEADS
    masks = []
    new_sel = False
    for j in range(nblk):
        s = idx_ref[b, g * nblk + j]
        kpos = s * SEL_BLK + tok
        masks.append((hd == g) & (kpos <= tpos) & (kpos < past_len))
        new_sel = new_sel | (s == past_len // SEL_BLK)
    k_all = jnp.concatenate([kr[...].astype(BF16) for kr in k_refs], axis=0)
    v_all = jnp.concatenate([vr[...].astype(BF16) for vr in v_refs], axis=0)
    o_sel = _attend_with_new(q, k_all, v_all, jnp.concatenate(masks, axis=1),
                             ksn_ref[...], vsn_ref[...], new_sel & (past_len <= tpos))
    rw = lax.broadcasted_iota(jnp.int32, (1, kw_ref.shape[0]), 1)
    diff = tpos - (past_len - w_buf + rw // N_KV_HEADS)
    wmask = (rw % N_KV_HEADS == g) & (diff >= 0) & (diff <= WINDOW)
    o_win = _attend_with_new(q, kw_ref[...].astype(BF16), vw_ref[...].astype(BF16), wmask,
                             kwn_ref[...], vwn_ref[...], tpos - past_len <= WINDOW)
    gs = _sigmoid(gn_ref[...])
    o = gs[:, 0:1] * ocmp_ref[...] + gs[:, 1:2] * o_sel + gs[:, 2:3] * o_win
    o_ref[...] = o * gate_ref[...]


def attn_sample(pool_k, pool_v, page_table, idx, q4, ocmp4, win_k, win_v, ks_new, vs_new, kw_new, vw_new,
                gn4, gate4, past_len, w_buf, page_size, page0, seq0):
    b, n_pages = page_table.shape
    nblk = idx.shape[1] // N_KV_HEADS
    rb = pool_k.shape[1]
    halves = page_size // SEL_BLK
    tpos = past_len

    def blk_map(j):
        def f(i, g, pt, ix):
            s = ix[i, g * nblk + j]
            page = pt[i, jnp.minimum(s // halves, n_pages - 1)]
            return ((page0 + page) * halves + s % halves, 0, 0)
        return f

    blk_spec = lambda j: pl.BlockSpec((None, rb, HEAD_DIM), blk_map(j))
    per_bg = lambda a: pl.BlockSpec((None, None) + a.shape[2:], lambda i, g, pt, ix: (i, g, 0, 0))
    per_b = lambda a: pl.BlockSpec((None,) + a.shape[1:], lambda i, g, pt, ix: (seq0 + i, 0, 0))
    small = (q4, ocmp4)
    news = (ks_new, vs_new, kw_new, vw_new, gn4, gate4)
    return pl.pallas_call(
        functools.partial(_attn_sample_kernel, nblk=nblk, tpos=tpos, past_len=past_len, w_buf=w_buf),
        grid_spec=pltpu.PrefetchScalarGridSpec(
            num_scalar_prefetch=2,
            grid=(b, N_KV_HEADS),
            in_specs=[blk_spec(j) for j in range(nblk)] * 2 + [per_bg(a) for a in small]
            + [per_b(win_k), per_b(win_v)] + [per_bg(a) for a in news],
            out_specs=per_bg(q4)),
        out_shape=jax.ShapeDtypeStruct(q4.shape, F32),
        compiler_params=_cparams(("parallel", "arbitrary")),
        name="attn_sample",
    )(page_table, idx, *([pool_k] * nblk), *([pool_v] * nblk), q4, ocmp4, win_k, win_v, *news)


def _point_sample_kernel(u_ref, v_ref, gmlp_ref, lg_ref, lb_ref, ws0_ref, bs0_ref,
                         x_ref, glru_ref, buf_ref, h0_ref, cw_ref, cb_ref, wa_ref, wx_ref, ba_ref, bx_ref,
                         lam_ref, omlp_ref, vrow_ref, olru_ref, hnew_ref):
    vn = _layer_norm(v_ref[...], lg_ref[...], lb_ref[...])
    vrow_ref[...] = vn
    mixed = ws0_ref[...] * vn + bs0_ref[...]
    omlp_ref[...] = (u_ref[...] * mixed * gmlp_ref[...].astype(F32)).astype(omlp_ref.dtype)
    x = x_ref[...]
    xc = cb_ref[...] + x * cw_ref[CONV_W - 1:CONV_W, :]
    for j in range(CONV_W - 1):
        xc = xc + buf_ref[j] * cw_ref[j:j + 1, :]
    a, u = _lru_gates(xc, wa_ref, wx_ref, ba_ref[...], bx_ref[...], lam_ref[...])
    h = a * h0_ref[...] + u
    hnew_ref[...] = h
    olru_ref[...] = (h * glru_ref[...].astype(F32)).astype(olru_ref.dtype)


def point_sample(uv, gate_mlp, xl, gate_lru, buf_t, h0, lw):
    b, w2 = uv.shape
    w = w2 // 2
    gw = w // MLP_GROUPS
    vec = lambda a: a.reshape(1, -1)
    ws0 = vec(jnp.repeat(lw["w_s"][:, 0, 0], gw))
    bs0 = vec(jnp.repeat(lw["b_s"][:, 0], gw))
    args = (uv[:, :w], uv[:, w:], gate_mlp, vec(lw["mlp_ln_g"]), vec(lw["mlp_ln_b"]), ws0, bs0,
            xl, gate_lru, buf_t, h0, lw["conv_w"], vec(lw["conv_b"]), lw["lru_wa"].astype(BF16),
            lw["lru_wx"].astype(BF16), vec(lw["lru_ba"]), vec(lw["lru_bx"]), vec(lw["lru_lambda"]))
    lw_ = xl.shape[1]
    return pl.pallas_call(
        _point_sample_kernel,
        out_shape=[jax.ShapeDtypeStruct((b, w), BF16), jax.ShapeDtypeStruct((b, w), F32),
                   jax.ShapeDtypeStruct((b, lw_), BF16), jax.ShapeDtypeStruct((b, lw_), F32)],
        compiler_params=pltpu.CompilerParams(vmem_limit_bytes=VMEM_LIMIT),
        name="point_sample",
    )(*args)


def sample_layer(x2d, z, lw, l, caches, page_table, past_len, w_buf):
    b = x2d.shape[0]
    seg = lambda i: z["kv"][:, i * KV_W:(i + 1) * KV_W]
    kvh = lambda a: a.reshape(b, N_KV_HEADS, 1, HEAD_DIM)
    n_pool, page_size = caches["cmp_k"].shape[1:3]
    page0 = l * n_pool
    pool3 = lambda a: a.reshape(-1, page_size * N_KV_HEADS, HEAD_DIM)
    halves = lambda a: a.reshape(-1, SEL_BLK * N_KV_HEADS, HEAD_DIM)
    tbk, tbv = compress_paged(pool3(caches["cmp_k"]), pool3(caches["cmp_v"]), page_table,
                              lw["cmp_k"][0], lw["cmp_v"][0], page0)
    q3 = z["q"].reshape(b, N_HEADS, HEAD_DIM)
    o_cmp, idx = cmp_sample(tbk, tbv, q3, lw["cmp_k"], lw["cmp_v"], lw["k_norm_g"][0:1], past_len)
    n_sel = min(N_SEL, -(-(past_len + 1) // SEL_BLK))
    idx = idx[:, :N_KV_HEADS, :n_sel].reshape(b, N_KV_HEADS * n_sel)
    four = lambda a: a.reshape(b, N_KV_HEADS, Q_PER_KV, -1)
    gn4 = z["gn"][:, :N_BRANCH * N_HEADS].reshape(b, N_BRANCH, N_KV_HEADS, Q_PER_KV).transpose(0, 2, 3, 1)
    win3 = lambda a: a.reshape(-1, w_buf * N_KV_HEADS, HEAD_DIM)
    o_nsa = attn_sample(halves(caches["sel_k"]), halves(caches["sel_v"]), page_table, idx,
                        four(q3), four(o_cmp), win3(caches["win_k"]), win3(caches["win_v"]),
                        kvh(seg(2)), kvh(seg(3)), kvh(seg(4)), kvh(seg(5)),
                        gn4, four(z["gate_nsa"].astype(F32)), past_len, w_buf, page_size, page0, l * b)
    o_mlp, v_rows, o_lru, h_new = point_sample(
        z["uv"], z["gate_mlp"], z["xl"], z["gate_lru"], caches["lru_conv"].transpose(1, 0, 2),
        caches["lru_h"].astype(F32), lw)
    m = merge([o_nsa.reshape(b, NSA_W).astype(BF16), o_mlp, o_lru], lw["w_br"], l, z["gm"], b, SEG)
    y = out_proj(m, lw["w_out"], l, x2d, b, SEG)
    tok = lambda a: a.reshape(b, 1, N_KV_HEADS, HEAD_DIM)
    state = dict(cmp_k=tok(seg(0)), cmp_v=tok(seg(1)), sel_k=tok(seg(2)), sel_v=tok(seg(3)),
                 win_k=jnp.concatenate([caches["win_k"][l], tok(seg(4))], axis=1)[:, -w_buf:],
                 win_v=jnp.concatenate([caches["win_v"][l], tok(seg(5))], axis=1)[:, -w_buf:],
                 lru_h=h_new, lru_conv=jnp.concatenate([caches["lru_conv"], z["xl"][:, None]], axis=1)[:, 1:],
                 mlp_v=v_rows[:, None])
    return y, state


def _layer_weights(l, p):
    d_model = p["w_in"].shape[1]
    mlp_w = p["mlp_ln_g"].shape[1]
    lru_w = p["lru_lambda"].shape[1]
    names = ("norm_g", "q_norm_g", "k_norm_g", "mlp_ln_g", "mlp_ln_b", "w_s", "b_s", "conv_w", "conv_b",
             "lru_wa", "lru_ba", "lru_wx", "lru_bx", "lru_lambda")
    lw = {n: p[n][l] for n in names}
    lw["w_pack"] = _pack_w_in(p["w_in"], l, d_model, mlp_w, lru_w)
    lw["gain"], lw["flag"] = _norm_vectors(p["q_norm_g"][l], p["k_norm_g"][l])
    lw["cmp_k"] = _cmp_weights(p["w_cmp_k"][l], p["cmp_pe_k"][l])
    lw["cmp_v"] = _cmp_weights(p["w_cmp_v"][l], p["cmp_pe_v"][l])
    lw["w_br"] = [p[n] for n in ("w_br_nsa_bf", "w_br_mlp_bf", "w_br_lru_bf")]
    lw["w_out"] = p["w_out_bf"]
    return lw


def kernel(x_prompt, x_sample, cache_cmp_k, cache_cmp_v, cache_sel_k, cache_sel_v, state_win_k, state_win_v,
           state_lru_h, state_lru_conv, page_table, norm_g, w_in, q_norm_g, k_norm_g, cmp_pe_k, cmp_pe_v,
           w_cmp_k, w_cmp_v, mlp_ln_g, mlp_ln_b, w_s, b_s, conv_w, conv_b, lru_wa, lru_ba, lru_wx, lru_bx,
           lru_lambda, w_br_nsa, w_br_mlp, w_br_lru, w_out):
    params = dict(norm_g=norm_g, w_in=w_in, q_norm_g=q_norm_g, k_norm_g=k_norm_g, cmp_pe_k=cmp_pe_k,
                  cmp_pe_v=cmp_pe_v, w_cmp_k=w_cmp_k, w_cmp_v=w_cmp_v, mlp_ln_g=mlp_ln_g, mlp_ln_b=mlp_ln_b,
                  w_s=w_s, b_s=b_s, conv_w=conv_w, conv_b=conv_b, lru_wa=lru_wa, lru_ba=lru_ba, lru_wx=lru_wx,
                  lru_bx=lru_bx, lru_lambda=lru_lambda, w_br_nsa=w_br_nsa, w_br_mlp=w_br_mlp,
                  w_br_lru=w_br_lru, w_out=w_out)
    depth = w_in.shape[0]
    b, t, d = x_prompt.shape
    bs, ts, _ = x_sample.shape
    assert ts == 1
    w_buf = state_win_k.shape[2]
    past_len = page_table.shape[1] * cache_cmp_k.shape[2]
    yp = x_prompt.reshape(b * t, d)
    ys = x_sample.reshape(bs * ts, d)
    p_st, s_st = [], []
    for n in ("w_br_nsa", "w_br_mlp", "w_br_lru", "w_out"):
        params[n + "_bf"] = params[n].astype(BF16)
    for l in range(depth):
        lw = _layer_weights(l, params)
        z, zs = _in_proj(yp, ys, lw, ROW_TILE)
        yp, sp = prompt_layer(yp, z, lw, l, b, t, w_buf)
        caches = dict(cmp_k=cache_cmp_k, cmp_v=cache_cmp_v, sel_k=cache_sel_k, sel_v=cache_sel_v,
                      win_k=state_win_k, win_v=state_win_v, lru_h=state_lru_h[l],
                      lru_conv=state_lru_conv[l])
        ys, ss = sample_layer(ys, zs, lw, l, caches, page_table, past_len, w_buf)
        p_st.append(sp)
        s_st.append(ss)
    stk = lambda sts, name: jnp.stack([st[name] for st in sts])
    names = ("cmp_k", "cmp_v", "sel_k", "sel_v", "win_k", "win_v", "lru_h", "lru_conv")
    return ((yp.reshape(b, t, d), ys.reshape(bs, ts, d))
            + tuple(stk(p_st, n) for n in names)
            + tuple(stk(s_st, n) for n in names + ("mlp_v",)))
```

```python
import functools

import numpy as np
import jax
import jax.numpy as jnp
from jax import lax
from jax.experimental import pallas as pl
from jax.experimental.pallas import tpu as pltpu

N_HEADS = 16
HEAD_DIM = 128
N_KV_HEADS = 4
Q_PER_KV = N_HEADS // N_KV_HEADS
NSA_W = N_HEADS * HEAD_DIM
KV_W = N_KV_HEADS * HEAD_DIM
CMP_BLK = 32
CMP_STRIDE = 16
SEL_BLK = 64
N_SEL = 16
WINDOW = 512
CHUNK = 128
MLP_GROUPS = 8
LRU_HEADS = 8
CONV_W = 4
LRU_C = 8.0
N_BRANCH = 3
EPS = 1e-6
BIG = 1e9
NEG = -1e30
SCALE = HEAD_DIM ** -0.5

LANES = 128
VMEM_LIMIT = 56 * 1024 * 1024

BF16 = jnp.bfloat16
F32 = jnp.float32


def _cparams(sem):
    return pltpu.CompilerParams(dimension_semantics=sem, vmem_limit_bytes=VMEM_LIMIT)


def _gelu(x):
    return 0.5 * x * (1.0 + jnp.tanh(0.7978845608028654 * (x + 0.044715 * (x * x * x))))


def _sigmoid(x):
    return 1.0 / (1.0 + jnp.exp(-x))


def _silu(x):
    return x * _sigmoid(x)


def _dot(a, b):
    return jnp.dot(a, b, preferred_element_type=F32)


def _dot_nt(a, b):
    return lax.dot_general(a, b, (((1,), (1,)), ((), ())), preferred_element_type=F32)


def _rms_kernel(x_ref, g_ref, o_ref):
    x = x_ref[...]
    ms = jnp.mean(x * x, axis=-1, keepdims=True)
    o_ref[...] = ((x * lax.rsqrt(ms + EPS)) * g_ref[...]).astype(o_ref.dtype)


def rms_rows(x, g, tm):
    m, d = x.shape
    return pl.pallas_call(
        _rms_kernel,
        grid=(m // tm,),
        in_specs=[pl.BlockSpec((tm, d), lambda i: (i, 0)),
                  pl.BlockSpec((1, d), lambda i: (0, 0))],
        out_specs=pl.BlockSpec((tm, d), lambda i: (i, 0)),
        out_shape=jax.ShapeDtypeStruct((m, d), BF16),
        compiler_params=_cparams(("parallel",)),
        name="rms_rows",
    )(x, g.reshape(1, d))


def _row_tile(shape):
    return pl.BlockSpec(shape, lambda i, j: (i, 0))


def _proj_act(acc, act, gain_ref, flag_ref):
    if act == "silu":
        return _silu(acc)
    if act == "gelu":
        return _gelu(acc)
    if act == "sigmoid":
        return _sigmoid(acc)
    if act == "norm":
        outs = []
        for c in range(acc.shape[1] // LANES):
            sl = slice(c * LANES, (c + 1) * LANES)
            z = acc[:, sl]
            ms = jnp.mean(z * z, axis=-1, keepdims=True)
            zn = (z * lax.rsqrt(ms + EPS)) * gain_ref[:, sl]
            outs.append(jnp.where(flag_ref[:, sl] > 0.5, zn, z))
        return jnp.concatenate(outs, axis=1)
    return acc


def _proj_kernel(a_ref, a2_ref, w_ref, *rest, act, outs):
    if act == "norm":
        gain_ref, flag_ref = rest[:2]
        rest = rest[2:]
    else:
        gain_ref = flag_ref = None
    w = w_ref[...]
    z = _proj_act(_dot(a_ref[...], w), act, gain_ref, flag_ref)
    for kind, o_ref in zip(outs, rest):
        if kind == "heads":
            nh = z.shape[1] // HEAD_DIM
            for g in range(nh):
                o_ref[pl.ds(g, z.shape[0], stride=nh), :] = z[:, g * HEAD_DIM:(g + 1) * HEAD_DIM]
        else:
            o_ref[...] = z.astype(o_ref.dtype)
    o2_ref = rest[-1]

    @pl.when(pl.program_id(0) == 0)
    def _():
        o2_ref[...] = _proj_act(_dot(a2_ref[...], w), act, gain_ref, flag_ref).astype(o2_ref.dtype)


def proj(a, a2, w, row0, nrows, act, outs, tm, tn, gain=None, flag=None):
    m, k = a.shape
    m2 = a2.shape[0]
    assert row0 % tn == 0 and nrows % tn == 0 and m % tm == 0
    jb = row0 // tn
    nj = nrows // tn
    in_specs = [_row_tile((tm, k)),
                pl.BlockSpec((m2, k), lambda i, j: (0, 0)),
                pl.BlockSpec((k, tn), lambda i, j: (0, jb + j))]
    args = [a, a2, w]
    if act == "norm":
        in_specs += [pl.BlockSpec((1, tn), lambda i, j: (0, j))] * 2
        args += [gain, flag]
    nh = tn // HEAD_DIM
    out_specs, out_shape = [], []
    for kind in outs:
        if kind == "heads":
            out_specs.append(pl.BlockSpec((None, tm * nh, HEAD_DIM), lambda i, j: (j, i, 0)))
            out_shape.append(jax.ShapeDtypeStruct((nj, m * nh, HEAD_DIM), F32))
        else:
            out_specs.append(pl.BlockSpec((tm, tn), lambda i, j: (i, j)))
            out_shape.append(jax.ShapeDtypeStruct((m, nrows), kind))
    out_specs.append(pl.BlockSpec((m2, tn), lambda i, j: (0, jnp.where(i == 0, j, nj - 1))))
    out_shape.append(jax.ShapeDtypeStruct((m2, nrows), F32 if act == "norm" else outs[0]))
    return pl.pallas_call(
        functools.partial(_proj_kernel, act=act, outs=tuple(outs)),
        grid=(m // tm, nj),
        in_specs=in_specs,
        out_specs=out_specs,
        out_shape=out_shape,
        compiler_params=_cparams(("arbitrary", "arbitrary")),
        name="proj_" + act,
    )(*args)


def _merge_kernel(a1, a2, a3, w1, w2, w3, g1, g2, g3, o_ref):
    m = g1[...].astype(F32) * _dot(a1[...], w1[...])
    m = m + g2[...].astype(F32) * _dot(a2[...], w2[...])
    m = m + g3[...].astype(F32) * _dot(a3[...], w3[...])
    o_ref[...] = m.astype(o_ref.dtype)


def merge(a_list, w_list, l, gm, tm, tn):
    m = a_list[0].shape[0]
    n = w_list[0].shape[2]
    nb = n // tn
    in_specs = [_row_tile((tm, a.shape[1])) for a in a_list]
    in_specs += [pl.BlockSpec((None, w.shape[1], tn), lambda i, j: (l, 0, j)) for w in w_list]
    in_specs += [pl.BlockSpec((tm, tn), functools.partial(lambda i, j, b: (i, b * nb + j), b=b))
                 for b in range(N_BRANCH)]
    return pl.pallas_call(
        _merge_kernel,
        grid=(m // tm, nb),
        in_specs=in_specs,
        out_specs=pl.BlockSpec((tm, tn), lambda i, j: (i, j)),
        out_shape=jax.ShapeDtypeStruct((m, n), BF16),
        compiler_params=_cparams(("parallel", "arbitrary")),
        name="merge",
    )(*a_list, *w_list, gm, gm, gm)


def _resid_kernel(a_ref, w_ref, x_ref, o_ref):
    o_ref[...] = x_ref[...] + _dot(a_ref[...], w_ref[...])


def out_proj(a, w, l, x, tm, tn):
    m, k = a.shape
    n = w.shape[2]
    return pl.pallas_call(
        _resid_kernel,
        grid=(m // tm, n // tn),
        in_specs=[_row_tile((tm, k)),
                  pl.BlockSpec((None, k, tn), lambda i, j: (l, 0, j)),
                  pl.BlockSpec((tm, tn), lambda i, j: (i, j))],
        out_specs=pl.BlockSpec((tm, tn), lambda i, j: (i, j)),
        out_shape=jax.ShapeDtypeStruct((m, n), F32),
        compiler_params=_cparams(("parallel", "arbitrary")),
        name="out_proj",
    )(a, w, x)


def _softmax_rows(s, mask):
    s = jnp.where(mask, s, NEG)
    m = jnp.max(s, axis=-1, keepdims=True)
    e = jnp.where(mask, jnp.exp(s - m), 0.0)
    d = jnp.sum(e, axis=-1, keepdims=True)
    return e / jnp.where(d > 0.0, d, 1.0)


def _compress_combine(acc, const, nvalid):
    nsub = acc.shape[0]
    top = acc[:, :LANES]
    bot = pltpu.roll(acc[:, LANES:], nsub - 1, 0)
    row = lax.broadcasted_iota(jnp.int32, (nsub, LANES), 0)
    return jnp.where(row < nvalid, top + bot + const, 0.0)


def _pe_const(pe_ref, w_ref):
    return jnp.dot(pe_ref[...], w_ref[...], precision=lax.Precision.HIGHEST,
                   preferred_element_type=F32)[0:1, :]


def _cmp_prompt_kernel(k_ref, v_ref, wk_ref, wv_ref, pek_ref, pev_ref, wkf_ref, wvf_ref, g_ref,
                       ko_ref, vo_ref):
    nsub = ko_ref.shape[0]
    acck = jnp.zeros((nsub, 2 * HEAD_DIM), F32)
    accv = jnp.zeros((nsub, 2 * HEAD_DIM), F32)
    for l in range(CMP_STRIDE):
        xk = k_ref[pl.ds(l, nsub, stride=CMP_STRIDE), :].astype(BF16)
        xv = v_ref[pl.ds(l, nsub, stride=CMP_STRIDE), :].astype(BF16)
        acck = acck + _dot(xk, wk_ref[l])
        accv = accv + _dot(xv, wv_ref[l])
    kc = _compress_combine(acck, _pe_const(pek_ref, wkf_ref), nsub - 1)
    ms = jnp.mean(kc * kc, axis=-1, keepdims=True)
    ko_ref[...] = (kc * lax.rsqrt(ms + EPS)) * g_ref[...]
    vo_ref[...] = _compress_combine(accv, _pe_const(pev_ref, wvf_ref), nsub - 1)


def _cmp_weights(w_cmp, pe):
    wcat = jnp.concatenate([w_cmp[:CMP_STRIDE], w_cmp[CMP_STRIDE:]], axis=2).astype(BF16)
    pe8 = jnp.broadcast_to(pe.reshape(1, CMP_BLK * HEAD_DIM), (8, CMP_BLK * HEAD_DIM))
    return wcat, pe8, w_cmp.reshape(CMP_BLK * HEAD_DIM, HEAD_DIM)


def compress_prompt(zf, b, t, wk, wv, gk):
    nsub = t // CMP_STRIDE
    wkc, pek, wkf = wk
    wvc, pev, wvf = wv
    full = lambda a: pl.BlockSpec(a.shape, lambda i, g: (0,) * a.ndim)
    head = lambda off: pl.BlockSpec((t, HEAD_DIM), lambda i, g: (i, off // HEAD_DIM + g))
    out = jax.ShapeDtypeStruct((b * N_KV_HEADS, nsub, HEAD_DIM), F32)
    return pl.pallas_call(
        _cmp_prompt_kernel,
        grid=(b, N_KV_HEADS),
        in_specs=[head(0), head(KV_W),
                  full(wkc), full(wvc), full(pek), full(pev), full(wkf), full(wvf), full(gk)],
        out_specs=[pl.BlockSpec((None, nsub, HEAD_DIM), lambda i, g: (i * N_KV_HEADS + g, 0, 0))] * 2,
        out_shape=[out, out],
        compiler_params=_cparams(("parallel", "parallel")),
        name="cmp_prompt",
    )(zf, zf, wkc, wvc, pek, pev, wkf, wvf, gk)


def _sel_matrix(nc_pad, nc, ns_pad, ns):
    c0 = np.arange(nc_pad) * CMP_STRIDE
    s0 = np.arange(ns_pad) * SEL_BLK
    m = (c0[:, None] < s0[None, :] + SEL_BLK) & (c0[:, None] + CMP_BLK > s0[None, :])
    m &= (np.arange(nc_pad)[:, None] < nc) & (np.arange(ns_pad)[None, :] < ns)
    return jnp.asarray(m.astype(np.float32))


EXP_C = SCALE * 1.4426950408889634


def _select_mask_t(imp_t, tpos_row, ns):
    blk = lax.broadcasted_iota(jnp.int32, imp_t.shape, 0)
    cur = tpos_row // SEL_BLK
    forced = (blk == 0) | (blk == cur) | (blk == cur - 1)
    impm = jnp.where(blk > cur, -BIG, jnp.where(forced, BIG, imp_t))
    rank = jnp.zeros(imp_t.shape, F32)
    for s2 in range(ns):
        row = impm[s2:s2 + 1, :]
        gt = jnp.where(row > impm, 1.0, 0.0)
        ge = jnp.where(row >= impm, 1.0, 0.0)
        rank = rank + jnp.where(blk > s2, ge, gt)
    return jnp.where(rank < float(min(N_SEL, ns)), 1.0, 0.0)


def _softmax_bias(s3, bias, valid):
    sm = s3 + bias[None]
    m = jnp.max(sm, axis=-1, keepdims=True)
    e = jnp.exp2((sm - m) * EXP_C)
    inv = 1.0 / jnp.sum(e, axis=-1, keepdims=True)
    if valid is not None:
        inv = jnp.where(valid[None], inv, 0.0)
    return e * inv


def _attn_prompt_kernel(q_ref, kc_ref, vc_ref, ks_ref, vs_ref, kw_ref, vw_ref, gn_ref, gate_ref,
                        msel_ref, e_ref, o_ref, *, tq, kvc, ns):
    g = pl.program_id(1)
    t0 = pl.program_id(2) * tq
    nh = Q_PER_KV
    rows = nh * tq
    q4 = jnp.concatenate([q_ref[:, h * HEAD_DIM:(h + 1) * HEAD_DIM] for h in range(nh)], axis=0)
    tpos = t0 + lax.broadcasted_iota(jnp.int32, (tq, 1), 0)
    tpos_row = t0 + lax.broadcasted_iota(jnp.int32, (1, tq), 1)

    ncp = kc_ref.shape[0]
    endpos = lax.broadcasted_iota(jnp.int32, (1, ncp), 1) * CMP_STRIDE + (CMP_BLK - 1)
    bias_c = jnp.where(endpos <= tpos, 0.0, NEG)
    p3 = _softmax_bias(_dot_nt(q4, kc_ref[...].astype(BF16)).reshape(nh, tq, ncp), bias_c,
                       tpos >= CMP_BLK - 1)
    o_cmp = _dot(p3.reshape(rows, ncp).astype(BF16), vc_ref[...].astype(BF16))
    imp = jnp.dot(jnp.sum(p3, axis=0), msel_ref[...], precision=lax.Precision.HIGHEST,
                  preferred_element_type=F32)
    ns8 = -(-ns // 8) * 8
    sel_t = _select_mask_t(imp.T[:ns8], tpos_row, ns)
    sel = jnp.concatenate([sel_t, jnp.zeros((imp.shape[1] - ns8, tq), F32)], axis=0).T.astype(BF16)

    def body(c, carry):
        m, l, acc = carry
        k0 = pl.multiple_of(c * kvc, kvc)
        kpos = k0 + lax.broadcasted_iota(jnp.int32, (1, kvc), 1)
        bias = jnp.where(kpos <= tpos, (_dot(sel, e_ref[c]) - 1.0) * -NEG, NEG)
        sm = _dot_nt(q4, ks_ref[pl.ds(k0, kvc), :]).reshape(nh, tq, kvc) + bias[None]
        m_new = jnp.maximum(m, jnp.max(sm, axis=-1, keepdims=True))
        alpha = jnp.exp2((m - m_new) * EXP_C)
        e = jnp.exp2((sm - m_new) * EXP_C)
        l = alpha * l + jnp.sum(e, axis=-1, keepdims=True)
        pv = _dot(e.reshape(rows, kvc).astype(BF16), vs_ref[pl.ds(k0, kvc), :])
        return m_new, l, alpha * acc + pv.reshape(nh, tq, HEAD_DIM)

    nch = (t0 + tq + kvc - 1) // kvc
    m, l, acc = lax.fori_loop(0, nch, body, (jnp.full((nh, tq, 1), NEG, F32), jnp.zeros((nh, tq, 1), F32),
                                             jnp.zeros((nh, tq, HEAD_DIM), F32)))
    o_sel = (acc * jnp.where(l > 0.0, 1.0 / l, 0.0)).reshape(rows, HEAD_DIM)

    nwin = WINDOW + tq
    w0 = pl.multiple_of(jnp.maximum(t0 - WINDOW, 0), tq)
    diff = tpos - (w0 + lax.broadcasted_iota(jnp.int32, (1, nwin), 1))
    bias_w = jnp.where((diff >= 0) & (diff <= WINDOW), 0.0, NEG)
    pw = _softmax_bias(_dot_nt(q4, kw_ref[pl.ds(w0, nwin), :]).reshape(nh, tq, nwin), bias_w, None)
    o_win = _dot(pw.reshape(rows, nwin).astype(BF16), vw_ref[pl.ds(w0, nwin), :])

    gs = _sigmoid(gn_ref[...])
    lane = lax.broadcasted_iota(jnp.int32, gs.shape, 1)
    for h in range(nh):
        r = slice(h * tq, (h + 1) * tq)
        o = None
        for br, ob in enumerate((o_cmp, o_sel, o_win)):
            cidx = br * N_HEADS + g * nh + h
            gcol = jnp.sum(jnp.where(lane == cidx, gs, 0.0), axis=-1, keepdims=True)
            o = gcol * ob[r] if o is None else o + gcol * ob[r]
        hs = slice(h * HEAD_DIM, (h + 1) * HEAD_DIM)
        o_ref[:, hs] = (o * gate_ref[:, hs].astype(F32)).astype(o_ref.dtype)


def attn_prompt(zq, zkv, kcmp, vcmp, gn, gate, b, t):
    tq, kvc = 256, 512
    assert t // CMP_STRIDE == LANES and t % kvc == 0 and t >= WINDOW + tq
    nq = t // tq
    ns = -(-t // SEL_BLK)
    nc = t // CMP_STRIDE - CMP_BLK // CMP_STRIDE + 1
    msel = _sel_matrix(LANES, nc, LANES, ns)
    kk = np.arange(t)
    e3 = (kk[None, :] // SEL_BLK == np.arange(LANES)[:, None]).astype(np.float32)
    e3 = jnp.asarray(e3.reshape(LANES, t // kvc, kvc).transpose(1, 0, 2), BF16)
    hb = lambda off: (lambda bi, g, i: (bi, off // HEAD_DIM + g))
    kv_spec = lambda off: pl.BlockSpec((t, HEAD_DIM), hb(off))
    cm_spec = pl.BlockSpec((None, t // CMP_STRIDE, HEAD_DIM), lambda bi, g, i: (bi * N_KV_HEADS + g, 0, 0))
    row4 = pl.BlockSpec((tq, Q_PER_KV * HEAD_DIM), lambda bi, g, i: (bi * nq + i, g))
    return pl.pallas_call(
        functools.partial(_attn_prompt_kernel, tq=tq, kvc=kvc, ns=ns),
        grid=(b, N_KV_HEADS, nq),
        in_specs=[row4, cm_spec, cm_spec,
                  kv_spec(2 * KV_W), kv_spec(3 * KV_W), kv_spec(4 * KV_W), kv_spec(5 * KV_W),
                  pl.BlockSpec((tq, LANES), lambda bi, g, i: (bi * nq + i, 0)),
                  row4,
                  pl.BlockSpec(msel.shape, lambda bi, g, i: (0, 0)),
                  pl.BlockSpec(e3.shape, lambda bi, g, i: (0, 0, 0))],
        out_specs=row4,
        out_shape=jax.ShapeDtypeStruct((b * t, NSA_W), BF16),
        compiler_params=_cparams(("parallel", "parallel", "arbitrary")),
        name="attn_prompt",
    )(zq, kcmp, vcmp, zkv, zkv, zkv, zkv, gn, gate, msel, e3)


def _layer_norm(v, g, b):
    vc = v - jnp.mean(v, axis=-1, keepdims=True)
    var = jnp.mean(vc * vc, axis=-1, keepdims=True)
    return vc * lax.rsqrt(var + EPS) * g + b


def _mlp_prompt_kernel(u_ref, v_ref, gate_ref, lg_ref, lb_ref, ws_ref, bst_ref, o_ref):
    tm = u_ref.shape[0]
    gw = ws_ref.shape[1]
    vb = _layer_norm(v_ref[...], lg_ref[...], lb_ref[...]).astype(BF16)
    row = lax.broadcasted_iota(jnp.int32, (CHUNK, CHUNK), 0)
    col = lax.broadcasted_iota(jnp.int32, (CHUNK, CHUNK), 1)
    for gi in range(MLP_GROUPS):
        cs = slice(gi * gw, (gi + 1) * gw)
        wsg = jnp.where(row >= col, ws_ref[gi], 0.0).astype(BF16)
        bias = bst_ref[:, gi:gi + 1]
        for ch in range(tm // CHUNK):
            rs = slice(ch * CHUNK, (ch + 1) * CHUNK)
            mixed = _dot(wsg, vb[rs, cs]) + bias
            o_ref[rs, cs] = (u_ref[rs, cs] * mixed * gate_ref[rs, cs].astype(F32)).astype(o_ref.dtype)


def mlp_prompt(uv, gate, ln_g, ln_b, w_s, b_s, tm):
    m, w2 = uv.shape
    w = w2 // 2
    assert w // MLP_GROUPS == CHUNK == w_s.shape[1]
    full = lambda a: pl.BlockSpec(a.shape, lambda i: (0,) * a.ndim)
    bst = b_s.T
    return pl.pallas_call(
        _mlp_prompt_kernel,
        grid=(m // tm,),
        in_specs=[pl.BlockSpec((tm, w), lambda i: (i, 0)), pl.BlockSpec((tm, w), lambda i: (i, 1)),
                  pl.BlockSpec((tm, w), lambda i: (i, 0)),
                  pl.BlockSpec((1, w), lambda i: (0, 0)), pl.BlockSpec((1, w), lambda i: (0, 0)),
                  full(w_s), full(bst)],
        out_specs=pl.BlockSpec((tm, w), lambda i: (i, 0)),
        out_shape=jax.ShapeDtypeStruct((m, w), BF16),
        compiler_params=_cparams(("parallel",)),
        name="mlp_prompt",
    )(uv, uv, gate, ln_g.reshape(1, w), ln_b.reshape(1, w), w_s, bst)


def _softplus(x):
    return jnp.maximum(x, 0.0) + jnp.log1p(jnp.exp(-jnp.abs(x)))


def _lru_gates(xc, wa_ref, wx_ref, ba, bx, lam):
    nh = xc.shape[1] // HEAD_DIM
    rs, is_ = [], []
    for hh in range(nh):
        xh = xc[:, hh * HEAD_DIM:(hh + 1) * HEAD_DIM].astype(BF16)
        rs.append(_dot(xh, wa_ref[hh]))
        is_.append(_dot(xh, wx_ref[hh]))
    r = _sigmoid(jnp.concatenate(rs, axis=1) + ba)
    i = _sigmoid(jnp.concatenate(is_, axis=1) + bx)
    log_a = -LRU_C * r * _softplus(-lam)
    th = jnp.tanh(log_a)
    return jnp.exp(log_a), jnp.sqrt(-2.0 * th / (1.0 - th)) * (i * xc)


def _lru_prompt_kernel(x_ref, gate_ref, cw_ref, cb_ref, wa_ref, wx_ref, ba_ref, bx_ref, lam_ref,
                       o_ref, h_ref, a_scr, u_scr):
    t, wb = x_ref.shape
    x = x_ref[...]
    row = lax.broadcasted_iota(jnp.int32, (t, wb), 0)
    xc = cb_ref[...] + x * cw_ref[CONV_W - 1:CONV_W, :]
    for d in range(1, CONV_W):
        xs = jnp.where(row >= d, pltpu.roll(x, d, 0), 0.0)
        xc = xc + xs * cw_ref[CONV_W - 1 - d:CONV_W - d, :]
    a, u = _lru_gates(xc, wa_ref, wx_ref, ba_ref[...], bx_ref[...], lam_ref[...])
    a_scr[...] = a
    u_scr[...] = u
    row8 = lax.broadcasted_iota(jnp.int32, (8, wb), 0)

    def body(bi, h):
        r0 = pl.multiple_of(bi * 8, 8)
        a8 = a_scr[pl.ds(r0, 8), :]
        u8 = u_scr[pl.ds(r0, 8), :]
        for d in (1, 2, 4):
            a_sh = jnp.where(row8 >= d, pltpu.roll(a8, d, 0), 1.0)
            u_sh = jnp.where(row8 >= d, pltpu.roll(u8, d, 0), 0.0)
            u8 = a8 * u_sh + u8
            a8 = a8 * a_sh
        h8 = a8 * h + u8
        u_scr[pl.ds(r0, 8), :] = h8
        return h8[7:8, :]

    h = lax.fori_loop(0, t // 8, body, jnp.zeros((1, wb), F32))
    h_ref[...] = h
    o_ref[...] = (u_scr[...] * gate_ref[...].astype(F32)).astype(o_ref.dtype)


def lru_prompt(xl, gate, conv_w, conv_b, wa, wx, ba, bx, lam, b, t):
    w = xl.shape[1]
    wb = 512
    nh = wb // HEAD_DIM
    assert w % wb == 0 and t % 8 == 0 and wa.shape[1] == HEAD_DIM
    vec = lambda a: a.reshape(1, w)
    vspec = pl.BlockSpec((1, wb), lambda i, j: (0, j))
    blk = pl.BlockSpec((t, wb), lambda i, j: (i, j))
    hspec = pl.BlockSpec((nh, HEAD_DIM, HEAD_DIM), lambda i, j: (j, 0, 0))
    return pl.pallas_call(
        _lru_prompt_kernel,
        grid=(b, w // wb),
        in_specs=[blk, blk, pl.BlockSpec((CONV_W, wb), lambda i, j: (0, j)), vspec,
                  hspec, hspec, vspec, vspec, vspec],
        out_specs=[blk, pl.BlockSpec((None, 1, wb), lambda i, j: (i, 0, j))],
        out_shape=[jax.ShapeDtypeStruct((b * t, w), BF16), jax.ShapeDtypeStruct((b, 1, w), F32)],
        scratch_shapes=[pltpu.VMEM((t, wb), F32), pltpu.VMEM((t, wb), F32)],
        compiler_params=_cparams(("parallel", "parallel")),
        name="lru_prompt",
    )(xl, gate, conv_w, vec(conv_b), wa.astype(BF16), wx.astype(BF16), vec(ba), vec(bx), vec(lam))


ATT_W = NSA_W + 6 * KV_W
SEG = 512


PACK_TAIL = 64


def _pack_kernel(a_ref, b_ref, o_ref, *, gn_tile, ngate):
    j = pl.program_id(0)

    @pl.when(j < gn_tile)
    def _():
        o_ref[...] = a_ref[...].T.astype(o_ref.dtype)

    @pl.when(j == gn_tile)
    def _():
        row = lax.broadcasted_iota(jnp.int32, a_ref.shape, 0)
        o_ref[...] = jnp.where(row < ngate, a_ref[...], 0.0).T.astype(o_ref.dtype)

    @pl.when(j > gn_tile)
    def _():
        o_ref[...] = jnp.concatenate([a_ref[ngate:, :], b_ref[:ngate, :]], axis=0).T.astype(o_ref.dtype)


def _pack_w_in(w_in, l, d_model, mlp_w, lru_w):
    ngate = N_BRANCH * N_HEADS
    sizes = (("att", ATT_W), ("gn", SEG), ("gate_nsa", NSA_W), ("uv", 2 * mlp_w), ("gate_mlp", mlp_w),
             ("xl", lru_w), ("gate_lru", lru_w), ("gm", N_BRANCH * d_model))
    offs, acc = {}, 0
    for name, size in sizes:
        offs[name] = (acc, size)
        acc += size
    wt = jnp.swapaxes(w_in, 1, 2)
    n_in, k = wt.shape[1:]
    assert acc == n_in - ngate + SEG and ATT_W % SEG == 0 and ngate <= PACK_TAIL and ngate % 16 == 0
    gn_tile = ATT_W // SEG
    w = pl.pallas_call(
        functools.partial(_pack_kernel, gn_tile=gn_tile, ngate=ngate),
        grid=(acc // SEG,),
        in_specs=[pl.BlockSpec((None, SEG, k), lambda j: (l, jnp.where(j > gn_tile, j - 1, j), 0)),
                  pl.BlockSpec((None, PACK_TAIL, k), lambda j: (l, (SEG // PACK_TAIL) * j, 0))],
        out_specs=pl.BlockSpec((k, SEG), lambda j: (0, j)),
        out_shape=jax.ShapeDtypeStruct((k, acc), BF16),
        compiler_params=_cparams(("parallel",)),
        name="pack_w",
    )(wt, wt)
    return w, offs


def _norm_vectors(q_g, k_g):
    one = jnp.ones((KV_W,), F32)
    zero = jnp.zeros((KV_W,), F32)
    gain = jnp.concatenate([jnp.tile(q_g, N_HEADS), one, one, jnp.tile(k_g[1], N_KV_HEADS), one,
                            jnp.tile(k_g[2], N_KV_HEADS), one])
    flag = jnp.concatenate([jnp.ones((NSA_W,), F32), zero, zero, one, zero, one, zero])
    return gain.reshape(1, ATT_W), flag.reshape(1, ATT_W)


def _in_proj(x2d, xs2d, lw, tm):
    h = rms_rows(x2d, lw["norm_g"], min(tm, 512))
    hs = rms_rows(xs2d, lw["norm_g"], xs2d.shape[0])
    w, offs = lw["w_pack"]
    p = lambda name, act, dt, tn=SEG, n=None, off=0, tm=tm, **kw: proj(
        h, hs, w, offs[name][0] + off, n or offs[name][1], act, dt if isinstance(dt, tuple) else (dt,),
        tm, tn, **kw)
    gain, flag = lw["gain"], lw["flag"]
    zq, zqs = p("att", "norm", BF16, n=NSA_W, gain=gain[:, :NSA_W], flag=flag[:, :NSA_W])
    kvf, kvb, kvh, kvs = p("att", "norm", (F32, BF16, "heads"), n=ATT_W - NSA_W, off=NSA_W,
                           gain=gain[:, NSA_W:], flag=flag[:, NSA_W:])
    z, zs = dict(q=zq, kvf=kvf, kvb=kvb, kvh=kvh), dict(q=zqs, kv=kvs)
    for name, act, dt, kw in (("gn", "none", F32, dict(tn=LANES, n=LANES)), ("gate_nsa", "silu", BF16, {}),
                              ("uv", "gelu", F32, {}), ("gate_mlp", "silu", BF16, {}), ("xl", "none", F32, {}),
                              ("gate_lru", "silu", BF16, {}), ("gm", "sigmoid", BF16, {})):
        z[name], zs[name] = p(name, act, dt, **kw)
    return z, zs


ROW_TILE = 1024


def prompt_layer(x2d, z, lw, l, b, t, w_buf):
    tm = ROW_TILE
    kcmp, vcmp = compress_prompt(z["kvf"], b, t, lw["cmp_k"], lw["cmp_v"], lw["k_norm_g"][0:1])
    o_nsa = attn_prompt(z["q"], z["kvb"], kcmp, vcmp, z["gn"], z["gate_nsa"], b, t)
    o_mlp = mlp_prompt(z["uv"], z["gate_mlp"], lw["mlp_ln_g"], lw["mlp_ln_b"], lw["w_s"], lw["b_s"], 512)
    o_lru, h_last = lru_prompt(z["xl"], z["gate_lru"], lw["conv_w"], lw["conv_b"], lw["lru_wa"], lw["lru_wx"],
                               lw["lru_ba"], lw["lru_bx"], lw["lru_lambda"], b, t)
    m = merge([o_nsa, o_mlp, o_lru], lw["w_br"], l, z["gm"], tm, SEG)
    y = out_proj(m, lw["w_out"], l, x2d, tm, SEG)
    kv = lambda i: z["kvh"][i].reshape(b, t, N_KV_HEADS, HEAD_DIM)
    xl = z["xl"].reshape(b, t, -1)
    assert t >= w_buf and t >= CONV_W - 1
    state = dict(cmp_k=kv(0), cmp_v=kv(1), sel_k=kv(2), sel_v=kv(3),
                 win_k=kv(4)[:, t - w_buf:], win_v=kv(5)[:, t - w_buf:],
                 lru_h=h_last.reshape(b, -1), lru_conv=xl[:, t - (CONV_W - 1):])
    return y, state


PAGES_PER_STEP = 32


def _cmp_paged_kernel(pt_ref, *refs, pg):
    k_refs, v_refs = refs[:pg], refs[pg:2 * pg]
    wk_ref, wv_ref, ko_ref, vo_ref, r_scr = refs[2 * pg:]
    rows = k_refs[0].shape[0]
    sub = N_KV_HEADS * CMP_STRIDE
    nsp = rows // sub
    for page_refs, w_ref, o_ref in ((k_refs, wk_ref, ko_ref), (v_refs, wv_ref, vo_ref)):
        blocks = [jnp.concatenate([r[pl.ds(sub * i + 8 * m, 8), :] for m in range(sub // 8)], axis=1)
                  for r in page_refs for i in range(nsp)]
        res = _dot(jnp.concatenate(blocks, axis=0).astype(BF16), w_ref[...])
        n = res.shape[0]
        both = res[:, :2 * HEAD_DIM] + pltpu.roll(res[:, 2 * HEAD_DIM:], n - N_KV_HEADS, 0)
        r_scr[0] = both[:, :HEAD_DIM]
        r_scr[1] = both[:, HEAD_DIM:]
        for g in range(N_KV_HEADS):
            o_ref[g] = jnp.concatenate([r_scr[0, pl.ds(g, n // 8, stride=8), :],
                                        r_scr[1, pl.ds(g, n // 8, stride=8), :]], axis=1)


def _cmp_paged_weights(wcat):
    return wcat.reshape(CMP_STRIDE // 2, 2, HEAD_DIM, 2 * HEAD_DIM).transpose(0, 2, 1, 3).reshape(
        CMP_STRIDE // 2 * HEAD_DIM, 4 * HEAD_DIM)


def compress_paged(pool_k, pool_v, page_table, wkc, wvc, page0):
    b, n_pages = page_table.shape
    rows = pool_k.shape[1]
    pg = PAGES_PER_STEP
    assert n_pages % pg == 0 and N_KV_HEADS * 2 == 8
    nsp = rows // N_KV_HEADS // CMP_STRIDE
    wkc, wvc = _cmp_paged_weights(wkc), _cmp_paged_weights(wvc)
    page_spec = lambda p: pl.BlockSpec((None, rows, HEAD_DIM), lambda i, c, pt: (page0 + pt[i, c * pg + p], 0, 0))
    full = lambda a: pl.BlockSpec(a.shape, lambda i, c, pt: (0,) * a.ndim)
    out = jax.ShapeDtypeStruct((b, N_KV_HEADS, n_pages * nsp, 2 * HEAD_DIM), F32)
    ospec = pl.BlockSpec((None, N_KV_HEADS, pg * nsp, 2 * HEAD_DIM), lambda i, c, pt: (i, 0, c, 0))
    return pl.pallas_call(
        functools.partial(_cmp_paged_kernel, pg=pg),
        grid_spec=pltpu.PrefetchScalarGridSpec(
            num_scalar_prefetch=1,
            grid=(b, n_pages // pg),
            in_specs=[page_spec(p) for p in range(pg)] * 2 + [full(wkc), full(wvc)],
            out_specs=[ospec, ospec],
            scratch_shapes=[pltpu.VMEM((2, pg * nsp * 8, HEAD_DIM), F32)]),
        out_shape=[out, out],
        compiler_params=_cparams(("parallel", "arbitrary")),
        name="cmp_paged",
    )(page_table, *([pool_k] * pg), *([pool_v] * pg), wkc, wvc)


def _cmp_sample_kernel(tbk_ref, tbv_ref, q_ref, pek_ref, pev_ref, wkf_ref, wvf_ref, gk_ref, msel_ref,
                       o_ref, idx_ref, *, tpos, nc, ns):
    ck = _pe_const(pek_ref, wkf_ref)
    cv = _pe_const(pev_ref, wvf_ref)
    q = q_ref[...].astype(BF16)
    nh = q.shape[0]
    nsub = tbk_ref.shape[1]
    hrow = lax.broadcasted_iota(jnp.int32, (nh, 1), 0)
    blk = lax.broadcasted_iota(jnp.int32, (1, nsub), 1)
    cmask = (blk * CMP_STRIDE + (CMP_BLK - 1) <= tpos) & (blk < nc)
    row8 = lax.broadcasted_iota(jnp.int32, (8, nsub), 0)
    o = jnp.zeros((nh, HEAD_DIM), F32)
    psum = jnp.zeros((8, nsub), F32)
    for g in range(N_KV_HEADS):
        kc = _compress_combine(tbk_ref[g], ck, nc)
        ms = jnp.mean(kc * kc, axis=-1, keepdims=True)
        kc = (kc * lax.rsqrt(ms + EPS)) * gk_ref[...]
        vc = _compress_combine(tbv_ref[g], cv, nc)
        p = _softmax_rows(_dot_nt(q, kc.astype(BF16)) * SCALE, cmask & (hrow // Q_PER_KV == g))
        o = o + _dot(p.astype(BF16), vc.astype(BF16))
        psum = jnp.where(row8 == g, jnp.sum(p, axis=0, keepdims=True), psum)
    o_ref[...] = o
    imp = jnp.dot(psum, msel_ref[...], precision=lax.Precision.HIGHEST, preferred_element_type=F32)
    lane = lax.broadcasted_iota(jnp.int32, imp.shape, 1)
    cur = tpos // SEL_BLK
    forced = (lane == 0) | (lane == cur) | (lane == cur - 1)
    impm = jnp.where(lane > cur, -BIG, jnp.where(forced, BIG, imp))
    impm = jnp.where(lane < ns, impm, -jnp.inf)
    lane_f = lane.astype(F32)
    out_lane = lax.broadcasted_iota(jnp.int32, idx_ref.shape, 1)
    idxs = jnp.zeros(idx_ref.shape, F32)
    for j in range(min(N_SEL, ns)):
        mx = jnp.max(impm, axis=-1, keepdims=True)
        am = jnp.min(jnp.where(impm == mx, lane_f, 1e9), axis=-1, keepdims=True)
        idxs = jnp.where(out_lane == j, am, idxs)
        impm = jnp.where(lane_f == am, -jnp.inf, impm)
    idx_ref[...] = idxs.astype(jnp.int32)


def cmp_sample(tbk, tbv, q3, ck, cv, gk, tpos):
    b, nkv, nsub, _ = tbk.shape
    nc = (tpos + 1) // CMP_STRIDE - CMP_BLK // CMP_STRIDE + 1
    ns = -(-(tpos + 1) // SEL_BLK)
    ns_pad = -(-ns // LANES) * LANES
    assert nc <= nsub
    msel = _sel_matrix(nsub, nc, ns_pad, ns)
    _, pek, wkf = ck
    _, pev, wvf = cv
    full = lambda a: pl.BlockSpec(a.shape, lambda i: (0,) * a.ndim)
    tb_spec = pl.BlockSpec((None, nkv, nsub, 2 * HEAD_DIM), lambda i: (i, 0, 0, 0))
    return pl.pallas_call(
        functools.partial(_cmp_sample_kernel, tpos=tpos, nc=nc, ns=ns),
        grid=(b,),
        in_specs=[tb_spec, tb_spec, pl.BlockSpec((None, N_HEADS, HEAD_DIM), lambda i: (i, 0, 0)),
                  full(pek), full(pev), full(wkf), full(wvf), full(gk), full(msel)],
        out_specs=[pl.BlockSpec((None, N_HEADS, HEAD_DIM), lambda i: (i, 0, 0)),
                   pl.BlockSpec((None, 8, LANES), lambda i: (i, 0, 0))],
        out_shape=[jax.ShapeDtypeStruct((b, N_HEADS, HEAD_DIM), F32),
                   jax.ShapeDtypeStruct((b, 8, LANES), jnp.int32)],
        compiler_params=_cparams(("parallel",)),
        name="cmp_sample",
    )(tbk, tbv, q3, pek, pev, wkf, wvf, gk, msel)


def _attend_with_new(q, k_all, v_all, mask, k_new, v_new, new_ok):
    s = jnp.where(mask, _dot_nt(q.astype(BF16), k_all) * SCALE, NEG)
    s_new = jnp.where(new_ok, jnp.sum(q * k_new, axis=-1, keepdims=True) * SCALE, NEG)
    m = jnp.maximum(jnp.max(s, axis=-1, keepdims=True), s_new)
    e = jnp.where(mask, jnp.exp(s - m), 0.0)
    e_new = jnp.where(new_ok, jnp.exp(s_new - m), 0.0)
    d = jnp.sum(e, axis=-1, keepdims=True) + e_new
    o = _dot(e.astype(BF16), v_all) + e_new * v_new
    return o / jnp.where(d > 0.0, d, 1.0)


def _attn_sample_kernel(pt_ref, idx_ref, *refs, nblk, tpos, past_len, w_buf):
    k_refs, v_refs = refs[:nblk], refs[nblk:2 * nblk]
    (q_ref, ocmp_ref, kw_ref, vw_ref, ksn_ref, vsn_ref, kwn_ref, vwn_ref, gn_ref, gate_ref,
     o_ref) = refs[2 * nblk:]
    b = pl.program_id(0)
    g = pl.program_id(1)
    q = q_ref[...]
    rb = k_refs[0].shape[0]
    r = lax.broadcasted_iota(jnp.int32, (1, rb), 1)
    tok, hd = r // N_KV_HEADS, r % N_KV_HEADS
    masks = []
    new_sel = False
    for j in range(nblk):
        s = idx_ref[b, g * nblk + j]
        kpos = s * SEL_BLK + tok
        masks.append((hd == g) & (kpos <= tpos) & (kpos < past_len))
        new_sel = new_sel | (s == past_len // SEL_BLK)
    k_all = jnp.concatenate([kr[...].astype(BF16) for kr in k_refs], axis=0)
    v_all = jnp.concatenate([vr[...].astype(BF16) for vr in v_refs], axis=0)
    o_sel = _attend_with_new(q, k_all, v_all, jnp.concatenate(masks, axis=1),
                             ksn_ref[...], vsn_ref[...], new_sel & (past_len <= tpos))
    rw = lax.broadcasted_iota(jnp.int32, (1, kw_ref.shape[0]), 1)
    diff = tpos - (past_len - w_buf + rw // N_KV_HEADS)
    wmask = (rw % N_KV_HEADS == g) & (diff >= 0) & (diff <= WINDOW)
    o_win = _attend_with_new(q, kw_ref[...].astype(BF16), vw_ref[...].astype(BF16), wmask,
                             kwn_ref[...], vwn_ref[...], tpos - past_len <= WINDOW)
    gs = _sigmoid(gn_ref[...])
    o = gs[:, 0:1] * ocmp_ref[...] + gs[:, 1:2] * o_sel + gs[:, 2:3] * o_win
    o_ref[...] = o * gate_ref[...]


def attn_sample(pool_k, pool_v, page_table, idx, q4, ocmp4, win_k, win_v, ks_new, vs_new, kw_new, vw_new,
                gn4, gate4, past_len, w_buf, page_size, page0, seq0):
    b, n_pages = page_table.shape
    nblk = idx.shape[1] // N_KV_HEADS
    rb = pool_k.shape[1]
    halves = page_size // SEL_BLK
    tpos = past_len

    def blk_map(j):
        def f(i, g, pt, ix):
            s = ix[i, g * nblk + j]
            page = pt[i, jnp.minimum(s // halves, n_pages - 1)]
            return ((page0 + page) * halves + s % halves, 0, 0)
        return f

    blk_spec = lambda j: pl.BlockSpec((None, rb, HEAD_DIM), blk_map(j))
    per_bg = lambda a: pl.BlockSpec((None, None) + a.shape[2:], lambda i, g, pt, ix: (i, g, 0, 0))
    per_b = lambda a: pl.BlockSpec((None,) + a.shape[1:], lambda i, g, pt, ix: (seq0 + i, 0, 0))
    small = (q4, ocmp4)
    news = (ks_new, vs_new, kw_new, vw_new, gn4, gate4)
    return pl.pallas_call(
        functools.partial(_attn_sample_kernel, nblk=nblk, tpos=tpos, past_len=past_len, w_buf=w_buf),
        grid_spec=pltpu.PrefetchScalarGridSpec(
            num_scalar_prefetch=2,
            grid=(b, N_KV_HEADS),
            in_specs=[blk_spec(j) for j in range(nblk)] * 2 + [per_bg(a) for a in small]
            + [per_b(win_k), per_b(win_v)] + [per_bg(a) for a in news],
            out_specs=per_bg(q4)),
        out_shape=jax.ShapeDtypeStruct(q4.shape, F32),
        compiler_params=_cparams(("parallel", "arbitrary")),
        name="attn_sample",
    )(page_table, idx, *([pool_k] * nblk), *([pool_v] * nblk), q4, ocmp4, win_k, win_v, *news)


def _point_sample_kernel(u_ref, v_ref, gmlp_ref, lg_ref, lb_ref, ws0_ref, bs0_ref,
                         x_ref, glru_ref, buf_ref, h0_ref, cw_ref, cb_ref, wa_ref, wx_ref, ba_ref, bx_ref,
                         lam_ref, omlp_ref, vrow_ref, olru_ref, hnew_ref):
    vn = _layer_norm(v_ref[...], lg_ref[...], lb_ref[...])
    vrow_ref[...] = vn
    mixed = ws0_ref[...] * vn + bs0_ref[...]
    omlp_ref[...] = (u_ref[...] * mixed * gmlp_ref[...].astype(F32)).astype(omlp_ref.dtype)
    x = x_ref[...]
    xc = cb_ref[...] + x * cw_ref[CONV_W - 1:CONV_W, :]
    for j in range(CONV_W - 1):
        xc = xc + buf_ref[j] * cw_ref[j:j + 1, :]
    a, u = _lru_gates(xc, wa_ref, wx_ref, ba_ref[...], bx_ref[...], lam_ref[...])
    h = a * h0_ref[...] + u
    hnew_ref[...] = h
    olru_ref[...] = (h * glru_ref[...].astype(F32)).astype(olru_ref.dtype)


def point_sample(uv, gate_mlp, xl, gate_lru, buf_t, h0, lw):
    b, w2 = uv.shape
    w = w2 // 2
    gw = w // MLP_GROUPS
    vec = lambda a: a.reshape(1, -1)
    ws0 = vec(jnp.repeat(lw["w_s"][:, 0, 0], gw))
    bs0 = vec(jnp.repeat(lw["b_s"][:, 0], gw))
    args = (uv[:, :w], uv[:, w:], gate_mlp, vec(lw["mlp_ln_g"]), vec(lw["mlp_ln_b"]), ws0, bs0,
            xl, gate_lru, buf_t, h0, lw["conv_w"], vec(lw["conv_b"]), lw["lru_wa"].astype(BF16),
            lw["lru_wx"].astype(BF16), vec(lw["lru_ba"]), vec(lw["lru_bx"]), vec(lw["lru_lambda"]))
    lw_ = xl.shape[1]
    return pl.pallas_call(
        _point_sample_kernel,
        out_shape=[jax.ShapeDtypeStruct((b, w), BF16), jax.ShapeDtypeStruct((b, w), F32),
                   jax.ShapeDtypeStruct((b, lw_), BF16), jax.ShapeDtypeStruct((b, lw_), F32)],
        compiler_params=pltpu.CompilerParams(vmem_limit_bytes=VMEM_LIMIT),
        name="point_sample",
    )(*args)


def sample_layer(x2d, z, lw, l, caches, page_table, past_len, w_buf):
    b = x2d.shape[0]
    seg = lambda i: z["kv"][:, i * KV_W:(i + 1) * KV_W]
    kvh = lambda a: a.reshape(b, N_KV_HEADS, 1, HEAD_DIM)
    n_pool, page_size = caches["cmp_k"].shape[1:3]
    page0 = l * n_pool
    pool3 = lambda a: a.reshape(-1, page_size * N_KV_HEADS, HEAD_DIM)
    halves = lambda a: a.reshape(-1, SEL_BLK * N_KV_HEADS, HEAD_DIM)
    tbk, tbv = compress_paged(pool3(caches["cmp_k"]), pool3(caches["cmp_v"]), page_table,
                              lw["cmp_k"][0], lw["cmp_v"][0], page0)
    q3 = z["q"].reshape(b, N_HEADS, HEAD_DIM)
    o_cmp, idx = cmp_sample(tbk, tbv, q3, lw["cmp_k"], lw["cmp_v"], lw["k_norm_g"][0:1], past_len)
    n_sel = min(N_SEL, -(-(past_len + 1) // SEL_BLK))
    idx = idx[:, :N_KV_HEADS, :n_sel].reshape(b, N_KV_HEADS * n_sel)
    four = lambda a: a.reshape(b, N_KV_HEADS, Q_PER_KV, -1)
    gn4 = z["gn"][:, :N_BRANCH * N_HEADS].reshape(b, N_BRANCH, N_KV_HEADS, Q_PER_KV).transpose(0, 2, 3, 1)
    win3 = lambda a: a.reshape(-1, w_buf * N_KV_HEADS, HEAD_DIM)
    o_nsa = attn_sample(halves(caches["sel_k"]), halves(caches["sel_v"]), page_table, idx,
                        four(q3), four(o_cmp), win3(caches["win_k"]), win3(caches["win_v"]),
                        kvh(seg(2)), kvh(seg(3)), kvh(seg(4)), kvh(seg(5)),
                        gn4, four(z["gate_nsa"].astype(F32)), past_len, w_buf, page_size, page0, l * b)
    o_mlp, v_rows, o_lru, h_new = point_sample(
        z["uv"], z["gate_mlp"], z["xl"], z["gate_lru"], caches["lru_conv"].transpose(1, 0, 2),
        caches["lru_h"].astype(F32), lw)
    m = merge([o_nsa.reshape(b, NSA_W).astype(BF16), o_mlp, o_lru], lw["w_br"], l, z["gm"], b, SEG)
    y = out_proj(m, lw["w_out"], l, x2d, b, SEG)
    tok = lambda a: a.reshape(b, 1, N_KV_HEADS, HEAD_DIM)
    state = dict(cmp_k=tok(seg(0)), cmp_v=tok(seg(1)), sel_k=tok(seg(2)), sel_v=tok(seg(3)),
                 win_k=jnp.concatenate([caches["win_k"][l], tok(seg(4))], axis=1)[:, -w_buf:],
                 win_v=jnp.concatenate([caches["win_v"][l], tok(seg(5))], axis=1)[:, -w_buf:],
                 lru_h=h_new, lru_conv=jnp.concatenate([caches["lru_conv"], z["xl"][:, None]], axis=1)[:, 1:],
                 mlp_v=v_rows[:, None])
    return y, state


def _layer_weights(l, p):
    d_model = p["w_in"].shape[1]
    mlp_w = p["mlp_ln_g"].shape[1]
    lru_w = p["lru_lambda"].shape[1]
    names = ("norm_g", "q_norm_g", "k_norm_g", "mlp_ln_g", "mlp_ln_b", "w_s", "b_s", "conv_w", "conv_b",
             "lru_wa", "lru_ba", "lru_wx", "lru_bx", "lru_lambda")
    lw = {n: p[n][l] for n in names}
    lw["w_pack"] = _pack_w_in(p["w_in"], l, d_model, mlp_w, lru_w)
    lw["gain"], lw["flag"] = _norm_vectors(p["q_norm_g"][l], p["k_norm_g"][l])
    lw["cmp_k"] = _cmp_weights(p["w_cmp_k"][l], p["cmp_pe_k"][l])
    lw["cmp_v"] = _cmp_weights(p["w_cmp_v"][l], p["cmp_pe_v"][l])
    lw["w_br"] = [p[n] for n in ("w_br_nsa_bf", "w_br_mlp_bf", "w_br_lru_bf")]
    lw["w_out"] = p["w_out_bf"]
    return lw


def kernel(x_prompt, x_sample, cache_cmp_k, cache_cmp_v, cache_sel_k, cache_sel_v, state_win_k, state_win_v,
           state_lru_h, state_lru_conv, page_table, norm_g, w_in, q_norm_g, k_norm_g, cmp_pe_k, cmp_pe_v,
           w_cmp_k, w_cmp_v, mlp_ln_g, mlp_ln_b, w_s, b_s, conv_w, conv_b, lru_wa, lru_ba, lru_wx, lru_bx,
           lru_lambda, w_br_nsa, w_br_mlp, w_br_lru, w_out):
    params = dict(norm_g=norm_g, w_in=w_in, q_norm_g=q_norm_g, k_norm_g=k_norm_g, cmp_pe_k=cmp_pe_k,
                  cmp_pe_v=cmp_pe_v, w_cmp_k=w_cmp_k, w_cmp_v=w_cmp_v, mlp_ln_g=mlp_ln_g, mlp_ln_b=mlp_ln_b,
                  w_s=w_s, b_s=b_s, conv_w=conv_w, conv_b=conv_b, lru_wa=lru_wa, lru_ba=lru_ba, lru_wx=lru_wx,
                  lru_bx=lru_bx, lru_lambda=lru_lambda, w_br_nsa=w_br_nsa, w_br_mlp=w_br_mlp,
                  w_br_lru=w_br_lru, w_out=w_out)
    depth = w_in.shape[0]
    b, t, d = x_prompt.shape
    bs, ts, _ = x_sample.shape
    assert ts == 1
    w_buf = state_win_k.shape[2]
    past_len = page_table.shape[1] * cache_cmp_k.shape[2]
    yp = x_prompt.reshape(b * t, d)
    ys = x_sample.reshape(bs * ts, d)
    p_st, s_st = [], []
    for n in ("w_br_nsa", "w_br_mlp", "w_br_lru", "w_out"):
        params[n + "_bf"] = params[n].astype(BF16)
    for l in range(depth):
        lw = _layer_weights(l, params)
        z, zs = _in_proj(yp, ys, lw, ROW_TILE)
        yp, sp = prompt_layer(yp, z, lw, l, b, t, w_buf)
        caches = dict(cmp_k=cache_cmp_k, cmp_v=cache_cmp_v, sel_k=cache_sel_k, sel_v=cache_sel_v,
                      win_k=state_win_k, win_v=state_win_v, lru_h=state_lru_h[l],
                      lru_conv=state_lru_conv[l])
        ys, ss = sample_layer(ys, zs, lw, l, caches, page_table, past_len, w_buf)
        p_st.append(sp)
        s_st.append(ss)
    stk = lambda sts, name: jnp.stack([st[name] for st in sts])
    names = ("cmp_k", "cmp_v", "sel_k", "sel_v", "win_k", "win_v", "lru_h", "lru_conv")
    return ((yp.reshape(b, t, d), ys.reshape(bs, ts, d))
            + tuple(stk(p_st, n) for n in names)
            + tuple(stk(s_st, n) for n in names + ("mlp_v",)))
```

```python
import functools

import numpy as np
import jax
import jax.numpy as jnp
from jax import lax
from jax.experimental import pallas as pl
from jax.experimental.pallas import tpu as pltpu

N_HEADS = 16
HEAD_DIM = 128
N_KV_HEADS = 4
Q_PER_KV = N_HEADS // N_KV_HEADS
NSA_W = N_HEADS * HEAD_DIM
KV_W = N_KV_HEADS * HEAD_DIM
CMP_BLK = 32
CMP_STRIDE = 16
SEL_BLK = 64
N_SEL = 16
WINDOW = 512
CHUNK = 128
MLP_GROUPS = 8
LRU_HEADS = 8
CONV_W = 4
LRU_C = 8.0
N_BRANCH = 3
EPS = 1e-6
BIG = 1e9
NEG = -1e30
SCALE = HEAD_DIM ** -0.5

LANES = 128
VMEM_LIMIT = 56 * 1024 * 1024

BF16 = jnp.bfloat16
F32 = jnp.float32


def _cparams(sem):
    return pltpu.CompilerParams(dimension_semantics=sem, vmem_limit_bytes=VMEM_LIMIT)


def _gelu(x):
    return 0.5 * x * (1.0 + jnp.tanh(0.7978845608028654 * (x + 0.044715 * (x * x * x))))


def _sigmoid(x):
    return 1.0 / (1.0 + jnp.exp(-x))


def _silu(x):
    return x * _sigmoid(x)


def _dot(a, b):
    return jnp.dot(a, b, preferred_element_type=F32)


def _dot_nt(a, b):
    return lax.dot_general(a, b, (((1,), (1,)), ((), ())), preferred_element_type=F32)


def _rms_kernel(x_ref, g_ref, o_ref):
    x = x_ref[...]
    ms = jnp.mean(x * x, axis=-1, keepdims=True)
    o_ref[...] = ((x * lax.rsqrt(ms + EPS)) * g_ref[...]).astype(o_ref.dtype)


def rms_rows(x, g, tm):
    m, d = x.shape
    return pl.pallas_call(
        _rms_kernel,
        grid=(m // tm,),
        in_specs=[pl.BlockSpec((tm, d), lambda i: (i, 0)),
                  pl.BlockSpec((1, d), lambda i: (0, 0))],
        out_specs=pl.BlockSpec((tm, d), lambda i: (i, 0)),
        out_shape=jax.ShapeDtypeStruct((m, d), BF16),
        compiler_params=_cparams(("parallel",)),
        name="rms_rows",
    )(x, g.reshape(1, d))


def _row_tile(shape):
    return pl.BlockSpec(shape, lambda i, j: (i, 0))


def _proj_act(acc, act, gain_ref, flag_ref):
    if act == "silu":
        return _silu(acc)
    if act == "gelu":
        return _gelu(acc)
    if act == "sigmoid":
        return _sigmoid(acc)
    if act == "norm":
        outs = []
        for c in range(acc.shape[1] // LANES):
            sl = slice(c * LANES, (c + 1) * LANES)
            z = acc[:, sl]
            ms = jnp.mean(z * z, axis=-1, keepdims=True)
            zn = (z * lax.rsqrt(ms + EPS)) * gain_ref[:, sl]
            outs.append(jnp.where(flag_ref[:, sl] > 0.5, zn, z))
        return jnp.concatenate(outs, axis=1)
    return acc


def _proj_kernel(a_ref, a2_ref, w_ref, *rest, act, outs, ranges):
    if act == "mixed":
        o_ref, o2_ref = rest
        j = pl.program_id(1)
        w = w_ref[...]

        def store(ref, x_ref):
            acc = _dot(x_ref[...], w)
            for a, lo, hi in ranges:
                @pl.when((j >= lo) & (j < hi))
                def _(a=a):
                    ref[...] = _proj_act(acc, a, None, None).astype(ref.dtype)

        store(o_ref, a_ref)
        pl.when(pl.program_id(0) == 0)(lambda: store(o2_ref, a2_ref))
        return
    if act == "norm":
        gain_ref, flag_ref = rest[:2]
        rest = rest[2:]
    else:
        gain_ref = flag_ref = None
    w = w_ref[...]
    z = _proj_act(_dot(a_ref[...], w), act, gain_ref, flag_ref)
    for kind, o_ref in zip(outs, rest):
        if kind == "heads":
            nh = z.shape[1] // HEAD_DIM
            for g in range(nh):
                o_ref[pl.ds(g, z.shape[0], stride=nh), :] = z[:, g * HEAD_DIM:(g + 1) * HEAD_DIM]
        else:
            o_ref[...] = z.astype(o_ref.dtype)
    o2_ref = rest[-1]

    @pl.when(pl.program_id(0) == 0)
    def _():
        o2_ref[...] = _proj_act(_dot(a2_ref[...], w), act, gain_ref, flag_ref).astype(o2_ref.dtype)


def proj(a, a2, w, row0, nrows, act, outs, tm, tn, gain=None, flag=None, ranges=None):
    m, k = a.shape
    m2 = a2.shape[0]
    assert row0 % tn == 0 and nrows % tn == 0 and m % tm == 0
    jb = row0 // tn
    nj = nrows // tn
    in_specs = [_row_tile((tm, k)),
                pl.BlockSpec((m2, k), lambda i, j: (0, 0)),
                pl.BlockSpec((k, tn), lambda i, j: (0, jb + j))]
    args = [a, a2, w]
    if act == "norm":
        in_specs += [pl.BlockSpec((1, tn), lambda i, j: (0, j))] * 2
        args += [gain, flag]
    nh = tn // HEAD_DIM
    out_specs, out_shape = [], []
    for kind in outs:
        if kind == "heads":
            out_specs.append(pl.BlockSpec((None, tm * nh, HEAD_DIM), lambda i, j: (j, i, 0)))
            out_shape.append(jax.ShapeDtypeStruct((nj, m * nh, HEAD_DIM), F32))
        else:
            out_specs.append(pl.BlockSpec((tm, tn), lambda i, j: (i, j)))
            out_shape.append(jax.ShapeDtypeStruct((m, nrows), kind))
    out_specs.append(pl.BlockSpec((m2, tn), lambda i, j: (0, jnp.where(i == 0, j, nj - 1))))
    out_shape.append(jax.ShapeDtypeStruct((m2, nrows), F32 if act == "norm" else outs[0]))
    return pl.pallas_call(
        functools.partial(_proj_kernel, act=act, outs=tuple(outs), ranges=ranges),
        grid=(m // tm, nj),
        in_specs=in_specs,
        out_specs=out_specs,
        out_shape=out_shape,
        compiler_params=_cparams(("arbitrary", "arbitrary")),
        name="proj_" + act,
    )(*args)


def _merge_kernel(a1, a2, a3, w1, w2, w3, g1, g2, g3, o_ref):
    m = g1[...].astype(F32) * _dot(a1[...], w1[...])
    m = m + g2[...].astype(F32) * _dot(a2[...], w2[...])
    m = m + g3[...].astype(F32) * _dot(a3[...], w3[...])
    o_ref[...] = m.astype(o_ref.dtype)


def merge(a_list, w_list, l, gm, tm, tn):
    m = a_list[0].shape[0]
    n = w_list[0].shape[2]
    nb = n // tn
    in_specs = [_row_tile((tm, a.shape[1])) for a in a_list]
    in_specs += [pl.BlockSpec((None, w.shape[1], tn), lambda i, j: (l, 0, j)) for w in w_list]
    in_specs += [pl.BlockSpec((tm, tn), functools.partial(lambda i, j, b: (i, b * nb + j), b=b))
                 for b in range(N_BRANCH)]
    return pl.pallas_call(
        _merge_kernel,
        grid=(m // tm, nb),
        in_specs=in_specs,
        out_specs=pl.BlockSpec((tm, tn), lambda i, j: (i, j)),
        out_shape=jax.ShapeDtypeStruct((m, n), BF16),
        compiler_params=_cparams(("parallel", "arbitrary")),
        name="merge",
    )(*a_list, *w_list, gm, gm, gm)


def _resid_kernel(a_ref, w_ref, x_ref, o_ref):
    o_ref[...] = x_ref[...] + _dot(a_ref[...], w_ref[...])


def out_proj(a, w, l, x, tm, tn):
    m, k = a.shape
    n = w.shape[2]
    return pl.pallas_call(
        _resid_kernel,
        grid=(m // tm, n // tn),
        in_specs=[_row_tile((tm, k)),
                  pl.BlockSpec((None, k, tn), lambda i, j: (l, 0, j)),
                  pl.BlockSpec((tm, tn), lambda i, j: (i, j))],
        out_specs=pl.BlockSpec((tm, tn), lambda i, j: (i, j)),
        out_shape=jax.ShapeDtypeStruct((m, n), F32),
        compiler_params=_cparams(("parallel", "arbitrary")),
        name="out_proj",
    )(a, w, x)


def _softmax_rows(s, mask):
    s = jnp.where(mask, s, NEG)
    m = jnp.max(s, axis=-1, keepdims=True)
    e = jnp.where(mask, jnp.exp(s - m), 0.0)
    d = jnp.sum(e, axis=-1, keepdims=True)
    return e / jnp.where(d > 0.0, d, 1.0)


def _compress_combine(acc, const, nvalid):
    nsub = acc.shape[0]
    top = acc[:, :LANES]
    bot = pltpu.roll(acc[:, LANES:], nsub - 1, 0)
    row = lax.broadcasted_iota(jnp.int32, (nsub, LANES), 0)
    return jnp.where(row < nvalid, top + bot + const, 0.0)


def _pe_const(pe_ref, w_ref):
    return jnp.dot(pe_ref[...], w_ref[...], precision=lax.Precision.HIGHEST,
                   preferred_element_type=F32)[0:1, :]


def _cmp_prompt_kernel(k_ref, v_ref, wk_ref, wv_ref, pek_ref, pev_ref, wkf_ref, wvf_ref, g_ref,
                       ko_ref, vo_ref):
    nsub = ko_ref.shape[0]
    acck = jnp.zeros((nsub, 2 * HEAD_DIM), F32)
    accv = jnp.zeros((nsub, 2 * HEAD_DIM), F32)
    for l in range(CMP_STRIDE):
        xk = k_ref[pl.ds(l, nsub, stride=CMP_STRIDE), :].astype(BF16)
        xv = v_ref[pl.ds(l, nsub, stride=CMP_STRIDE), :].astype(BF16)
        acck = acck + _dot(xk, wk_ref[l])
        accv = accv + _dot(xv, wv_ref[l])
    kc = _compress_combine(acck, _pe_const(pek_ref, wkf_ref), nsub - 1)
    ms = jnp.mean(kc * kc, axis=-1, keepdims=True)
    ko_ref[...] = (kc * lax.rsqrt(ms + EPS)) * g_ref[...]
    vo_ref[...] = _compress_combine(accv, _pe_const(pev_ref, wvf_ref), nsub - 1)


def _cmp_weights(w_cmp, pe):
    wcat = jnp.concatenate([w_cmp[:CMP_STRIDE], w_cmp[CMP_STRIDE:]], axis=2).astype(BF16)
    pe8 = jnp.broadcast_to(pe.reshape(1, CMP_BLK * HEAD_DIM), (8, CMP_BLK * HEAD_DIM))
    return wcat, pe8, w_cmp.reshape(CMP_BLK * HEAD_DIM, HEAD_DIM)


def compress_prompt(zf, b, t, wk, wv, gk):
    nsub = t // CMP_STRIDE
    wkc, pek, wkf = wk
    wvc, pev, wvf = wv
    full = lambda a: pl.BlockSpec(a.shape, lambda i, g: (0,) * a.ndim)
    head = lambda off: pl.BlockSpec((t, HEAD_DIM), lambda i, g: (i, off // HEAD_DIM + g))
    out = jax.ShapeDtypeStruct((b * N_KV_HEADS, nsub, HEAD_DIM), F32)
    return pl.pallas_call(
        _cmp_prompt_kernel,
        grid=(b, N_KV_HEADS),
        in_specs=[head(0), head(KV_W),
                  full(wkc), full(wvc), full(pek), full(pev), full(wkf), full(wvf), full(gk)],
        out_specs=[pl.BlockSpec((None, nsub, HEAD_DIM), lambda i, g: (i * N_KV_HEADS + g, 0, 0))] * 2,
        out_shape=[out, out],
        compiler_params=_cparams(("parallel", "parallel")),
        name="cmp_prompt",
    )(zf, zf, wkc, wvc, pek, pev, wkf, wvf, gk)


def _sel_matrix(nc_pad, nc, ns_pad, ns):
    c0 = np.arange(nc_pad) * CMP_STRIDE
    s0 = np.arange(ns_pad) * SEL_BLK
    m = (c0[:, None] < s0[None, :] + SEL_BLK) & (c0[:, None] + CMP_BLK > s0[None, :])
    m &= (np.arange(nc_pad)[:, None] < nc) & (np.arange(ns_pad)[None, :] < ns)
    return jnp.asarray(m.astype(np.float32))


EXP_C = SCALE * 1.4426950408889634


def _select_mask_t(imp_t, tpos_row, ns):
    blk = lax.broadcasted_iota(jnp.int32, imp_t.shape, 0)
    cur = tpos_row // SEL_BLK
    forced = (blk == 0) | (blk == cur) | (blk == cur - 1)
    impm = jnp.where(blk > cur, -BIG, jnp.where(forced, BIG, imp_t))
    rank = jnp.zeros(imp_t.shape, F32)
    for s2 in range(ns):
        row = impm[s2:s2 + 1, :]
        gt = jnp.where(row > impm, 1.0, 0.0)
        ge = jnp.where(row >= impm, 1.0, 0.0)
        rank = rank + jnp.where(blk > s2, ge, gt)
    return jnp.where(rank < float(min(N_SEL, ns)), 1.0, 0.0)


def _softmax_bias(s3, bias, valid):
    sm = s3 + bias[None]
    m = jnp.max(sm, axis=-1, keepdims=True)
    e = jnp.exp2((sm - m) * EXP_C)
    inv = 1.0 / jnp.sum(e, axis=-1, keepdims=True)
    if valid is not None:
        inv = jnp.where(valid[None], inv, 0.0)
    return e * inv


def _attn_prompt_kernel(q_ref, kc_ref, vc_ref, ks_ref, vs_ref, kw_ref, vw_ref, gn_ref, gate_ref,
                        msel_ref, e_ref, o_ref, *, tq, kvc, ns):
    g = pl.program_id(1)
    t0 = pl.program_id(2) * tq
    nh = Q_PER_KV
    rows = nh * tq
    q4 = jnp.concatenate([q_ref[:, h * HEAD_DIM:(h + 1) * HEAD_DIM] for h in range(nh)], axis=0)
    tpos = t0 + lax.broadcasted_iota(jnp.int32, (tq, 1), 0)
    tpos_row = t0 + lax.broadcasted_iota(jnp.int32, (1, tq), 1)

    ncp = kc_ref.shape[0]
    endpos = lax.broadcasted_iota(jnp.int32, (1, ncp), 1) * CMP_STRIDE + (CMP_BLK - 1)
    bias_c = jnp.where(endpos <= tpos, 0.0, NEG)
    p3 = _softmax_bias(_dot_nt(q4, kc_ref[...].astype(BF16)).reshape(nh, tq, ncp), bias_c,
                       tpos >= CMP_BLK - 1)
    o_cmp = _dot(p3.reshape(rows, ncp).astype(BF16), vc_ref[...].astype(BF16))
    imp = jnp.dot(jnp.sum(p3, axis=0), msel_ref[...], precision=lax.Precision.HIGHEST,
                  preferred_element_type=F32)
    ns8 = -(-ns // 8) * 8
    sel_t = _select_mask_t(imp.T[:ns8], tpos_row, ns)
    sel = jnp.concatenate([sel_t, jnp.zeros((imp.shape[1] - ns8, tq), F32)], axis=0).T.astype(BF16)

    def body(c, carry):
        m, l, acc = carry
        k0 = pl.multiple_of(c * kvc, kvc)
        kpos = k0 + lax.broadcasted_iota(jnp.int32, (1, kvc), 1)
        bias = jnp.where(kpos <= tpos, (_dot(sel, e_ref[c]) - 1.0) * -NEG, NEG)
        sm = _dot_nt(q4, ks_ref[pl.ds(k0, kvc), :]).reshape(nh, tq, kvc) + bias[None]
        m_new = jnp.maximum(m, jnp.max(sm, axis=-1, keepdims=True))
        alpha = jnp.exp2((m - m_new) * EXP_C)
        e = jnp.exp2((sm - m_new) * EXP_C)
        l = alpha * l + jnp.sum(e, axis=-1, keepdims=True)
        pv = _dot(e.reshape(rows, kvc).astype(BF16), vs_ref[pl.ds(k0, kvc), :])
        return m_new, l, alpha * acc + pv.reshape(nh, tq, HEAD_DIM)

    nch = (t0 + tq + kvc - 1) // kvc
    m, l, acc = lax.fori_loop(0, nch, body, (jnp.full((nh, tq, 1), NEG, F32), jnp.zeros((nh, tq, 1), F32),
                                             jnp.zeros((nh, tq, HEAD_DIM), F32)))
    o_sel = (acc * jnp.where(l > 0.0, 1.0 / l, 0.0)).reshape(rows, HEAD_DIM)

    nwin = WINDOW + tq
    w0 = pl.multiple_of(jnp.maximum(t0 - WINDOW, 0), tq)
    diff = tpos - (w0 + lax.broadcasted_iota(jnp.int32, (1, nwin), 1))
    bias_w = jnp.where((diff >= 0) & (diff <= WINDOW), 0.0, NEG)
    pw = _softmax_bias(_dot_nt(q4, kw_ref[pl.ds(w0, nwin), :]).reshape(nh, tq, nwin), bias_w, None)
    o_win = _dot(pw.reshape(rows, nwin).astype(BF16), vw_ref[pl.ds(w0, nwin), :])

    gs = _sigmoid(gn_ref[...])
    lane = lax.broadcasted_iota(jnp.int32, gs.shape, 1)
    for h in range(nh):
        r = slice(h * tq, (h + 1) * tq)
        o = None
        for br, ob in enumerate((o_cmp, o_sel, o_win)):
            cidx = br * N_HEADS + g * nh + h
            gcol = jnp.sum(jnp.where(lane == cidx, gs, 0.0), axis=-1, keepdims=True)
            o = gcol * ob[r] if o is None else o + gcol * ob[r]
        hs = slice(h * HEAD_DIM, (h + 1) * HEAD_DIM)
        o_ref[:, hs] = (o * gate_ref[:, hs].astype(F32)).astype(o_ref.dtype)


def attn_prompt(zq, zkv, kcmp, vcmp, gn, gate, b, t):
    tq, kvc = 256, 512
    assert t // CMP_STRIDE == LANES and t % kvc == 0 and t >= WINDOW + tq
    nq = t // tq
    ns = -(-t // SEL_BLK)
    nc = t // CMP_STRIDE - CMP_BLK // CMP_STRIDE + 1
    msel = _sel_matrix(LANES, nc, LANES, ns)
    kk = np.arange(t)
    e3 = (kk[None, :] // SEL_BLK == np.arange(LANES)[:, None]).astype(np.float32)
    e3 = jnp.asarray(e3.reshape(LANES, t // kvc, kvc).transpose(1, 0, 2), BF16)
    hb = lambda off: (lambda bi, g, i: (bi, off // HEAD_DIM + g))
    kv_spec = lambda off: pl.BlockSpec((t, HEAD_DIM), hb(off))
    cm_spec = pl.BlockSpec((None, t // CMP_STRIDE, HEAD_DIM), lambda bi, g, i: (bi * N_KV_HEADS + g, 0, 0))
    row4 = pl.BlockSpec((tq, Q_PER_KV * HEAD_DIM), lambda bi, g, i: (bi * nq + i, g))
    return pl.pallas_call(
        functools.partial(_attn_prompt_kernel, tq=tq, kvc=kvc, ns=ns),
        grid=(b, N_KV_HEADS, nq),
        in_specs=[row4, cm_spec, cm_spec,
                  kv_spec(2 * KV_W), kv_spec(3 * KV_W), kv_spec(4 * KV_W), kv_spec(5 * KV_W),
                  pl.BlockSpec((tq, LANES), lambda bi, g, i: (bi * nq + i, 0)),
                  row4,
                  pl.BlockSpec(msel.shape, lambda bi, g, i: (0, 0)),
                  pl.BlockSpec(e3.shape, lambda bi, g, i: (0, 0, 0))],
        out_specs=row4,
        out_shape=jax.ShapeDtypeStruct((b * t, NSA_W), BF16),
        compiler_params=_cparams(("parallel", "parallel", "arbitrary")),
        name="attn_prompt",
    )(zq, kcmp, vcmp, zkv, zkv, zkv, zkv, gn, gate, msel, e3)


def _layer_norm(v, g, b):
    vc = v - jnp.mean(v, axis=-1, keepdims=True)
    var = jnp.mean(vc * vc, axis=-1, keepdims=True)
    return vc * lax.rsqrt(var + EPS) * g + b


def _mlp_prompt_kernel(u_ref, v_ref, gate_ref, lg_ref, lb_ref, ws_ref, bst_ref, o_ref):
    tm = u_ref.shape[0]
    gw = ws_ref.shape[1]
    vb = _layer_norm(v_ref[...], lg_ref[...], lb_ref[...]).astype(BF16)
    row = lax.broadcasted_iota(jnp.int32, (CHUNK, CHUNK), 0)
    col = lax.broadcasted_iota(jnp.int32, (CHUNK, CHUNK), 1)
    for gi in range(MLP_GROUPS):
        cs = slice(gi * gw, (gi + 1) * gw)
        wsg = jnp.where(row >= col, ws_ref[gi], 0.0).astype(BF16)
        bias = bst_ref[:, gi:gi + 1]
        for ch in range(tm // CHUNK):
            rs = slice(ch * CHUNK, (ch + 1) * CHUNK)
            mixed = _dot(wsg, vb[rs, cs]) + bias
            o_ref[rs, cs] = (u_ref[rs, cs] * mixed * gate_ref[rs, cs].astype(F32)).astype(o_ref.dtype)


def mlp_prompt(z, cu, cg, ln_g, ln_b, w_s, b_s, tm):
    m = z.shape[0]
    w = ln_g.shape[0]
    assert w // MLP_GROUPS == CHUNK == w_s.shape[1]
    full = lambda a: pl.BlockSpec(a.shape, lambda i: (0,) * a.ndim)
    bst = b_s.T
    return pl.pallas_call(
        _mlp_prompt_kernel,
        grid=(m // tm,),
        in_specs=[pl.BlockSpec((tm, w), lambda i: (i, cu)), pl.BlockSpec((tm, w), lambda i: (i, cu + 1)),
                  pl.BlockSpec((tm, w), lambda i: (i, cg)),
                  pl.BlockSpec((1, w), lambda i: (0, 0)), pl.BlockSpec((1, w), lambda i: (0, 0)),
                  full(w_s), full(bst)],
        out_specs=pl.BlockSpec((tm, w), lambda i: (i, 0)),
        out_shape=jax.ShapeDtypeStruct((m, w), BF16),
        compiler_params=_cparams(("parallel",)),
        name="mlp_prompt",
    )(z, z, z, ln_g.reshape(1, w), ln_b.reshape(1, w), w_s, bst)


def _softplus(x):
    return jnp.maximum(x, 0.0) + jnp.log1p(jnp.exp(-jnp.abs(x)))


def _lru_gates(xc, wa_ref, wx_ref, ba, bx, lam):
    nh = xc.shape[1] // HEAD_DIM
    rs, is_ = [], []
    for hh in range(nh):
        xh = xc[:, hh * HEAD_DIM:(hh + 1) * HEAD_DIM].astype(BF16)
        rs.append(_dot(xh, wa_ref[hh]))
        is_.append(_dot(xh, wx_ref[hh]))
    r = _sigmoid(jnp.concatenate(rs, axis=1) + ba)
    i = _sigmoid(jnp.concatenate(is_, axis=1) + bx)
    log_a = -LRU_C * r * _softplus(-lam)
    th = jnp.tanh(log_a)
    return jnp.exp(log_a), jnp.sqrt(-2.0 * th / (1.0 - th)) * (i * xc)


def _lru_prompt_kernel(x_ref, gate_ref, cw_ref, cb_ref, wa_ref, wx_ref, ba_ref, bx_ref, lam_ref,
                       o_ref, h_ref, a_scr, u_scr):
    t, wb = x_ref.shape
    x = x_ref[...]
    row = lax.broadcasted_iota(jnp.int32, (t, wb), 0)
    xc = cb_ref[...] + x * cw_ref[CONV_W - 1:CONV_W, :]
    for d in range(1, CONV_W):
        xs = jnp.where(row >= d, pltpu.roll(x, d, 0), 0.0)
        xc = xc + xs * cw_ref[CONV_W - 1 - d:CONV_W - d, :]
    a, u = _lru_gates(xc, wa_ref, wx_ref, ba_ref[...], bx_ref[...], lam_ref[...])
    a_scr[...] = a
    u_scr[...] = u
    row8 = lax.broadcasted_iota(jnp.int32, (8, wb), 0)

    def body(bi, h):
        r0 = pl.multiple_of(bi * 8, 8)
        a8 = a_scr[pl.ds(r0, 8), :]
        u8 = u_scr[pl.ds(r0, 8), :]
        for d in (1, 2, 4):
            a_sh = jnp.where(row8 >= d, pltpu.roll(a8, d, 0), 1.0)
            u_sh = jnp.where(row8 >= d, pltpu.roll(u8, d, 0), 0.0)
            u8 = a8 * u_sh + u8
            a8 = a8 * a_sh
        h8 = a8 * h + u8
        u_scr[pl.ds(r0, 8), :] = h8
        return h8[7:8, :]

    h = lax.fori_loop(0, t // 8, body, jnp.zeros((1, wb), F32))
    h_ref[...] = h
    o_ref[...] = (u_scr[...] * gate_ref[...].astype(F32)).astype(o_ref.dtype)


def lru_prompt(z, cx, cg, conv_w, conv_b, wa, wx, ba, bx, lam, b, t):
    w = lam.shape[0]
    wb = 512
    nh = wb // HEAD_DIM
    assert w % wb == 0 and t % 8 == 0 and wa.shape[1] == HEAD_DIM
    vec = lambda a: a.reshape(1, w)
    vspec = pl.BlockSpec((1, wb), lambda i, j: (0, j))
    blk = pl.BlockSpec((t, wb), lambda i, j: (i, j))
    zblk = lambda c0: pl.BlockSpec((t, wb), lambda i, j: (i, c0 + j))
    hspec = pl.BlockSpec((nh, HEAD_DIM, HEAD_DIM), lambda i, j: (j, 0, 0))
    return pl.pallas_call(
        _lru_prompt_kernel,
        grid=(b, w // wb),
        in_specs=[zblk(cx), zblk(cg), pl.BlockSpec((CONV_W, wb), lambda i, j: (0, j)), vspec,
                  hspec, hspec, vspec, vspec, vspec],
        out_specs=[blk, pl.BlockSpec((None, 1, wb), lambda i, j: (i, 0, j))],
        out_shape=[jax.ShapeDtypeStruct((b * t, w), BF16), jax.ShapeDtypeStruct((b, 1, w), F32)],
        scratch_shapes=[pltpu.VMEM((t, wb), F32), pltpu.VMEM((t, wb), F32)],
        compiler_params=_cparams(("parallel", "parallel")),
        name="lru_prompt",
    )(z, z, conv_w, vec(conv_b), wa.astype(BF16), wx.astype(BF16), vec(ba), vec(bx), vec(lam))


ATT_W = NSA_W + 6 * KV_W
SEG = 512


PACK_TAIL = 64


def _pack_kernel(a_ref, b_ref, o_ref, *, gn_tile, ngate):
    j = pl.program_id(0)

    @pl.when(j < gn_tile)
    def _():
        o_ref[...] = a_ref[...].T.astype(o_ref.dtype)

    @pl.when(j == gn_tile)
    def _():
        row = lax.broadcasted_iota(jnp.int32, a_ref.shape, 0)
        o_ref[...] = jnp.where(row < ngate, a_ref[...], 0.0).T.astype(o_ref.dtype)

    @pl.when(j > gn_tile)
    def _():
        o_ref[...] = jnp.concatenate([a_ref[ngate:, :], b_ref[:ngate, :]], axis=0).T.astype(o_ref.dtype)


def _pack_w_in(w_in, l, d_model, mlp_w, lru_w):
    ngate = N_BRANCH * N_HEADS
    sizes = (("att", ATT_W), ("gn", SEG), ("gate_nsa", NSA_W), ("uv", 2 * mlp_w), ("gate_mlp", mlp_w),
             ("xl", lru_w), ("gate_lru", lru_w), ("gm", N_BRANCH * d_model))
    offs, acc = {}, 0
    for name, size in sizes:
        offs[name] = (acc, size)
        acc += size
    wt = jnp.swapaxes(w_in, 1, 2)
    n_in, k = wt.shape[1:]
    assert acc == n_in - ngate + SEG and ATT_W % SEG == 0 and ngate <= PACK_TAIL and ngate % 16 == 0
    gn_tile = ATT_W // SEG
    w = pl.pallas_call(
        functools.partial(_pack_kernel, gn_tile=gn_tile, ngate=ngate),
        grid=(acc // SEG,),
        in_specs=[pl.BlockSpec((None, SEG, k), lambda j: (l, jnp.where(j > gn_tile, j - 1, j), 0)),
                  pl.BlockSpec((None, PACK_TAIL, k), lambda j: (l, (SEG // PACK_TAIL) * j, 0))],
        out_specs=pl.BlockSpec((k, SEG), lambda j: (0, j)),
        out_shape=jax.ShapeDtypeStruct((k, acc), BF16),
        compiler_params=_cparams(("parallel",)),
        name="pack_w",
    )(wt, wt)
    return w, offs


def _norm_vectors(q_g, k_g):
    one = jnp.ones((KV_W,), F32)
    zero = jnp.zeros((KV_W,), F32)
    gain = jnp.concatenate([jnp.tile(q_g, N_HEADS), one, one, jnp.tile(k_g[1], N_KV_HEADS), one,
                            jnp.tile(k_g[2], N_KV_HEADS), one])
    flag = jnp.concatenate([jnp.ones((NSA_W,), F32), zero, zero, one, zero, one, zero])
    return gain.reshape(1, ATT_W), flag.reshape(1, ATT_W)


MID_SEGMENTS = (("gate_nsa", "silu"), ("uv", "gelu"), ("gate_mlp", "silu"), ("xl", "none"), ("gate_lru", "silu"))


def _in_proj(x2d, xs2d, lw, tm):
    h = rms_rows(x2d, lw["norm_g"], min(tm, 512))
    hs = rms_rows(xs2d, lw["norm_g"], xs2d.shape[0])
    w, offs = lw["w_pack"]
    p = lambda name, act, dt, tn=COL_TILE, n=None, off=0, tm=tm, **kw: proj(
        h, hs, w, offs[name][0] + off, n or offs[name][1], act, dt if isinstance(dt, tuple) else (dt,),
        tm, tn, **kw)
    gain, flag = lw["gain"], lw["flag"]
    zq, zqs = p("att", "norm", BF16, n=NSA_W, gain=gain[:, :NSA_W], flag=flag[:, :NSA_W])
    kvf, kvb, kvh, kvs = p("att", "norm", (F32, BF16, "heads"), n=ATT_W - NSA_W, off=NSA_W, tn=SEG, tm=tm // 2,
                           gain=gain[:, NSA_W:], flag=flag[:, NSA_W:])
    z, zs = dict(q=zq, kvf=kvf, kvb=kvb, kvh=kvh), dict(q=zqs, kv=kvs)
    z["gn"], zs["gn"] = p("gn", "none", F32, tn=LANES, n=LANES)
    z["gm"], zs["gm"] = p("gm", "sigmoid", BF16)
    mid0 = offs["gate_nsa"][0]
    ranges, mid = [], {}
    for name, act in MID_SEGMENTS:
        lo = offs[name][0] - mid0
        ranges.append((act, lo // COL_TILE, (lo + offs[name][1]) // COL_TILE))
        mid[name] = (lo, offs[name][1])
    z["mid"], zs["mid"] = p("gate_nsa", "mixed", F32, n=sum(n for _, n in mid.values()), ranges=tuple(ranges))
    z["mid_off"] = zs["mid_off"] = mid
    return z, zs


ROW_TILE = 2048
COL_TILE = 256


def prompt_layer(x2d, z, lw, l, b, t, w_buf):
    tm = ROW_TILE
    kcmp, vcmp = compress_prompt(z["kvf"], b, t, lw["cmp_k"], lw["cmp_v"], lw["k_norm_g"][0:1])
    mid, off = z["mid"], z["mid_off"]
    assert off["gate_nsa"][0] == 0
    o_nsa = attn_prompt(z["q"], z["kvb"], kcmp, vcmp, z["gn"], mid, b, t)
    mw = off["gate_mlp"][1]
    o_mlp = mlp_prompt(mid, off["uv"][0] // mw, off["gate_mlp"][0] // mw, lw["mlp_ln_g"], lw["mlp_ln_b"],
                       lw["w_s"], lw["b_s"], 512)
    o_lru, h_last = lru_prompt(mid, off["xl"][0] // 512, off["gate_lru"][0] // 512, lw["conv_w"], lw["conv_b"],
                               lw["lru_wa"], lw["lru_wx"], lw["lru_ba"], lw["lru_bx"], lw["lru_lambda"], b, t)
    m = merge([o_nsa, o_mlp, o_lru], lw["w_br"], l, z["gm"], tm, COL_TILE)
    y = out_proj(m, lw["w_out"], l, x2d, tm, COL_TILE)
    kv = lambda i: z["kvh"][i].reshape(b, t, N_KV_HEADS, HEAD_DIM)
    xl = mid.reshape(b, t, -1)[:, t - (CONV_W - 1):, off["xl"][0]:off["xl"][0] + off["xl"][1]]
    assert t >= w_buf and t >= CONV_W - 1
    state = dict(cmp_k=kv(0), cmp_v=kv(1), sel_k=kv(2), sel_v=kv(3),
                 win_k=kv(4)[:, t - w_buf:], win_v=kv(5)[:, t - w_buf:],
                 lru_h=h_last.reshape(b, -1), lru_conv=xl)
    return y, state


PAGES_PER_STEP = 32


def _cmp_paged_kernel(pt_ref, *refs, pg):
    k_refs, v_refs = refs[:pg], refs[pg:2 * pg]
    wk_ref, wv_ref, ko_ref, vo_ref, r_scr = refs[2 * pg:]
    rows = k_refs[0].shape[0]
    sub = N_KV_HEADS * CMP_STRIDE
    nsp = rows // sub
    for page_refs, w_ref, o_ref in ((k_refs, wk_ref, ko_ref), (v_refs, wv_ref, vo_ref)):
        blocks = [jnp.concatenate([r[pl.ds(sub * i + 8 * m, 8), :] for m in range(sub // 8)], axis=1)
                  for r in page_refs for i in range(nsp)]
        res = _dot(jnp.concatenate(blocks, axis=0).astype(BF16), w_ref[...])
        n = res.shape[0]
        both = res[:, :2 * HEAD_DIM] + pltpu.roll(res[:, 2 * HEAD_DIM:], n - N_KV_HEADS, 0)
        r_scr[0] = both[:, :HEAD_DIM]
        r_scr[1] = both[:, HEAD_DIM:]
        for g in range(N_KV_HEADS):
            o_ref[g] = jnp.concatenate([r_scr[0, pl.ds(g, n // 8, stride=8), :],
                                        r_scr[1, pl.ds(g, n // 8, stride=8), :]], axis=1)


def _cmp_paged_weights(wcat):
    return wcat.reshape(CMP_STRIDE // 2, 2, HEAD_DIM, 2 * HEAD_DIM).transpose(0, 2, 1, 3).reshape(
        CMP_STRIDE // 2 * HEAD_DIM, 4 * HEAD_DIM)


def compress_paged(pool_k, pool_v, page_table, wkc, wvc, page0):
    b, n_pages = page_table.shape
    rows = pool_k.shape[1]
    pg = PAGES_PER_STEP
    assert n_pages % pg == 0 and N_KV_HEADS * 2 == 8
    nsp = rows // N_KV_HEADS // CMP_STRIDE
    wkc, wvc = _cmp_paged_weights(wkc), _cmp_paged_weights(wvc)
    page_spec = lambda p: pl.BlockSpec((None, rows, HEAD_DIM), lambda i, c, pt: (page0 + pt[i, c * pg + p], 0, 0))
    full = lambda a: pl.BlockSpec(a.shape, lambda i, c, pt: (0,) * a.ndim)
    out = jax.ShapeDtypeStruct((b, N_KV_HEADS, n_pages * nsp, 2 * HEAD_DIM), F32)
    ospec = pl.BlockSpec((None, N_KV_HEADS, pg * nsp, 2 * HEAD_DIM), lambda i, c, pt: (i, 0, c, 0))
    return pl.pallas_call(
        functools.partial(_cmp_paged_kernel, pg=pg),
        grid_spec=pltpu.PrefetchScalarGridSpec(
            num_scalar_prefetch=1,
            grid=(b, n_pages // pg),
            in_specs=[page_spec(p) for p in range(pg)] * 2 + [full(wkc), full(wvc)],
            out_specs=[ospec, ospec],
            scratch_shapes=[pltpu.VMEM((2, pg * nsp * 8, HEAD_DIM), F32)]),
        out_shape=[out, out],
        compiler_params=_cparams(("parallel", "arbitrary")),
        name="cmp_paged",
    )(page_table, *([pool_k] * pg), *([pool_v] * pg), wkc, wvc)


def _cmp_sample_kernel(tbk_ref, tbv_ref, q_ref, pek_ref, pev_ref, wkf_ref, wvf_ref, gk_ref, msel_ref,
                       o_ref, idx_ref, *, tpos, nc, ns):
    ck = _pe_const(pek_ref, wkf_ref)
    cv = _pe_const(pev_ref, wvf_ref)
    q = q_ref[...].astype(BF16)
    nh = q.shape[0]
    nsub = tbk_ref.shape[1]
    hrow = lax.broadcasted_iota(jnp.int32, (nh, 1), 0)
    blk = lax.broadcasted_iota(jnp.int32, (1, nsub), 1)
    cmask = (blk * CMP_STRIDE + (CMP_BLK - 1) <= tpos) & (blk < nc)
    row8 = lax.broadcasted_iota(jnp.int32, (8, nsub), 0)
    o = jnp.zeros((nh, HEAD_DIM), F32)
    psum = jnp.zeros((8, nsub), F32)
    for g in range(N_KV_HEADS):
        kc = _compress_combine(tbk_ref[g], ck, nc)
        ms = jnp.mean(kc * kc, axis=-1, keepdims=True)
        kc = (kc * lax.rsqrt(ms + EPS)) * gk_ref[...]
        vc = _compress_combine(tbv_ref[g], cv, nc)
        p = _softmax_rows(_dot_nt(q, kc.astype(BF16)) * SCALE, cmask & (hrow // Q_PER_KV == g))
        o = o + _dot(p.astype(BF16), vc.astype(BF16))
        psum = jnp.where(row8 == g, jnp.sum(p, axis=0, keepdims=True), psum)
    o_ref[...] = o
    imp = jnp.dot(psum, msel_ref[...], precision=lax.Precision.HIGHEST, preferred_element_type=F32)
    lane = lax.broadcasted_iota(jnp.int32, imp.shape, 1)
    cur = tpos // SEL_BLK
    forced = (lane == 0) | (lane == cur) | (lane == cur - 1)
    impm = jnp.where(lane > cur, -BIG, jnp.where(forced, BIG, imp))
    impm = jnp.where(lane < ns, impm, -jnp.inf)
    lane_f = lane.astype(F32)
    out_lane = lax.broadcasted_iota(jnp.int32, idx_ref.shape, 1)
    idxs = jnp.zeros(idx_ref.shape, F32)
    for j in range(min(N_SEL, ns)):
        mx = jnp.max(impm, axis=-1, keepdims=True)
        am = jnp.min(jnp.where(impm == mx, lane_f, 1e9), axis=-1, keepdims=True)
        idxs = jnp.where(out_lane == j, am, idxs)
        impm = jnp.where(lane_f == am, -jnp.inf, impm)
    idx_ref[...] = idxs.astype(jnp.int32)


def cmp_sample(tbk, tbv, q3, ck, cv, gk, tpos):
    b, nkv, nsub, _ = tbk.shape
    nc = (tpos + 1) // CMP_STRIDE - CMP_BLK // CMP_STRIDE + 1
    ns = -(-(tpos + 1) // SEL_BLK)
    ns_pad = -(-ns // LANES) * LANES
    assert nc <= nsub
    msel = _sel_matrix(nsub, nc, ns_pad, ns)
    _, pek, wkf = ck
    _, pev, wvf = cv
    full = lambda a: pl.BlockSpec(a.shape, lambda i: (0,) * a.ndim)
    tb_spec = pl.BlockSpec((None, nkv, nsub, 2 * HEAD_DIM), lambda i: (i, 0, 0, 0))
    return pl.pallas_call(
        functools.partial(_cmp_sample_kernel, tpos=tpos, nc=nc, ns=ns),
        grid=(b,),
        in_specs=[tb_spec, tb_spec, pl.BlockSpec((None, N_HEADS, HEAD_DIM), lambda i: (i, 0, 0)),
                  full(pek), full(pev), full(wkf), full(wvf), full(gk), full(msel)],
        out_specs=[pl.BlockSpec((None, N_HEADS, HEAD_DIM), lambda i: (i, 0, 0)),
                   pl.BlockSpec((None, 8, LANES), lambda i: (i, 0, 0))],
        out_shape=[jax.ShapeDtypeStruct((b, N_HEADS, HEAD_DIM), F32),
                   jax.ShapeDtypeStruct((b, 8, LANES), jnp.int32)],
        compiler_params=_cparams(("parallel",)),
        name="cmp_sample",
    )(tbk, tbv, q3, pek, pev, wkf, wvf, gk, msel)


def _attend_with_new(q, k_all, v_all, mask, k_new, v_new, new_ok):
    s = jnp.where(mask, _dot_nt(q.astype(BF16), k_all) * SCALE, NEG)
    s_new = jnp.where(new_ok, jnp.sum(q * k_new, axis=-1, keepdims=True) * SCALE, NEG)
    m = jnp.maximum(jnp.max(s, axis=-1, keepdims=True), s_new)
    e = jnp.where(mask, jnp.exp(s - m), 0.0)
    e_new = jnp.where(new_ok, jnp.exp(s_new - m), 0.0)
    d = jnp.sum(e, axis=-1, keepdims=True) + e_new
    o = _dot(e.astype(BF16), v_all) + e_new * v_new
    return o / jnp.where(d > 0.0, d, 1.0)


def _attn_sample_kernel(pt_ref, idx_ref, *refs, nblk, tpos, past_len, w_buf):
    k_refs, v_refs = refs[:nblk], refs[nblk:2 * nblk]
    (q_ref, ocmp_ref, kw_ref, vw_ref, ksn_ref, vsn_ref, kwn_ref, vwn_ref, gn_ref, gate_ref,
     o_ref) = refs[2 * nblk:]
    b = pl.program_id(0)
    g = pl.program_id(1)
    q = q_ref[...]
    rb = k_refs[0].shape[0]
    r = lax.broadcasted_iota(jnp.int32, (1, rb), 1)
    tok, hd = r // N_KV_HEADS, r % N_KV_HEADS
    masks = []
    new_sel = False
    for j in range(nblk):
        s = idx_ref[b, g * nblk + j]
        kpos = s * SEL_BLK + tok
        masks.append((hd == g) & (kpos <= tpos) & (kpos < past_len))
        new_sel = new_sel | (s == past_len // SEL_BLK)
    k_all = jnp.concatenate([kr[...].astype(BF16) for kr in k_refs], axis=0)
    v_all = jnp.concatenate([vr[...].astype(BF16) for vr in v_refs], axis=0)
    o_sel = _attend_with_new(q, k_all, v_all, jnp.concatenate(masks, axis=1),
                             ksn_ref[...], vsn_ref[...], new_sel & (past_len <= tpos))
    rw = lax.broadcasted_iota(jnp.int32, (1, kw_ref.shape[0]), 1)
    diff = tpos - (past_len - w_buf + rw // N_KV_HEADS)
    wmask = (rw % N_KV_HEADS == g) & (diff >= 0) & (diff <= WINDOW)
    o_win = _attend_with_new(q, kw_ref[...].astype(BF16), vw_ref[...].astype(BF16), wmask,
                             kwn_ref[...], vwn_ref[...], tpos - past_len <= WINDOW)
    gs = _sigmoid(gn_ref[...])
    o = gs[:, 0:1] * ocmp_ref[...] + gs[:, 1:2] * o_sel + gs[:, 2:3] * o_win
    o_ref[...] = o * gate_ref[...]


def attn_sample(pool_k, pool_v, page_table, idx, q4, ocmp4, win_k, win_v, ks_new, vs_new, kw_new, vw_new,
                gn4, gate4, past_len, w_buf, page_size, page0, seq0):
    b, n_pages = page_table.shape
    nblk = idx.shape[1] // N_KV_HEADS
    rb = pool_k.shape[1]
    halves = page_size // SEL_BLK
    tpos = past_len

    def blk_map(j):
        def f(i, g, pt, ix):
            s = ix[i, g * nblk + j]
            page = pt[i, jnp.minimum(s // halves, n_pages - 1)]
            return ((page0 + page) * halves + s % halves, 0, 0)
        return f

    blk_spec = lambda j: pl.BlockSpec((None, rb, HEAD_DIM), blk_map(j))
    per_bg = lambda a: pl.BlockSpec((None, None) + a.shape[2:], lambda i, g, pt, ix: (i, g, 0, 0))
    per_b = lambda a: pl.BlockSpec((None,) + a.shape[1:], lambda i, g, pt, ix: (seq0 + i, 0, 0))
    small = (q4, ocmp4)
    news = (ks_new, vs_new, kw_new, vw_new, gn4, gate4)
    return pl.pallas_call(
        functools.partial(_attn_sample_kernel, nblk=nblk, tpos=tpos, past_len=past_len, w_buf=w_buf),
        grid_spec=pltpu.PrefetchScalarGridSpec(
            num_scalar_prefetch=2,
            grid=(b, N_KV_HEADS),
            in_specs=[blk_spec(j) for j in range(nblk)] * 2 + [per_bg(a) for a in small]
            + [per_b(win_k), per_b(win_v)] + [per_bg(a) for a in news],
            out_specs=per_bg(q4)),
        out_shape=jax.ShapeDtypeStruct(q4.shape, F32),
        compiler_params=_cparams(("parallel", "arbitrary")),
        name="attn_sample",
    )(page_table, idx, *([pool_k] * nblk), *([pool_v] * nblk), q4, ocmp4, win_k, win_v, *news)


def _point_sample_kernel(u_ref, v_ref, gmlp_ref, lg_ref, lb_ref, ws0_ref, bs0_ref,
                         x_ref, glru_ref, buf_ref, h0_ref, cw_ref, cb_ref, wa_ref, wx_ref, ba_ref, bx_ref,
                         lam_ref, omlp_ref, vrow_ref, olru_ref, hnew_ref):
    vn = _layer_norm(v_ref[...], lg_ref[...], lb_ref[...])
    vrow_ref[...] = vn
    mixed = ws0_ref[...] * vn + bs0_ref[...]
    omlp_ref[...] = (u_ref[...] * mixed * gmlp_ref[...].astype(F32)).astype(omlp_ref.dtype)
    x = x_ref[...]
    xc = cb_ref[...] + x * cw_ref[CONV_W - 1:CONV_W, :]
    for j in range(CONV_W - 1):
        xc = xc + buf_ref[j] * cw_ref[j:j + 1, :]
    a, u = _lru_gates(xc, wa_ref, wx_ref, ba_ref[...], bx_ref[...], lam_ref[...])
    h = a * h0_ref[...] + u
    hnew_ref[...] = h
    olru_ref[...] = (h * glru_ref[...].astype(F32)).astype(olru_ref.dtype)


def point_sample(uv, gate_mlp, xl, gate_lru, buf_t, h0, lw):
    b, w2 = uv.shape
    w = w2 // 2
    gw = w // MLP_GROUPS
    vec = lambda a: a.reshape(1, -1)
    ws0 = vec(jnp.repeat(lw["w_s"][:, 0, 0], gw))
    bs0 = vec(jnp.repeat(lw["b_s"][:, 0], gw))
    args = (uv[:, :w], uv[:, w:], gate_mlp, vec(lw["mlp_ln_g"]), vec(lw["mlp_ln_b"]), ws0, bs0,
            xl, gate_lru, buf_t, h0, lw["conv_w"], vec(lw["conv_b"]), lw["lru_wa"].astype(BF16),
            lw["lru_wx"].astype(BF16), vec(lw["lru_ba"]), vec(lw["lru_bx"]), vec(lw["lru_lambda"]))
    lw_ = xl.shape[1]
    return pl.pallas_call(
        _point_sample_kernel,
        out_shape=[jax.ShapeDtypeStruct((b, w), BF16), jax.ShapeDtypeStruct((b, w), F32),
                   jax.ShapeDtypeStruct((b, lw_), BF16), jax.ShapeDtypeStruct((b, lw_), F32)],
        compiler_params=pltpu.CompilerParams(vmem_limit_bytes=VMEM_LIMIT),
        name="point_sample",
    )(*args)


def sample_layer(x2d, z, lw, l, caches, page_table, past_len, w_buf):
    b = x2d.shape[0]
    seg = lambda i: z["kv"][:, i * KV_W:(i + 1) * KV_W]
    kvh = lambda a: a.reshape(b, N_KV_HEADS, 1, HEAD_DIM)
    n_pool, page_size = caches["cmp_k"].shape[1:3]
    page0 = l * n_pool
    pool3 = lambda a: a.reshape(-1, page_size * N_KV_HEADS, HEAD_DIM)
    halves = lambda a: a.reshape(-1, SEL_BLK * N_KV_HEADS, HEAD_DIM)
    tbk, tbv = compress_paged(pool3(caches["cmp_k"]), pool3(caches["cmp_v"]), page_table,
                              lw["cmp_k"][0], lw["cmp_v"][0], page0)
    q3 = z["q"].reshape(b, N_HEADS, HEAD_DIM)
    o_cmp, idx = cmp_sample(tbk, tbv, q3, lw["cmp_k"], lw["cmp_v"], lw["k_norm_g"][0:1], past_len)
    n_sel = min(N_SEL, -(-(past_len + 1) // SEL_BLK))
    idx = idx[:, :N_KV_HEADS, :n_sel].reshape(b, N_KV_HEADS * n_sel)
    four = lambda a: a.reshape(b, N_KV_HEADS, Q_PER_KV, -1)
    mseg = lambda name: z["mid"][:, z["mid_off"][name][0]:z["mid_off"][name][0] + z["mid_off"][name][1]]
    gn4 = z["gn"][:, :N_BRANCH * N_HEADS].reshape(b, N_BRANCH, N_KV_HEADS, Q_PER_KV).transpose(0, 2, 3, 1)
    win3 = lambda a: a.reshape(-1, w_buf * N_KV_HEADS, HEAD_DIM)
    o_nsa = attn_sample(halves(caches["sel_k"]), halves(caches["sel_v"]), page_table, idx,
                        four(q3), four(o_cmp), win3(caches["win_k"]), win3(caches["win_v"]),
                        kvh(seg(2)), kvh(seg(3)), kvh(seg(4)), kvh(seg(5)),
                        gn4, four(mseg("gate_nsa")), past_len, w_buf, page_size, page0, l * b)
    o_mlp, v_rows, o_lru, h_new = point_sample(
        mseg("uv"), mseg("gate_mlp"), mseg("xl"), mseg("gate_lru"), caches["lru_conv"].transpose(1, 0, 2),
        caches["lru_h"].astype(F32), lw)
    m = merge([o_nsa.reshape(b, NSA_W).astype(BF16), o_mlp, o_lru], lw["w_br"], l, z["gm"], b, SEG)
    y = out_proj(m, lw["w_out"], l, x2d, b, SEG)
    tok = lambda a: a.reshape(b, 1, N_KV_HEADS, HEAD_DIM)
    state = dict(cmp_k=tok(seg(0)), cmp_v=tok(seg(1)), sel_k=tok(seg(2)), sel_v=tok(seg(3)),
                 win_k=jnp.concatenate([caches["win_k"][l], tok(seg(4))], axis=1)[:, -w_buf:],
                 win_v=jnp.concatenate([caches["win_v"][l], tok(seg(5))], axis=1)[:, -w_buf:],
                 lru_h=h_new, lru_conv=jnp.concatenate([caches["lru_conv"], mseg("xl")[:, None]], axis=1)[:, 1:],
                 mlp_v=v_rows[:, None])
    return y, state


def _layer_weights(l, p):
    d_model = p["w_in"].shape[1]
    mlp_w = p["mlp_ln_g"].shape[1]
    lru_w = p["lru_lambda"].shape[1]
    names = ("norm_g", "q_norm_g", "k_norm_g", "mlp_ln_g", "mlp_ln_b", "w_s", "b_s", "conv_w", "conv_b",
             "lru_wa", "lru_ba", "lru_wx", "lru_bx", "lru_lambda")
    lw = {n: p[n][l] for n in names}
    lw["w_pack"] = _pack_w_in(p["w_in"], l, d_model, mlp_w, lru_w)
    lw["gain"], lw["flag"] = _norm_vectors(p["q_norm_g"][l], p["k_norm_g"][l])
    lw["cmp_k"] = _cmp_weights(p["w_cmp_k"][l], p["cmp_pe_k"][l])
    lw["cmp_v"] = _cmp_weights(p["w_cmp_v"][l], p["cmp_pe_v"][l])
    lw["w_br"] = [p[n] for n in ("w_br_nsa_bf", "w_br_mlp_bf", "w_br_lru_bf")]
    lw["w_out"] = p["w_out_bf"]
    return lw


def kernel(x_prompt, x_sample, cache_cmp_k, cache_cmp_v, cache_sel_k, cache_sel_v, state_win_k, state_win_v,
           state_lru_h, state_lru_conv, page_table, norm_g, w_in, q_norm_g, k_norm_g, cmp_pe_k, cmp_pe_v,
           w_cmp_k, w_cmp_v, mlp_ln_g, mlp_ln_b, w_s, b_s, conv_w, conv_b, lru_wa, lru_ba, lru_wx, lru_bx,
           lru_lambda, w_br_nsa, w_br_mlp, w_br_lru, w_out):
    params = dict(norm_g=norm_g, w_in=w_in, q_norm_g=q_norm_g, k_norm_g=k_norm_g, cmp_pe_k=cmp_pe_k,
                  cmp_pe_v=cmp_pe_v, w_cmp_k=w_cmp_k, w_cmp_v=w_cmp_v, mlp_ln_g=mlp_ln_g, mlp_ln_b=mlp_ln_b,
                  w_s=w_s, b_s=b_s, conv_w=conv_w, conv_b=conv_b, lru_wa=lru_wa, lru_ba=lru_ba, lru_wx=lru_wx,
                  lru_bx=lru_bx, lru_lambda=lru_lambda, w_br_nsa=w_br_nsa, w_br_mlp=w_br_mlp,
                  w_br_lru=w_br_lru, w_out=w_out)
    depth = w_in.shape[0]
    b, t, d = x_prompt.shape
    bs, ts, _ = x_sample.shape
    assert ts == 1
    w_buf = state_win_k.shape[2]
    past_len = page_table.shape[1] * cache_cmp_k.shape[2]
    yp = x_prompt.reshape(b * t, d)
    ys = x_sample.reshape(bs * ts, d)
    p_st, s_st = [], []
    for n in ("w_br_nsa", "w_br_mlp", "w_br_lru", "w_out"):
        params[n + "_bf"] = params[n].astype(BF16)
    for l in range(depth):
        lw = _layer_weights(l, params)
        z, zs = _in_proj(yp, ys, lw, ROW_TILE)
        yp, sp = prompt_layer(yp, z, lw, l, b, t, w_buf)
        caches = dict(cmp_k=cache_cmp_k, cmp_v=cache_cmp_v, sel_k=cache_sel_k, sel_v=cache_sel_v,
                      win_k=state_win_k, win_v=state_win_v, lru_h=state_lru_h[l],
                      lru_conv=state_lru_conv[l])
        ys, ss = sample_layer(ys, zs, lw, l, caches, page_table, past_len, w_buf)
        p_st.append(sp)
        s_st.append(ss)
    stk = lambda sts, name: jnp.stack([st[name] for st in sts])
    names = ("cmp_k", "cmp_v", "sel_k", "sel_v", "win_k", "win_v", "lru_h", "lru_conv")
    return ((yp.reshape(b, t, d), ys.reshape(bs, ts, d))
            + tuple(stk(p_st, n) for n in names)
            + tuple(stk(s_st, n) for n in names + ("mlp_v",)))
```

```python
import functools

import numpy as np
import jax
import jax.numpy as jnp
from jax import lax
from jax.experimental import pallas as pl
from jax.experimental.pallas import tpu as pltpu

N_HEADS = 16
HEAD_DIM = 128
N_KV_HEADS = 4
Q_PER_KV = N_HEADS // N_KV_HEADS
NSA_W = N_HEADS * HEAD_DIM
KV_W = N_KV_HEADS * HEAD_DIM
CMP_BLK = 32
CMP_STRIDE = 16
SEL_BLK = 64
N_SEL = 16
WINDOW = 512
CHUNK = 128
MLP_GROUPS = 8
LRU_HEADS = 8
CONV_W = 4
LRU_C = 8.0
N_BRANCH = 3
EPS = 1e-6
BIG = 1e9
NEG = -1e30
SCALE = HEAD_DIM ** -0.5

LANES = 128
VMEM_LIMIT = 56 * 1024 * 1024

BF16 = jnp.bfloat16
F32 = jnp.float32


def _cparams(sem):
    return pltpu.CompilerParams(dimension_semantics=sem, vmem_limit_bytes=VMEM_LIMIT)


def _gelu(x):
    return 0.5 * x * (1.0 + jnp.tanh(0.7978845608028654 * (x + 0.044715 * (x * x * x))))


def _sigmoid(x):
    return 1.0 / (1.0 + jnp.exp(-x))


def _silu(x):
    return x * _sigmoid(x)


def _dot(a, b):
    return jnp.dot(a, b, preferred_element_type=F32)


def _dot_nt(a, b):
    return lax.dot_general(a, b, (((1,), (1,)), ((), ())), preferred_element_type=F32)


def _rms_kernel(x_ref, g_ref, o_ref):
    x = x_ref[...]
    ms = jnp.mean(x * x, axis=-1, keepdims=True)
    o_ref[...] = ((x * lax.rsqrt(ms + EPS)) * g_ref[...]).astype(o_ref.dtype)


def rms_rows(x, g, tm):
    m, d = x.shape
    return pl.pallas_call(
        _rms_kernel,
        grid=(m // tm,),
        in_specs=[pl.BlockSpec((tm, d), lambda i: (i, 0)),
                  pl.BlockSpec((1, d), lambda i: (0, 0))],
        out_specs=pl.BlockSpec((tm, d), lambda i: (i, 0)),
        out_shape=jax.ShapeDtypeStruct((m, d), BF16),
        compiler_params=_cparams(("parallel",)),
        name="rms_rows",
    )(x, g.reshape(1, d))


def _row_tile(shape):
    return pl.BlockSpec(shape, lambda i, j: (i, 0))


def _proj_act(acc, act, gain_ref, flag_ref):
    if act == "silu":
        return _silu(acc)
    if act == "gelu":
        return _gelu(acc)
    if act == "sigmoid":
        return _sigmoid(acc)
    if act == "norm":
        outs = []
        for c in range(acc.shape[1] // LANES):
            sl = slice(c * LANES, (c + 1) * LANES)
            z = acc[:, sl]
            ms = jnp.mean(z * z, axis=-1, keepdims=True)
            zn = (z * lax.rsqrt(ms + EPS)) * gain_ref[:, sl]
            outs.append(jnp.where(flag_ref[:, sl] > 0.5, zn, z))
        return jnp.concatenate(outs, axis=1)
    return acc


def _proj_kernel(a_ref, a2_ref, w_ref, *rest, act, outs, ranges):
    if act == "mixed":
        o_ref, o2_ref = rest
        j = pl.program_id(1)
        w = w_ref[...]

        def store(ref, x_ref):
            acc = _dot(x_ref[...], w)
            for a, lo, hi in ranges:
                @pl.when((j >= lo) & (j < hi))
                def _(a=a):
                    ref[...] = _proj_act(acc, a, None, None).astype(ref.dtype)

        store(o_ref, a_ref)
        pl.when(pl.program_id(0) == 0)(lambda: store(o2_ref, a2_ref))
        return
    if act == "norm":
        gain_ref, flag_ref = rest[:2]
        rest = rest[2:]
    else:
        gain_ref = flag_ref = None
    w = w_ref[...]
    z = _proj_act(_dot(a_ref[...], w), act, gain_ref, flag_ref)
    for kind, o_ref in zip(outs, rest):
        if kind == "heads":
            nh = z.shape[1] // HEAD_DIM
            for g in range(nh):
                o_ref[pl.ds(g, z.shape[0], stride=nh), :] = z[:, g * HEAD_DIM:(g + 1) * HEAD_DIM]
        else:
            o_ref[...] = z.astype(o_ref.dtype)
    o2_ref = rest[-1]

    @pl.when(pl.program_id(0) == 0)
    def _():
        o2_ref[...] = _proj_act(_dot(a2_ref[...], w), act, gain_ref, flag_ref).astype(o2_ref.dtype)


def proj(a, a2, w, row0, nrows, act, outs, tm, tn, gain=None, flag=None, ranges=None):
    m, k = a.shape
    m2 = a2.shape[0]
    assert row0 % tn == 0 and nrows % tn == 0 and m % tm == 0
    jb = row0 // tn
    nj = nrows // tn
    in_specs = [_row_tile((tm, k)),
                pl.BlockSpec((m2, k), lambda i, j: (0, 0)),
                pl.BlockSpec((k, tn), lambda i, j: (0, jb + j))]
    args = [a, a2, w]
    if act == "norm":
        in_specs += [pl.BlockSpec((1, tn), lambda i, j: (0, j))] * 2
        args += [gain, flag]
    nh = tn // HEAD_DIM
    out_specs, out_shape = [], []
    for kind in outs:
        if kind == "heads":
            out_specs.append(pl.BlockSpec((None, tm * nh, HEAD_DIM), lambda i, j: (j, i, 0)))
            out_shape.append(jax.ShapeDtypeStruct((nj, m * nh, HEAD_DIM), F32))
        else:
            out_specs.append(pl.BlockSpec((tm, tn), lambda i, j: (i, j)))
            out_shape.append(jax.ShapeDtypeStruct((m, nrows), kind))
    out_specs.append(pl.BlockSpec((m2, tn), lambda i, j: (0, jnp.where(i == 0, j, nj - 1))))
    out_shape.append(jax.ShapeDtypeStruct((m2, nrows), F32 if act == "norm" else outs[0]))
    return pl.pallas_call(
        functools.partial(_proj_kernel, act=act, outs=tuple(outs), ranges=ranges),
        grid=(m // tm, nj),
        in_specs=in_specs,
        out_specs=out_specs,
        out_shape=out_shape,
        compiler_params=_cparams(("arbitrary", "arbitrary")),
        name="proj_" + act,
    )(*args)


def _merge_kernel(a1, a2, a3, w1, w2, w3, g1, g2, g3, o_ref):
    m = g1[...].astype(F32) * _dot(a1[...], w1[...])
    m = m + g2[...].astype(F32) * _dot(a2[...], w2[...])
    m = m + g3[...].astype(F32) * _dot(a3[...], w3[...])
    o_ref[...] = m.astype(o_ref.dtype)


def merge(a_list, w_list, l, gm, tm, tn):
    m = a_list[0].shape[0]
    n = w_list[0].shape[2]
    nb = n // tn
    in_specs = [_row_tile((tm, a.shape[1])) for a in a_list]
    in_specs += [pl.BlockSpec((None, w.shape[1], tn), lambda i, j: (l, 0, j)) for w in w_list]
    in_specs += [pl.BlockSpec((tm, tn), functools.partial(lambda i, j, b: (i, b * nb + j), b=b))
                 for b in range(N_BRANCH)]
    return pl.pallas_call(
        _merge_kernel,
        grid=(m // tm, nb),
        in_specs=in_specs,
        out_specs=pl.BlockSpec((tm, tn), lambda i, j: (i, j)),
        out_shape=jax.ShapeDtypeStruct((m, n), BF16),
        compiler_params=_cparams(("parallel", "arbitrary")),
        name="merge",
    )(*a_list, *w_list, gm, gm, gm)


def _resid_kernel(a_ref, w_ref, x_ref, o_ref):
    o_ref[...] = x_ref[...] + _dot(a_ref[...], w_ref[...])


def out_proj(a, w, l, x, tm, tn):
    m, k = a.shape
    n = w.shape[2]
    return pl.pallas_call(
        _resid_kernel,
        grid=(m // tm, n // tn),
        in_specs=[_row_tile((tm, k)),
                  pl.BlockSpec((None, k, tn), lambda i, j: (l, 0, j)),
                  pl.BlockSpec((tm, tn), lambda i, j: (i, j))],
        out_specs=pl.BlockSpec((tm, tn), lambda i, j: (i, j)),
        out_shape=jax.ShapeDtypeStruct((m, n), F32),
        compiler_params=_cparams(("parallel", "arbitrary")),
        name="out_proj",
    )(a, w, x)


def _softmax_rows(s, mask):
    s = jnp.where(mask, s, NEG)
    m = jnp.max(s, axis=-1, keepdims=True)
    e = jnp.where(mask, jnp.exp(s - m), 0.0)
    d = jnp.sum(e, axis=-1, keepdims=True)
    return e / jnp.where(d > 0.0, d, 1.0)


def _compress_combine(acc, const, nvalid):
    nsub = acc.shape[0]
    top = acc[:, :LANES]
    bot = pltpu.roll(acc[:, LANES:], nsub - 1, 0)
    row = lax.broadcasted_iota(jnp.int32, (nsub, LANES), 0)
    return jnp.where(row < nvalid, top + bot + const, 0.0)


def _pe_const(pe_ref, w_ref):
    return jnp.dot(pe_ref[...], w_ref[...], precision=lax.Precision.HIGHEST,
                   preferred_element_type=F32)[0:1, :]


def _cmp_prompt_kernel(k_ref, v_ref, wk_ref, wv_ref, pek_ref, pev_ref, wkf_ref, wvf_ref, g_ref,
                       ko_ref, vo_ref):
    nsub = ko_ref.shape[0]
    acck = jnp.zeros((nsub, 2 * HEAD_DIM), F32)
    accv = jnp.zeros((nsub, 2 * HEAD_DIM), F32)
    for l in range(CMP_STRIDE):
        xk = k_ref[pl.ds(l, nsub, stride=CMP_STRIDE), :].astype(BF16)
        xv = v_ref[pl.ds(l, nsub, stride=CMP_STRIDE), :].astype(BF16)
        acck = acck + _dot(xk, wk_ref[l])
        accv = accv + _dot(xv, wv_ref[l])
    kc = _compress_combine(acck, _pe_const(pek_ref, wkf_ref), nsub - 1)
    ms = jnp.mean(kc * kc, axis=-1, keepdims=True)
    ko_ref[...] = (kc * lax.rsqrt(ms + EPS)) * g_ref[...]
    vo_ref[...] = _compress_combine(accv, _pe_const(pev_ref, wvf_ref), nsub - 1)


def _cmp_weights(w_cmp, pe):
    wcat = jnp.concatenate([w_cmp[:CMP_STRIDE], w_cmp[CMP_STRIDE:]], axis=2).astype(BF16)
    pe8 = jnp.broadcast_to(pe.reshape(1, CMP_BLK * HEAD_DIM), (8, CMP_BLK * HEAD_DIM))
    return wcat, pe8, w_cmp.reshape(CMP_BLK * HEAD_DIM, HEAD_DIM)


def compress_prompt(zf, b, t, wk, wv, gk):
    nsub = t // CMP_STRIDE
    wkc, pek, wkf = wk
    wvc, pev, wvf = wv
    full = lambda a: pl.BlockSpec(a.shape, lambda i, g: (0,) * a.ndim)
    head = lambda off: pl.BlockSpec((t, HEAD_DIM), lambda i, g: (i, off // HEAD_DIM + g))
    out = jax.ShapeDtypeStruct((b * N_KV_HEADS, nsub, HEAD_DIM), F32)
    return pl.pallas_call(
        _cmp_prompt_kernel,
        grid=(b, N_KV_HEADS),
        in_specs=[head(0), head(KV_W),
                  full(wkc), full(wvc), full(pek), full(pev), full(wkf), full(wvf), full(gk)],
        out_specs=[pl.BlockSpec((None, nsub, HEAD_DIM), lambda i, g: (i * N_KV_HEADS + g, 0, 0))] * 2,
        out_shape=[out, out],
        compiler_params=_cparams(("parallel", "parallel")),
        name="cmp_prompt",
    )(zf, zf, wkc, wvc, pek, pev, wkf, wvf, gk)


def _sel_matrix(nc_pad, nc, ns_pad, ns):
    c0 = np.arange(nc_pad) * CMP_STRIDE
    s0 = np.arange(ns_pad) * SEL_BLK
    m = (c0[:, None] < s0[None, :] + SEL_BLK) & (c0[:, None] + CMP_BLK > s0[None, :])
    m &= (np.arange(nc_pad)[:, None] < nc) & (np.arange(ns_pad)[None, :] < ns)
    return jnp.asarray(m.astype(np.float32))


EXP_C = SCALE * 1.4426950408889634


def _select_mask_t(imp_t, tpos_row, ns):
    blk = lax.broadcasted_iota(jnp.int32, imp_t.shape, 0)
    cur = tpos_row // SEL_BLK
    forced = (blk == 0) | (blk == cur) | (blk == cur - 1)
    impm = jnp.where(blk > cur, -BIG, jnp.where(forced, BIG, imp_t))
    rank = jnp.zeros(imp_t.shape, F32)
    for s2 in range(ns):
        row = impm[s2:s2 + 1, :]
        gt = jnp.where(row > impm, 1.0, 0.0)
        ge = jnp.where(row >= impm, 1.0, 0.0)
        rank = rank + jnp.where(blk > s2, ge, gt)
    return jnp.where(rank < float(min(N_SEL, ns)), 1.0, 0.0)


def _softmax_bias(s3, bias, valid):
    sm = s3 + bias[None]
    m = jnp.max(sm, axis=-1, keepdims=True)
    e = jnp.exp2((sm - m) * EXP_C)
    inv = 1.0 / jnp.sum(e, axis=-1, keepdims=True)
    if valid is not None:
        inv = jnp.where(valid[None], inv, 0.0)
    return e * inv


def _attn_prompt_kernel(q_ref, kc_ref, vc_ref, ks_ref, vs_ref, kw_ref, vw_ref, gn_ref, gate_ref,
                        msel_ref, e_ref, o_ref, *, tq, kvc, ns):
    g = pl.program_id(1)
    t0 = pl.program_id(2) * tq
    nh = Q_PER_KV
    rows = nh * tq
    q4 = jnp.concatenate([q_ref[:, h * HEAD_DIM:(h + 1) * HEAD_DIM] for h in range(nh)], axis=0)
    tpos = t0 + lax.broadcasted_iota(jnp.int32, (tq, 1), 0)
    tpos_row = t0 + lax.broadcasted_iota(jnp.int32, (1, tq), 1)

    ncp = kc_ref.shape[0]
    endpos = lax.broadcasted_iota(jnp.int32, (1, ncp), 1) * CMP_STRIDE + (CMP_BLK - 1)
    bias_c = jnp.where(endpos <= tpos, 0.0, NEG)
    p3 = _softmax_bias(_dot_nt(q4, kc_ref[...].astype(BF16)).reshape(nh, tq, ncp), bias_c,
                       tpos >= CMP_BLK - 1)
    o_cmp = _dot(p3.reshape(rows, ncp).astype(BF16), vc_ref[...].astype(BF16))
    imp = jnp.dot(jnp.sum(p3, axis=0), msel_ref[...], precision=lax.Precision.HIGHEST,
                  preferred_element_type=F32)
    ns8 = -(-ns // 8) * 8
    sel_t = _select_mask_t(imp.T[:ns8], tpos_row, ns)
    sel = jnp.concatenate([sel_t, jnp.zeros((imp.shape[1] - ns8, tq), F32)], axis=0).T.astype(BF16)

    def body(c, carry):
        m, l, acc = carry
        k0 = pl.multiple_of(c * kvc, kvc)
        kpos = k0 + lax.broadcasted_iota(jnp.int32, (1, kvc), 1)
        bias = jnp.where(kpos <= tpos, (_dot(sel, e_ref[c]) - 1.0) * -NEG, NEG)
        sm = _dot_nt(q4, ks_ref[pl.ds(k0, kvc), :]).reshape(nh, tq, kvc) + bias[None]
        m_new = jnp.maximum(m, jnp.max(sm, axis=-1, keepdims=True))
        alpha = jnp.exp2((m - m_new) * EXP_C)
        e = jnp.exp2((sm - m_new) * EXP_C)
        l = alpha * l + jnp.sum(e, axis=-1, keepdims=True)
        pv = _dot(e.reshape(rows, kvc).astype(BF16), vs_ref[pl.ds(k0, kvc), :])
        return m_new, l, alpha * acc + pv.reshape(nh, tq, HEAD_DIM)

    nch = (t0 + tq + kvc - 1) // kvc
    m, l, acc = lax.fori_loop(0, nch, body, (jnp.full((nh, tq, 1), NEG, F32), jnp.zeros((nh, tq, 1), F32),
                                             jnp.zeros((nh, tq, HEAD_DIM), F32)))
    o_sel = (acc * jnp.where(l > 0.0, 1.0 / l, 0.0)).reshape(rows, HEAD_DIM)

    nwin = WINDOW + tq
    w0 = pl.multiple_of(jnp.maximum(t0 - WINDOW, 0), tq)
    diff = tpos - (w0 + lax.broadcasted_iota(jnp.int32, (1, nwin), 1))
    bias_w = jnp.where((diff >= 0) & (diff <= WINDOW), 0.0, NEG)
    pw = _softmax_bias(_dot_nt(q4, kw_ref[pl.ds(w0, nwin), :]).reshape(nh, tq, nwin), bias_w, None)
    o_win = _dot(pw.reshape(rows, nwin).astype(BF16), vw_ref[pl.ds(w0, nwin), :])

    gs = _sigmoid(gn_ref[...])
    lane = lax.broadcasted_iota(jnp.int32, gs.shape, 1)
    for h in range(nh):
        r = slice(h * tq, (h + 1) * tq)
        o = None
        for br, ob in enumerate((o_cmp, o_sel, o_win)):
            cidx = br * N_HEADS + g * nh + h
            gcol = jnp.sum(jnp.where(lane == cidx, gs, 0.0), axis=-1, keepdims=True)
            o = gcol * ob[r] if o is None else o + gcol * ob[r]
        hs = slice(h * HEAD_DIM, (h + 1) * HEAD_DIM)
        o_ref[:, hs] = (o * gate_ref[:, hs].astype(F32)).astype(o_ref.dtype)


def attn_prompt(zq, zkv, kcmp, vcmp, gn, gate, b, t):
    tq, kvc = 256, 512
    assert t // CMP_STRIDE == LANES and t % kvc == 0 and t >= WINDOW + tq
    nq = t // tq
    ns = -(-t // SEL_BLK)
    nc = t // CMP_STRIDE - CMP_BLK // CMP_STRIDE + 1
    msel = _sel_matrix(LANES, nc, LANES, ns)
    kk = np.arange(t)
    e3 = (kk[None, :] // SEL_BLK == np.arange(LANES)[:, None]).astype(np.float32)
    e3 = jnp.asarray(e3.reshape(LANES, t // kvc, kvc).transpose(1, 0, 2), BF16)
    hb = lambda off: (lambda bi, g, i: (bi, off // HEAD_DIM + g))
    kv_spec = lambda off: pl.BlockSpec((t, HEAD_DIM), hb(off))
    cm_spec = pl.BlockSpec((None, t // CMP_STRIDE, HEAD_DIM), lambda bi, g, i: (bi * N_KV_HEADS + g, 0, 0))
    row4 = pl.BlockSpec((tq, Q_PER_KV * HEAD_DIM), lambda bi, g, i: (bi * nq + i, g))
    return pl.pallas_call(
        functools.partial(_attn_prompt_kernel, tq=tq, kvc=kvc, ns=ns),
        grid=(b, N_KV_HEADS, nq),
        in_specs=[row4, cm_spec, cm_spec,
                  kv_spec(2 * KV_W), kv_spec(3 * KV_W), kv_spec(4 * KV_W), kv_spec(5 * KV_W),
                  pl.BlockSpec((tq, LANES), lambda bi, g, i: (bi * nq + i, 0)),
                  row4,
                  pl.BlockSpec(msel.shape, lambda bi, g, i: (0, 0)),
                  pl.BlockSpec(e3.shape, lambda bi, g, i: (0, 0, 0))],
        out_specs=row4,
        out_shape=jax.ShapeDtypeStruct((b * t, NSA_W), BF16),
        compiler_params=_cparams(("parallel", "parallel", "arbitrary")),
        name="attn_prompt",
    )(zq, kcmp, vcmp, zkv, zkv, zkv, zkv, gn, gate, msel, e3)


def _layer_norm(v, g, b):
    vc = v - jnp.mean(v, axis=-1, keepdims=True)
    var = jnp.mean(vc * vc, axis=-1, keepdims=True)
    return vc * lax.rsqrt(var + EPS) * g + b


def _mlp_prompt_kernel(u_ref, v_ref, gate_ref, lg_ref, lb_ref, ws_ref, bst_ref, o_ref):
    tm = u_ref.shape[0]
    gw = ws_ref.shape[1]
    vb = _layer_norm(v_ref[...], lg_ref[...], lb_ref[...]).astype(BF16)
    row = lax.broadcasted_iota(jnp.int32, (CHUNK, CHUNK), 0)
    col = lax.broadcasted_iota(jnp.int32, (CHUNK, CHUNK), 1)
    for gi in range(MLP_GROUPS):
        cs = slice(gi * gw, (gi + 1) * gw)
        wsg = jnp.where(row >= col, ws_ref[gi], 0.0).astype(BF16)
        bias = bst_ref[:, gi:gi + 1]
        for ch in range(tm // CHUNK):
            rs = slice(ch * CHUNK, (ch + 1) * CHUNK)
            mixed = _dot(wsg, vb[rs, cs]) + bias
            o_ref[rs, cs] = (u_ref[rs, cs] * mixed * gate_ref[rs, cs].astype(F32)).astype(o_ref.dtype)


def mlp_prompt(z, cu, cg, ln_g, ln_b, w_s, b_s, tm):
    m = z.shape[0]
    w = ln_g.shape[0]
    assert w // MLP_GROUPS == CHUNK == w_s.shape[1]
    full = lambda a: pl.BlockSpec(a.shape, lambda i: (0,) * a.ndim)
    bst = b_s.T
    return pl.pallas_call(
        _mlp_prompt_kernel,
        grid=(m // tm,),
        in_specs=[pl.BlockSpec((tm, w), lambda i: (i, cu)), pl.BlockSpec((tm, w), lambda i: (i, cu + 1)),
                  pl.BlockSpec((tm, w), lambda i: (i, cg)),
                  pl.BlockSpec((1, w), lambda i: (0, 0)), pl.BlockSpec((1, w), lambda i: (0, 0)),
                  full(w_s), full(bst)],
        out_specs=pl.BlockSpec((tm, w), lambda i: (i, 0)),
        out_shape=jax.ShapeDtypeStruct((m, w), BF16),
        compiler_params=_cparams(("parallel",)),
        name="mlp_prompt",
    )(z, z, z, ln_g.reshape(1, w), ln_b.reshape(1, w), w_s, bst)


def _softplus(x):
    return jnp.maximum(x, 0.0) + jnp.log1p(jnp.exp(-jnp.abs(x)))


def _lru_gates(xc, wa_ref, wx_ref, ba, bx, lam):
    nh = xc.shape[1] // HEAD_DIM
    rs, is_ = [], []
    for hh in range(nh):
        xh = xc[:, hh * HEAD_DIM:(hh + 1) * HEAD_DIM].astype(BF16)
        rs.append(_dot(xh, wa_ref[hh]))
        is_.append(_dot(xh, wx_ref[hh]))
    r = _sigmoid(jnp.concatenate(rs, axis=1) + ba)
    i = _sigmoid(jnp.concatenate(is_, axis=1) + bx)
    log_a = -LRU_C * r * _softplus(-lam)
    th = jnp.tanh(log_a)
    return jnp.exp(log_a), jnp.sqrt(-2.0 * th / (1.0 - th)) * (i * xc)


SCAN_GROUP = 4


def _lru_prompt_kernel(x_ref, gate_ref, cw_ref, cb_ref, wa_ref, wx_ref, ba_ref, bx_ref, lam_ref,
                       o_ref, h_ref, a_scr, u_scr):
    t, wb = x_ref.shape
    x = x_ref[...]
    row = lax.broadcasted_iota(jnp.int32, (t, wb), 0)
    xc = cb_ref[...] + x * cw_ref[CONV_W - 1:CONV_W, :]
    for d in range(1, CONV_W):
        xs = jnp.where(row >= d, pltpu.roll(x, d, 0), 0.0)
        xc = xc + xs * cw_ref[CONV_W - 1 - d:CONV_W - d, :]
    a, u = _lru_gates(xc, wa_ref, wx_ref, ba_ref[...], bx_ref[...], lam_ref[...])
    a_scr[...] = a
    u_scr[...] = u
    row8 = lax.broadcasted_iota(jnp.int32, (8, wb), 0)

    def body(bi, h):
        r0 = pl.multiple_of(bi * (8 * SCAN_GROUP), 8 * SCAN_GROUP)
        tiles = []
        for k in range(SCAN_GROUP):
            a8 = a_scr[pl.ds(r0 + 8 * k, 8), :]
            u8 = u_scr[pl.ds(r0 + 8 * k, 8), :]
            for d in (1, 2, 4):
                a_sh = jnp.where(row8 >= d, pltpu.roll(a8, d, 0), 1.0)
                u_sh = jnp.where(row8 >= d, pltpu.roll(u8, d, 0), 0.0)
                u8 = a8 * u_sh + u8
                a8 = a8 * a_sh
            tiles.append((a8, u8))
        for k, (a8, u8) in enumerate(tiles):
            h8 = a8 * h + u8
            u_scr[pl.ds(r0 + 8 * k, 8), :] = h8
            h = h8[7:8, :]
        return h

    h = lax.fori_loop(0, t // (8 * SCAN_GROUP), body, jnp.zeros((1, wb), F32))
    h_ref[...] = h
    o_ref[...] = (u_scr[...] * gate_ref[...].astype(F32)).astype(o_ref.dtype)


def lru_prompt(z, cx, cg, conv_w, conv_b, wa, wx, ba, bx, lam, b, t):
    w = lam.shape[0]
    wb = 512
    nh = wb // HEAD_DIM
    assert w % wb == 0 and t % (8 * SCAN_GROUP) == 0 and wa.shape[1] == HEAD_DIM
    vec = lambda a: a.reshape(1, w)
    vspec = pl.BlockSpec((1, wb), lambda i, j: (0, j))
    blk = pl.BlockSpec((t, wb), lambda i, j: (i, j))
    zblk = lambda c0: pl.BlockSpec((t, wb), lambda i, j: (i, c0 + j))
    hspec = pl.BlockSpec((nh, HEAD_DIM, HEAD_DIM), lambda i, j: (j, 0, 0))
    return pl.pallas_call(
        _lru_prompt_kernel,
        grid=(b, w // wb),
        in_specs=[zblk(cx), zblk(cg), pl.BlockSpec((CONV_W, wb), lambda i, j: (0, j)), vspec,
                  hspec, hspec, vspec, vspec, vspec],
        out_specs=[blk, pl.BlockSpec((None, 1, wb), lambda i, j: (i, 0, j))],
        out_shape=[jax.ShapeDtypeStruct((b * t, w), BF16), jax.ShapeDtypeStruct((b, 1, w), F32)],
        scratch_shapes=[pltpu.VMEM((t, wb), F32), pltpu.VMEM((t, wb), F32)],
        compiler_params=_cparams(("parallel", "parallel")),
        name="lru_prompt",
    )(z, z, conv_w, vec(conv_b), wa.astype(BF16), wx.astype(BF16), vec(ba), vec(bx), vec(lam))


ATT_W = NSA_W + 6 * KV_W
SEG = 512


PACK_TAIL = 64


def _pack_kernel(a_ref, b_ref, o_ref, *, gn_tile, ngate):
    j = pl.program_id(0)

    @pl.when(j < gn_tile)
    def _():
        o_ref[...] = a_ref[...].T.astype(o_ref.dtype)

    @pl.when(j == gn_tile)
    def _():
        row = lax.broadcasted_iota(jnp.int32, a_ref.shape, 0)
        o_ref[...] = jnp.where(row < ngate, a_ref[...], 0.0).T.astype(o_ref.dtype)

    @pl.when(j > gn_tile)
    def _():
        o_ref[...] = jnp.concatenate([a_ref[ngate:, :], b_ref[:ngate, :]], axis=0).T.astype(o_ref.dtype)


def _pack_w_in(w_in, l, d_model, mlp_w, lru_w):
    ngate = N_BRANCH * N_HEADS
    sizes = (("att", ATT_W), ("gn", SEG), ("gate_nsa", NSA_W), ("uv", 2 * mlp_w), ("gate_mlp", mlp_w),
             ("xl", lru_w), ("gate_lru", lru_w), ("gm", N_BRANCH * d_model))
    offs, acc = {}, 0
    for name, size in sizes:
        offs[name] = (acc, size)
        acc += size
    wt = jnp.swapaxes(w_in, 1, 2)
    n_in, k = wt.shape[1:]
    assert acc == n_in - ngate + SEG and ATT_W % SEG == 0 and ngate <= PACK_TAIL and ngate % 16 == 0
    gn_tile = ATT_W // SEG
    w = pl.pallas_call(
        functools.partial(_pack_kernel, gn_tile=gn_tile, ngate=ngate),
        grid=(acc // SEG,),
        in_specs=[pl.BlockSpec((None, SEG, k), lambda j: (l, jnp.where(j > gn_tile, j - 1, j), 0)),
                  pl.BlockSpec((None, PACK_TAIL, k), lambda j: (l, (SEG // PACK_TAIL) * j, 0))],
        out_specs=pl.BlockSpec((k, SEG), lambda j: (0, j)),
        out_shape=jax.ShapeDtypeStruct((k, acc), BF16),
        compiler_params=_cparams(("parallel",)),
        name="pack_w",
    )(wt, wt)
    return w, offs


def _norm_vectors(q_g, k_g):
    one = jnp.ones((KV_W,), F32)
    zero = jnp.zeros((KV_W,), F32)
    gain = jnp.concatenate([jnp.tile(q_g, N_HEADS), one, one, jnp.tile(k_g[1], N_KV_HEADS), one,
                            jnp.tile(k_g[2], N_KV_HEADS), one])
    flag = jnp.concatenate([jnp.ones((NSA_W,), F32), zero, zero, one, zero, one, zero])
    return gain.reshape(1, ATT_W), flag.reshape(1, ATT_W)


MID_SEGMENTS = (("gate_nsa", "silu"), ("uv", "gelu"), ("gate_mlp", "silu"), ("xl", "none"), ("gate_lru", "silu"))


def _in_proj(x2d, xs2d, lw, tm):
    h = rms_rows(x2d, lw["norm_g"], min(tm, 512))
    hs = rms_rows(xs2d, lw["norm_g"], xs2d.shape[0])
    w, offs = lw["w_pack"]
    p = lambda name, act, dt, tn=COL_TILE, n=None, off=0, tm=tm, **kw: proj(
        h, hs, w, offs[name][0] + off, n or offs[name][1], act, dt if isinstance(dt, tuple) else (dt,),
        tm, tn, **kw)
    gain, flag = lw["gain"], lw["flag"]
    zq, zqs = p("att", "norm", BF16, n=NSA_W, gain=gain[:, :NSA_W], flag=flag[:, :NSA_W])
    kvf, kvb, kvh, kvs = p("att", "norm", (F32, BF16, "heads"), n=ATT_W - NSA_W, off=NSA_W, tn=SEG,
                           gain=gain[:, NSA_W:], flag=flag[:, NSA_W:])
    z, zs = dict(q=zq, kvf=kvf, kvb=kvb, kvh=kvh), dict(q=zqs, kv=kvs)
    z["gn"], zs["gn"] = p("gn", "none", F32, tn=LANES, n=LANES)
    z["gm"], zs["gm"] = p("gm", "sigmoid", BF16)
    mid0 = offs["gate_nsa"][0]
    ranges, mid = [], {}
    for name, act in MID_SEGMENTS:
        lo = offs[name][0] - mid0
        ranges.append((act, lo // COL_TILE, (lo + offs[name][1]) // COL_TILE))
        mid[name] = (lo, offs[name][1])
    z["mid"], zs["mid"] = p("gate_nsa", "mixed", F32, n=sum(n for _, n in mid.values()), ranges=tuple(ranges))
    z["mid_off"] = zs["mid_off"] = mid
    return z, zs


ROW_TILE = 1024
COL_TILE = 512


def prompt_layer(x2d, z, lw, l, b, t, w_buf):
    tm = ROW_TILE
    kcmp, vcmp = compress_prompt(z["kvf"], b, t, lw["cmp_k"], lw["cmp_v"], lw["k_norm_g"][0:1])
    mid, off = z["mid"], z["mid_off"]
    assert off["gate_nsa"][0] == 0
    o_nsa = attn_prompt(z["q"], z["kvb"], kcmp, vcmp, z["gn"], mid, b, t)
    mw = off["gate_mlp"][1]
    o_mlp = mlp_prompt(mid, off["uv"][0] // mw, off["gate_mlp"][0] // mw, lw["mlp_ln_g"], lw["mlp_ln_b"],
                       lw["w_s"], lw["b_s"], 512)
    o_lru, h_last = lru_prompt(mid, off["xl"][0] // 512, off["gate_lru"][0] // 512, lw["conv_w"], lw["conv_b"],
                               lw["lru_wa"], lw["lru_wx"], lw["lru_ba"], lw["lru_bx"], lw["lru_lambda"], b, t)
    m = merge([o_nsa, o_mlp, o_lru], lw["w_br"], l, z["gm"], tm, COL_TILE)
    y = out_proj(m, lw["w_out"], l, x2d, tm, COL_TILE)
    kv = lambda i: z["kvh"][i].reshape(b, t, N_KV_HEADS, HEAD_DIM)
    xl = mid.reshape(b, t, -1)[:, t - (CONV_W - 1):, off["xl"][0]:off["xl"][0] + off["xl"][1]]
    assert t >= w_buf and t >= CONV_W - 1
    state = dict(cmp_k=kv(0), cmp_v=kv(1), sel_k=kv(2), sel_v=kv(3),
                 win_k=kv(4)[:, t - w_buf:], win_v=kv(5)[:, t - w_buf:],
                 lru_h=h_last.reshape(b, -1), lru_conv=xl)
    return y, state


PAGES_PER_STEP = 32


def _cmp_paged_kernel(pt_ref, *refs, pg):
    k_refs, v_refs = refs[:pg], refs[pg:2 * pg]
    wk_ref, wv_ref, ko_ref, vo_ref, r_scr = refs[2 * pg:]
    rows = k_refs[0].shape[0]
    sub = N_KV_HEADS * CMP_STRIDE
    nsp = rows // sub
    for page_refs, w_ref, o_ref in ((k_refs, wk_ref, ko_ref), (v_refs, wv_ref, vo_ref)):
        blocks = [jnp.concatenate([r[pl.ds(sub * i + 8 * m, 8), :] for m in range(sub // 8)], axis=1)
                  for r in page_refs for i in range(nsp)]
        res = _dot(jnp.concatenate(blocks, axis=0).astype(BF16), w_ref[...])
        n = res.shape[0]
        both = res[:, :2 * HEAD_DIM] + pltpu.roll(res[:, 2 * HEAD_DIM:], n - N_KV_HEADS, 0)
        r_scr[0] = both[:, :HEAD_DIM]
        r_scr[1] = both[:, HEAD_DIM:]
        for g in range(N_KV_HEADS):
            o_ref[g] = jnp.concatenate([r_scr[0, pl.ds(g, n // 8, stride=8), :],
                                        r_scr[1, pl.ds(g, n // 8, stride=8), :]], axis=1)


def _cmp_paged_weights(wcat):
    return wcat.reshape(CMP_STRIDE // 2, 2, HEAD_DIM, 2 * HEAD_DIM).transpose(0, 2, 1, 3).reshape(
        CMP_STRIDE // 2 * HEAD_DIM, 4 * HEAD_DIM)


def compress_paged(pool_k, pool_v, page_table, wkc, wvc, page0):
    b, n_pages = page_table.shape
    rows = pool_k.shape[1]
    pg = PAGES_PER_STEP
    assert n_pages % pg == 0 and N_KV_HEADS * 2 == 8
    nsp = rows // N_KV_HEADS // CMP_STRIDE
    wkc, wvc = _cmp_paged_weights(wkc), _cmp_paged_weights(wvc)
    page_spec = lambda p: pl.BlockSpec((None, rows, HEAD_DIM), lambda i, c, pt: (page0 + pt[i, c * pg + p], 0, 0))
    full = lambda a: pl.BlockSpec(a.shape, lambda i, c, pt: (0,) * a.ndim)
    out = jax.ShapeDtypeStruct((b, N_KV_HEADS, n_pages * nsp, 2 * HEAD_DIM), F32)
    ospec = pl.BlockSpec((None, N_KV_HEADS, pg * nsp, 2 * HEAD_DIM), lambda i, c, pt: (i, 0, c, 0))
    return pl.pallas_call(
        functools.partial(_cmp_paged_kernel, pg=pg),
        grid_spec=pltpu.PrefetchScalarGridSpec(
            num_scalar_prefetch=1,
            grid=(b, n_pages // pg),
            in_specs=[page_spec(p) for p in range(pg)] * 2 + [full(wkc), full(wvc)],
            out_specs=[ospec, ospec],
            scratch_shapes=[pltpu.VMEM((2, pg * nsp * 8, HEAD_DIM), F32)]),
        out_shape=[out, out],
        compiler_params=_cparams(("parallel", "arbitrary")),
        name="cmp_paged",
    )(page_table, *([pool_k] * pg), *([pool_v] * pg), wkc, wvc)


def _cmp_sample_kernel(tbk_ref, tbv_ref, q_ref, pek_ref, pev_ref, wkf_ref, wvf_ref, gk_ref, msel_ref,
                       o_ref, idx_ref, *, tpos, nc, ns):
    ck = _pe_const(pek_ref, wkf_ref)
    cv = _pe_const(pev_ref, wvf_ref)
    q = q_ref[...].astype(BF16)
    nh = q.shape[0]
    nsub = tbk_ref.shape[1]
    hrow = lax.broadcasted_iota(jnp.int32, (nh, 1), 0)
    blk = lax.broadcasted_iota(jnp.int32, (1, nsub), 1)
    cmask = (blk * CMP_STRIDE + (CMP_BLK - 1) <= tpos) & (blk < nc)
    row8 = lax.broadcasted_iota(jnp.int32, (8, nsub), 0)
    o = jnp.zeros((nh, HEAD_DIM), F32)
    psum = jnp.zeros((8, nsub), F32)
    for g in range(N_KV_HEADS):
        kc = _compress_combine(tbk_ref[g], ck, nc)
        ms = jnp.mean(kc * kc, axis=-1, keepdims=True)
        kc = (kc * lax.rsqrt(ms + EPS)) * gk_ref[...]
        vc = _compress_combine(tbv_ref[g], cv, nc)
        p = _softmax_rows(_dot_nt(q, kc.astype(BF16)) * SCALE, cmask & (hrow // Q_PER_KV == g))
        o = o + _dot(p.astype(BF16), vc.astype(BF16))
        psum = jnp.where(row8 == g, jnp.sum(p, axis=0, keepdims=True), psum)
    o_ref[...] = o
    imp = jnp.dot(psum, msel_ref[...], precision=lax.Precision.HIGHEST, preferred_element_type=F32)
    lane = lax.broadcasted_iota(jnp.int32, imp.shape, 1)
    cur = tpos // SEL_BLK
    forced = (lane == 0) | (lane == cur) | (lane == cur - 1)
    impm = jnp.where(lane > cur, -BIG, jnp.where(forced, BIG, imp))
    impm = jnp.where(lane < ns, impm, -jnp.inf)
    lane_f = lane.astype(F32)
    out_lane = lax.broadcasted_iota(jnp.int32, idx_ref.shape, 1)
    idxs = jnp.zeros(idx_ref.shape, F32)
    for j in range(min(N_SEL, ns)):
        mx = jnp.max(impm, axis=-1, keepdims=True)
        am = jnp.min(jnp.where(impm == mx, lane_f, 1e9), axis=-1, keepdims=True)
        idxs = jnp.where(out_lane == j, am, idxs)
        impm = jnp.where(lane_f == am, -jnp.inf, impm)
    idx_ref[...] = idxs.astype(jnp.int32)


def cmp_sample(tbk, tbv, q3, ck, cv, gk, tpos):
    b, nkv, nsub, _ = tbk.shape
    nc = (tpos + 1) // CMP_STRIDE - CMP_BLK // CMP_STRIDE + 1
    ns = -(-(tpos + 1) // SEL_BLK)
    ns_pad = -(-ns // LANES) * LANES
    assert nc <= nsub
    msel = _sel_matrix(nsub, nc, ns_pad, ns)
    _, pek, wkf = ck
    _, pev, wvf = cv
    full = lambda a: pl.BlockSpec(a.shape, lambda i: (0,) * a.ndim)
    tb_spec = pl.BlockSpec((None, nkv, nsub, 2 * HEAD_DIM), lambda i: (i, 0, 0, 0))
    return pl.pallas_call(
        functools.partial(_cmp_sample_kernel, tpos=tpos, nc=nc, ns=ns),
        grid=(b,),
        in_specs=[tb_spec, tb_spec, pl.BlockSpec((None, N_HEADS, HEAD_DIM), lambda i: (i, 0, 0)),
                  full(pek), full(pev), full(wkf), full(wvf), full(gk), full(msel)],
        out_specs=[pl.BlockSpec((None, N_HEADS, HEAD_DIM), lambda i: (i, 0, 0)),
                   pl.BlockSpec((None, 8, LANES), lambda i: (i, 0, 0))],
        out_shape=[jax.ShapeDtypeStruct((b, N_HEADS, HEAD_DIM), F32),
                   jax.ShapeDtypeStruct((b, 8, LANES), jnp.int32)],
        compiler_params=_cparams(("parallel",)),
        name="cmp_sample",
    )(tbk, tbv, q3, pek, pev, wkf, wvf, gk, msel)


def _attend_with_new(q, k_all, v_all, mask, k_new, v_new, new_ok):
    s = jnp.where(mask, _dot_nt(q.astype(BF16), k_all) * SCALE, NEG)
    s_new = jnp.where(new_ok, jnp.sum(q * k_new, axis=-1, keepdims=True) * SCALE, NEG)
    m = jnp.maximum(jnp.max(s, axis=-1, keepdims=True), s_new)
    e = jnp.where(mask, jnp.exp(s - m), 0.0)
    e_new = jnp.where(new_ok, jnp.exp(s_new - m), 0.0)
    d = jnp.sum(e, axis=-1, keepdims=True) + e_new
    o = _dot(e.astype(BF16), v_all) + e_new * v_new
    return o / jnp.where(d > 0.0, d, 1.0)


def _attn_sample_kernel(pt_ref, idx_ref, *refs, nblk, tpos, past_len, w_buf):
    k_refs, v_refs = refs[:nblk], refs[nblk:2 * nblk]
    (q_ref, ocmp_ref, kw_ref, vw_ref, ksn_ref, vsn_ref, kwn_ref, vwn_ref, gn_ref, gate_ref,
     o_ref) = refs[2 * nblk:]
    b = pl.program_id(0)
    g = pl.program_id(1)
    q = q_ref[...]
    rb = k_refs[0].shape[0]
    r = lax.broadcasted_iota(jnp.int32, (1, rb), 1)
    tok, hd = r // N_KV_HEADS, r % N_KV_HEADS
    masks = []
    new_sel = False
    for j in range(nblk):
        s = idx_ref[b, g * nblk + j]
        kpos = s * SEL_BLK + tok
        masks.append((hd == g) & (kpos <= tpos) & (kpos < past_len))
        new_sel = new_sel | (s == past_len // SEL_BLK)
    k_all = jnp.concatenate([kr[...].astype(BF16) for kr in k_refs], axis=0)
    v_all = jnp.concatenate([vr[...].astype(BF16) for vr in v_refs], axis=0)
    o_sel = _attend_with_new(q, k_all, v_all, jnp.concatenate(masks, axis=1),
                             ksn_ref[...], vsn_ref[...], new_sel & (past_len <= tpos))
    rw = lax.broadcasted_iota(jnp.int32, (1, kw_ref.shape[0]), 1)
    diff = tpos - (past_len - w_buf + rw // N_KV_HEADS)
    wmask = (rw % N_KV_HEADS == g) & (diff >= 0) & (diff <= WINDOW)
    o_win = _attend_with_new(q, kw_ref[...].astype(BF16), vw_ref[...].astype(BF16), wmask,
                             kwn_ref[...], vwn_ref[...], tpos - past_len <= WINDOW)
    gs = _sigmoid(gn_ref[...])
    o = gs[:, 0:1] * ocmp_ref[...] + gs[:, 1:2] * o_sel + gs[:, 2:3] * o_win
    o_ref[...] = o * gate_ref[...]


def attn_sample(pool_k, pool_v, page_table, idx, q4, ocmp4, win_k, win_v, ks_new, vs_new, kw_new, vw_new,
                gn4, gate4, past_len, w_buf, page_size, page0, seq0):
    b, n_pages = page_table.shape
    nblk = idx.shape[1] // N_KV_HEADS
    rb = pool_k.shape[1]
    halves = page_size // SEL_BLK
    tpos = past_len

    def blk_map(j):
        def f(i, g, pt, ix):
            s = ix[i, g * nblk + j]
            page = pt[i, jnp.minimum(s // halves, n_pages - 1)]
            return ((page0 + page) * halves + s % halves, 0, 0)
        return f

    blk_spec = lambda j: pl.BlockSpec((None, rb, HEAD_DIM), blk_map(j))
    per_bg = lambda a: pl.BlockSpec((None, None) + a.shape[2:], lambda i, g, pt, ix: (i, g, 0, 0))
    per_b = lambda a: pl.BlockSpec((None,) + a.shape[1:], lambda i, g, pt, ix: (seq0 + i, 0, 0))
    small = (q4, ocmp4)
    news = (ks_new, vs_new, kw_new, vw_new, gn4, gate4)
    return pl.pallas_call(
        functools.partial(_attn_sample_kernel, nblk=nblk, tpos=tpos, past_len=past_len, w_buf=w_buf),
        grid_spec=pltpu.PrefetchScalarGridSpec(
            num_scalar_prefetch=2,
            grid=(b, N_KV_HEADS),
            in_specs=[blk_spec(j) for j in range(nblk)] * 2 + [per_bg(a) for a in small]
            + [per_b(win_k), per_b(win_v)] + [per_bg(a) for a in news],
            out_specs=per_bg(q4)),
        out_shape=jax.ShapeDtypeStruct(q4.shape, F32),
        compiler_params=_cparams(("parallel", "arbitrary")),
        name="attn_sample",
    )(page_table, idx, *([pool_k] * nblk), *([pool_v] * nblk), q4, ocmp4, win_k, win_v, *news)


def _point_sample_kernel(u_ref, v_ref, gmlp_ref, lg_ref, lb_ref, ws0_ref, bs0_ref,
                         x_ref, glru_ref, buf_ref, h0_ref, cw_ref, cb_ref, wa_ref, wx_ref, ba_ref, bx_ref,
                         lam_ref, omlp_ref, vrow_ref, olru_ref, hnew_ref):
    vn = _layer_norm(v_ref[...], lg_ref[...], lb_ref[...])
    vrow_ref[...] = vn
    mixed = ws0_ref[...] * vn + bs0_ref[...]
    omlp_ref[...] = (u_ref[...] * mixed * gmlp_ref[...].astype(F32)).astype(omlp_ref.dtype)
    x = x_ref[...]
    xc = cb_ref[...] + x * cw_ref[CONV_W - 1:CONV_W, :]
    for j in range(CONV_W - 1):
        xc = xc + buf_ref[j] * cw_ref[j:j + 1, :]
    a, u = _lru_gates(xc, wa_ref, wx_ref, ba_ref[...], bx_ref[...], lam_ref[...])
    h = a * h0_ref[...] + u
    hnew_ref[...] = h
    olru_ref[...] = (h * glru_ref[...].astype(F32)).astype(olru_ref.dtype)


def point_sample(uv, gate_mlp, xl, gate_lru, buf_t, h0, lw):
    b, w2 = uv.shape
    w = w2 // 2
    gw = w // MLP_GROUPS
    vec = lambda a: a.reshape(1, -1)
    ws0 = vec(jnp.repeat(lw["w_s"][:, 0, 0], gw))
    bs0 = vec(jnp.repeat(lw["b_s"][:, 0], gw))
    args = (uv[:, :w], uv[:, w:], gate_mlp, vec(lw["mlp_ln_g"]), vec(lw["mlp_ln_b"]), ws0, bs0,
            xl, gate_lru, buf_t, h0, lw["conv_w"], vec(lw["conv_b"]), lw["lru_wa"].astype(BF16),
            lw["lru_wx"].astype(BF16), vec(lw["lru_ba"]), vec(lw["lru_bx"]), vec(lw["lru_lambda"]))
    lw_ = xl.shape[1]
    return pl.pallas_call(
        _point_sample_kernel,
        out_shape=[jax.ShapeDtypeStruct((b, w), BF16), jax.ShapeDtypeStruct((b, w), F32),
                   jax.ShapeDtypeStruct((b, lw_), BF16), jax.ShapeDtypeStruct((b, lw_), F32)],
        compiler_params=pltpu.CompilerParams(vmem_limit_bytes=VMEM_LIMIT),
        name="point_sample",
    )(*args)


def sample_layer(x2d, z, lw, l, caches, page_table, past_len, w_buf):
    b = x2d.shape[0]
    seg = lambda i: z["kv"][:, i * KV_W:(i + 1) * KV_W]
    kvh = lambda a: a.reshape(b, N_KV_HEADS, 1, HEAD_DIM)
    n_pool, page_size = caches["cmp_k"].shape[1:3]
    page0 = l * n_pool
    pool3 = lambda a: a.reshape(-1, page_size * N_KV_HEADS, HEAD_DIM)
    halves = lambda a: a.reshape(-1, SEL_BLK * N_KV_HEADS, HEAD_DIM)
    tbk, tbv = compress_paged(pool3(caches["cmp_k"]), pool3(caches["cmp_v"]), page_table,
                              lw["cmp_k"][0], lw["cmp_v"][0], page0)
    q3 = z["q"].reshape(b, N_HEADS, HEAD_DIM)
    o_cmp, idx = cmp_sample(tbk, tbv, q3, lw["cmp_k"], lw["cmp_v"], lw["k_norm_g"][0:1], past_len)
    n_sel = min(N_SEL, -(-(past_len + 1) // SEL_BLK))
    idx = idx[:, :N_KV_HEADS, :n_sel].reshape(b, N_KV_HEADS * n_sel)
    four = lambda a: a.reshape(b, N_KV_HEADS, Q_PER_KV, -1)
    mseg = lambda name: z["mid"][:, z["mid_off"][name][0]:z["mid_off"][name][0] + z["mid_off"][name][1]]
    gn4 = z["gn"][:, :N_BRANCH * N_HEADS].reshape(b, N_BRANCH, N_KV_HEADS, Q_PER_KV).transpose(0, 2, 3, 1)
    win3 = lambda a: a.reshape(-1, w_buf * N_KV_HEADS, HEAD_DIM)
    o_nsa = attn_sample(halves(caches["sel_k"]), halves(caches["sel_v"]), page_table, idx,
                        four(q3), four(o_cmp), win3(caches["win_k"]), win3(caches["win_v"]),
                        kvh(seg(2)), kvh(seg(3)), kvh(seg(4)), kvh(seg(5)),
                        gn4, four(mseg("gate_nsa")), past_len, w_buf, page_size, page0, l * b)
    o_mlp, v_rows, o_lru, h_new = point_sample(
        mseg("uv"), mseg("gate_mlp"), mseg("xl"), mseg("gate_lru"), caches["lru_conv"].transpose(1, 0, 2),
        caches["lru_h"].astype(F32), lw)
    m = merge([o_nsa.reshape(b, NSA_W).astype(BF16), o_mlp, o_lru], lw["w_br"], l, z["gm"], b, SEG)
    y = out_proj(m, lw["w_out"], l, x2d, b, SEG)
    tok = lambda a: a.reshape(b, 1, N_KV_HEADS, HEAD_DIM)
    state = dict(cmp_k=tok(seg(0)), cmp_v=tok(seg(1)), sel_k=tok(seg(2)), sel_v=tok(seg(3)),
                 win_k=jnp.concatenate([caches["win_k"][l], tok(seg(4))], axis=1)[:, -w_buf:],
                 win_v=jnp.concatenate([caches["win_v"][l], tok(seg(5))], axis=1)[:, -w_buf:],
                 lru_h=h_new, lru_conv=jnp.concatenate([caches["lru_conv"], mseg("xl")[:, None]], axis=1)[:, 1:],
                 mlp_v=v_rows[:, None])
    return y, state


def _layer_weights(l, p):
    d_model = p["w_in"].shape[1]
    mlp_w = p["mlp_ln_g"].shape[1]
    lru_w = p["lru_lambda"].shape[1]
    names = ("norm_g", "q_norm_g", "k_norm_g", "mlp_ln_g", "mlp_ln_b", "w_s", "b_s", "conv_w", "conv_b",
             "lru_wa", "lru_ba", "lru_wx", "lru_bx", "lru_lambda")
    lw = {n: p[n][l] for n in names}
    lw["w_pack"] = _pack_w_in(p["w_in"], l, d_model, mlp_w, lru_w)
    lw["gain"], lw["flag"] = _norm_vectors(p["q_norm_g"][l], p["k_norm_g"][l])
    lw["cmp_k"] = _cmp_weights(p["w_cmp_k"][l], p["cmp_pe_k"][l])
    lw["cmp_v"] = _cmp_weights(p["w_cmp_v"][l], p["cmp_pe_v"][l])
    lw["w_br"] = [p[n] for n in ("w_br_nsa_bf", "w_br_mlp_bf", "w_br_lru_bf")]
    lw["w_out"] = p["w_out_bf"]
    return lw


def kernel(x_prompt, x_sample, cache_cmp_k, cache_cmp_v, cache_sel_k, cache_sel_v, state_win_k, state_win_v,
           state_lru_h, state_lru_conv, page_table, norm_g, w_in, q_norm_g, k_norm_g, cmp_pe_k, cmp_pe_v,
           w_cmp_k, w_cmp_v, mlp_ln_g, mlp_ln_b, w_s, b_s, conv_w, conv_b, lru_wa, lru_ba, lru_wx, lru_bx,
           lru_lambda, w_br_nsa, w_br_mlp, w_br_lru, w_out):
    params = dict(norm_g=norm_g, w_in=w_in, q_norm_g=q_norm_g, k_norm_g=k_norm_g, cmp_pe_k=cmp_pe_k,
                  cmp_pe_v=cmp_pe_v, w_cmp_k=w_cmp_k, w_cmp_v=w_cmp_v, mlp_ln_g=mlp_ln_g, mlp_ln_b=mlp_ln_b,
                  w_s=w_s, b_s=b_s, conv_w=conv_w, conv_b=conv_b, lru_wa=lru_wa, lru_ba=lru_ba, lru_wx=lru_wx,
                  lru_bx=lru_bx, lru_lambda=lru_lambda, w_br_nsa=w_br_nsa, w_br_mlp=w_br_mlp,
                  w_br_lru=w_br_lru, w_out=w_out)
    depth = w_in.shape[0]
    b, t, d = x_prompt.shape
    bs, ts, _ = x_sample.shape
    assert ts == 1
    w_buf = state_win_k.shape[2]
    past_len = page_table.shape[1] * cache_cmp_k.shape[2]
    yp = x_prompt.reshape(b * t, d)
    ys = x_sample.reshape(bs * ts, d)
    p_st, s_st = [], []
    for n in ("w_br_nsa", "w_br_mlp", "w_br_lru", "w_out"):
        params[n + "_bf"] = params[n].astype(BF16)
    for l in range(depth):
        lw = _layer_weights(l, params)
        z, zs = _in_proj(yp, ys, lw, ROW_TILE)
        yp, sp = prompt_layer(yp, z, lw, l, b, t, w_buf)
        caches = dict(cmp_k=cache_cmp_k, cmp_v=cache_cmp_v, sel_k=cache_sel_k, sel_v=cache_sel_v,
                      win_k=state_win_k, win_v=state_win_v, lru_h=state_lru_h[l],
                      lru_conv=state_lru_conv[l])
        ys, ss = sample_layer(ys, zs, lw, l, caches, page_table, past_len, w_buf)
        p_st.append(sp)
        s_st.append(ss)
    stk = lambda sts, name: jnp.stack([st[name] for st in sts])
    names = ("cmp_k", "cmp_v", "sel_k", "sel_v", "win_k", "win_v", "lru_h", "lru_conv")
    return ((yp.reshape(b, t, d), ys.reshape(bs, ts, d))
            + tuple(stk(p_st, n) for n in names)
            + tuple(stk(s_st, n) for n in names + ("mlp_v",)))
```

```python
import functools

import numpy as np
import jax
import jax.numpy as jnp
from jax import lax
from jax.experimental import pallas as pl
from jax.experimental.pallas import tpu as pltpu

N_HEADS = 16
HEAD_DIM = 128
N_KV_HEADS = 4
Q_PER_KV = N_HEADS // N_KV_HEADS
NSA_W = N_HEADS * HEAD_DIM
KV_W = N_KV_HEADS * HEAD_DIM
CMP_BLK = 32
CMP_STRIDE = 16
SEL_BLK = 64
N_SEL = 16
WINDOW = 512
CHUNK = 128
MLP_GROUPS = 8
LRU_HEADS = 8
CONV_W = 4
LRU_C = 8.0
N_BRANCH = 3
EPS = 1e-6
BIG = 1e9
NEG = -1e30
SCALE = HEAD_DIM ** -0.5

LANES = 128
VMEM_LIMIT = 56 * 1024 * 1024

BF16 = jnp.bfloat16
F32 = jnp.float32


def _cparams(sem):
    return pltpu.CompilerParams(dimension_semantics=sem, vmem_limit_bytes=VMEM_LIMIT)


def _gelu(x):
    return 0.5 * x * (1.0 + jnp.tanh(0.7978845608028654 * (x + 0.044715 * (x * x * x))))


def _sigmoid(x):
    return 1.0 / (1.0 + jnp.exp(-x))


def _silu(x):
    return x * _sigmoid(x)


def _dot(a, b):
    return jnp.dot(a, b, preferred_element_type=F32)


def _dot_nt(a, b):
    return lax.dot_general(a, b, (((1,), (1,)), ((), ())), preferred_element_type=F32)


def _rms_kernel(x_ref, g_ref, o_ref):
    x = x_ref[...]
    ms = jnp.mean(x * x, axis=-1, keepdims=True)
    o_ref[...] = ((x * lax.rsqrt(ms + EPS)) * g_ref[...]).astype(o_ref.dtype)


def rms_rows(x, g, tm):
    m, d = x.shape
    return pl.pallas_call(
        _rms_kernel,
        grid=(m // tm,),
        in_specs=[pl.BlockSpec((tm, d), lambda i: (i, 0)),
                  pl.BlockSpec((1, d), lambda i: (0, 0))],
        out_specs=pl.BlockSpec((tm, d), lambda i: (i, 0)),
        out_shape=jax.ShapeDtypeStruct((m, d), BF16),
        compiler_params=_cparams(("parallel",)),
        name="rms_rows",
    )(x, g.reshape(1, d))


def _row_tile(shape):
    return pl.BlockSpec(shape, lambda i, j: (i, 0))


def _proj_act(acc, act, gain_ref, flag_ref):
    if act == "silu":
        return _silu(acc)
    if act == "gelu":
        return _gelu(acc)
    if act == "sigmoid":
        return _sigmoid(acc)
    if act == "norm":
        outs = []
        for c in range(acc.shape[1] // LANES):
            sl = slice(c * LANES, (c + 1) * LANES)
            z = acc[:, sl]
            ms = jnp.mean(z * z, axis=-1, keepdims=True)
            zn = (z * lax.rsqrt(ms + EPS)) * gain_ref[:, sl]
            outs.append(jnp.where(flag_ref[:, sl] > 0.5, zn, z))
        return jnp.concatenate(outs, axis=1)
    return acc


def _proj_kernel(a_ref, a2_ref, w_ref, *rest, act, outs, ranges):
    if act == "mixed":
        o_ref, o2_ref = rest
        j = pl.program_id(1)
        w = w_ref[...]

        def store(ref, x_ref):
            acc = _dot(x_ref[...], w)
            for a, lo, hi in ranges:
                @pl.when((j >= lo) & (j < hi))
                def _(a=a):
                    ref[...] = _proj_act(acc, a, None, None).astype(ref.dtype)

        store(o_ref, a_ref)
        pl.when(pl.program_id(0) == 0)(lambda: store(o2_ref, a2_ref))
        return
    if act == "norm":
        gain_ref, flag_ref = rest[:2]
        rest = rest[2:]
    else:
        gain_ref = flag_ref = None
    w = w_ref[...]
    z = _proj_act(_dot(a_ref[...], w), act, gain_ref, flag_ref)
    for kind, o_ref in zip(outs, rest):
        if kind == "heads":
            nh = z.shape[1] // HEAD_DIM
            for g in range(nh):
                o_ref[pl.ds(g, z.shape[0], stride=nh), :] = z[:, g * HEAD_DIM:(g + 1) * HEAD_DIM]
        else:
            o_ref[...] = z.astype(o_ref.dtype)
    o2_ref = rest[-1]

    @pl.when(pl.program_id(0) == 0)
    def _():
        o2_ref[...] = _proj_act(_dot(a2_ref[...], w), act, gain_ref, flag_ref).astype(o2_ref.dtype)


def proj(a, a2, w, row0, nrows, act, outs, tm, tn, gain=None, flag=None, ranges=None):
    m, k = a.shape
    m2 = a2.shape[0]
    assert row0 % tn == 0 and nrows % tn == 0 and m % tm == 0
    jb = row0 // tn
    nj = nrows // tn
    in_specs = [_row_tile((tm, k)),
                pl.BlockSpec((m2, k), lambda i, j: (0, 0)),
                pl.BlockSpec((k, tn), lambda i, j: (0, jb + j))]
    args = [a, a2, w]
    if act == "norm":
        in_specs += [pl.BlockSpec((1, tn), lambda i, j: (0, j))] * 2
        args += [gain, flag]
    nh = tn // HEAD_DIM
    out_specs, out_shape = [], []
    for kind in outs:
        if kind == "heads":
            out_specs.append(pl.BlockSpec((None, tm * nh, HEAD_DIM), lambda i, j: (j, i, 0)))
            out_shape.append(jax.ShapeDtypeStruct((nj, m * nh, HEAD_DIM), F32))
        else:
            out_specs.append(pl.BlockSpec((tm, tn), lambda i, j: (i, j)))
            out_shape.append(jax.ShapeDtypeStruct((m, nrows), kind))
    out_specs.append(pl.BlockSpec((m2, tn), lambda i, j: (0, jnp.where(i == 0, j, nj - 1))))
    out_shape.append(jax.ShapeDtypeStruct((m2, nrows), F32 if act == "norm" else outs[0]))
    return pl.pallas_call(
        functools.partial(_proj_kernel, act=act, outs=tuple(outs), ranges=ranges),
        grid=(m // tm, nj),
        in_specs=in_specs,
        out_specs=out_specs,
        out_shape=out_shape,
        compiler_params=_cparams(("arbitrary", "arbitrary")),
        name="proj_" + act,
    )(*args)


def _merge_kernel(a1, a2, a3, w1, w2, w3, g1, g2, g3, o_ref):
    m = g1[...].astype(F32) * _dot(a1[...], w1[...])
    m = m + g2[...].astype(F32) * _dot(a2[...], w2[...])
    m = m + g3[...].astype(F32) * _dot(a3[...], w3[...])
    o_ref[...] = m.astype(o_ref.dtype)


def merge(a_list, w_list, l, gm, tm, tn):
    m = a_list[0].shape[0]
    n = w_list[0].shape[2]
    nb = n // tn
    in_specs = [_row_tile((tm, a.shape[1])) for a in a_list]
    in_specs += [pl.BlockSpec((None, w.shape[1], tn), lambda i, j: (l, 0, j)) for w in w_list]
    in_specs += [pl.BlockSpec((tm, tn), functools.partial(lambda i, j, b: (i, b * nb + j), b=b))
                 for b in range(N_BRANCH)]
    return pl.pallas_call(
        _merge_kernel,
        grid=(m // tm, nb),
        in_specs=in_specs,
        out_specs=pl.BlockSpec((tm, tn), lambda i, j: (i, j)),
        out_shape=jax.ShapeDtypeStruct((m, n), BF16),
        compiler_params=_cparams(("parallel", "arbitrary")),
        name="merge",
    )(*a_list, *w_list, gm, gm, gm)


def _resid_kernel(a_ref, w_ref, x_ref, o_ref):
    o_ref[...] = x_ref[...] + _dot(a_ref[...], w_ref[...])


def out_proj(a, w, l, x, tm, tn):
    m, k = a.shape
    n = w.shape[2]
    return pl.pallas_call(
        _resid_kernel,
        grid=(m // tm, n // tn),
        in_specs=[_row_tile((tm, k)),
                  pl.BlockSpec((None, k, tn), lambda i, j: (l, 0, j)),
                  pl.BlockSpec((tm, tn), lambda i, j: (i, j))],
        out_specs=pl.BlockSpec((tm, tn), lambda i, j: (i, j)),
        out_shape=jax.ShapeDtypeStruct((m, n), F32),
        compiler_params=_cparams(("parallel", "arbitrary")),
        name="out_proj",
    )(a, w, x)


def _softmax_rows(s, mask):
    s = jnp.where(mask, s, NEG)
    m = jnp.max(s, axis=-1, keepdims=True)
    e = jnp.where(mask, jnp.exp(s - m), 0.0)
    d = jnp.sum(e, axis=-1, keepdims=True)
    return e / jnp.where(d > 0.0, d, 1.0)


def _compress_combine(acc, const, nvalid):
    nsub = acc.shape[0]
    top = acc[:, :LANES]
    bot = pltpu.roll(acc[:, LANES:], nsub - 1, 0)
    row = lax.broadcasted_iota(jnp.int32, (nsub, LANES), 0)
    return jnp.where(row < nvalid, top + bot + const, 0.0)


def _pe_const(pe_ref, w_ref):
    return jnp.dot(pe_ref[...], w_ref[...], precision=lax.Precision.HIGHEST,
                   preferred_element_type=F32)[0:1, :]


def _cmp_prompt_kernel(k_ref, v_ref, wk_ref, wv_ref, pek_ref, pev_ref, wkf_ref, wvf_ref, g_ref,
                       ko_ref, vo_ref):
    nsub = ko_ref.shape[0]
    acck = jnp.zeros((nsub, 2 * HEAD_DIM), F32)
    accv = jnp.zeros((nsub, 2 * HEAD_DIM), F32)
    for l in range(CMP_STRIDE):
        xk = k_ref[pl.ds(l, nsub, stride=CMP_STRIDE), :].astype(BF16)
        xv = v_ref[pl.ds(l, nsub, stride=CMP_STRIDE), :].astype(BF16)
        acck = acck + _dot(xk, wk_ref[l])
        accv = accv + _dot(xv, wv_ref[l])
    kc = _compress_combine(acck, _pe_const(pek_ref, wkf_ref), nsub - 1)
    ms = jnp.mean(kc * kc, axis=-1, keepdims=True)
    ko_ref[...] = (kc * lax.rsqrt(ms + EPS)) * g_ref[...]
    vo_ref[...] = _compress_combine(accv, _pe_const(pev_ref, wvf_ref), nsub - 1)


def _cmp_weights(w_cmp, pe):
    wcat = jnp.concatenate([w_cmp[:CMP_STRIDE], w_cmp[CMP_STRIDE:]], axis=2).astype(BF16)
    pe8 = jnp.broadcast_to(pe.reshape(1, CMP_BLK * HEAD_DIM), (8, CMP_BLK * HEAD_DIM))
    return wcat, pe8, w_cmp.reshape(CMP_BLK * HEAD_DIM, HEAD_DIM)


def compress_prompt(zf, b, t, wk, wv, gk):
    nsub = t // CMP_STRIDE
    wkc, pek, wkf = wk
    wvc, pev, wvf = wv
    full = lambda a: pl.BlockSpec(a.shape, lambda i, g: (0,) * a.ndim)
    head = lambda off: pl.BlockSpec((t, HEAD_DIM), lambda i, g: (i, off // HEAD_DIM + g))
    out = jax.ShapeDtypeStruct((b * N_KV_HEADS, nsub, HEAD_DIM), F32)
    return pl.pallas_call(
        _cmp_prompt_kernel,
        grid=(b, N_KV_HEADS),
        in_specs=[head(0), head(KV_W),
                  full(wkc), full(wvc), full(pek), full(pev), full(wkf), full(wvf), full(gk)],
        out_specs=[pl.BlockSpec((None, nsub, HEAD_DIM), lambda i, g: (i * N_KV_HEADS + g, 0, 0))] * 2,
        out_shape=[out, out],
        compiler_params=_cparams(("parallel", "parallel")),
        name="cmp_prompt",
    )(zf, zf, wkc, wvc, pek, pev, wkf, wvf, gk)


def _sel_matrix(nc_pad, nc, ns_pad, ns):
    c0 = np.arange(nc_pad) * CMP_STRIDE
    s0 = np.arange(ns_pad) * SEL_BLK
    m = (c0[:, None] < s0[None, :] + SEL_BLK) & (c0[:, None] + CMP_BLK > s0[None, :])
    m &= (np.arange(nc_pad)[:, None] < nc) & (np.arange(ns_pad)[None, :] < ns)
    return jnp.asarray(m.astype(np.float32))


EXP_C = SCALE * 1.4426950408889634


def _select_mask_t(imp_t, tpos_row, ns):
    blk = lax.broadcasted_iota(jnp.int32, imp_t.shape, 0)
    cur = tpos_row // SEL_BLK
    forced = (blk == 0) | (blk == cur) | (blk == cur - 1)
    impm = jnp.where(blk > cur, -BIG, jnp.where(forced, BIG, imp_t))
    rank = jnp.zeros(imp_t.shape, F32)
    for s2 in range(ns):
        row = impm[s2:s2 + 1, :]
        gt = jnp.where(row > impm, 1.0, 0.0)
        ge = jnp.where(row >= impm, 1.0, 0.0)
        rank = rank + jnp.where(blk > s2, ge, gt)
    return jnp.where(rank < float(min(N_SEL, ns)), 1.0, 0.0)


def _softmax_bias(s3, bias, valid):
    sm = s3 + bias[None]
    m = jnp.max(sm, axis=-1, keepdims=True)
    e = jnp.exp2((sm - m) * EXP_C)
    inv = 1.0 / jnp.sum(e, axis=-1, keepdims=True)
    if valid is not None:
        inv = jnp.where(valid[None], inv, 0.0)
    return e * inv


def _attn_prompt_kernel(q_ref, kc_ref, vc_ref, ks_ref, vs_ref, kw_ref, vw_ref, gn_ref, gate_ref,
                        msel_ref, e_ref, o_ref, *, tq, kvc, ns):
    g = pl.program_id(1)
    t0 = pl.program_id(2) * tq
    nh = Q_PER_KV
    rows = nh * tq
    q4 = jnp.concatenate([q_ref[:, h * HEAD_DIM:(h + 1) * HEAD_DIM] for h in range(nh)], axis=0)
    tpos = t0 + lax.broadcasted_iota(jnp.int32, (tq, 1), 0)
    tpos_row = t0 + lax.broadcasted_iota(jnp.int32, (1, tq), 1)

    ncp = kc_ref.shape[0]
    endpos = lax.broadcasted_iota(jnp.int32, (1, ncp), 1) * CMP_STRIDE + (CMP_BLK - 1)
    bias_c = jnp.where(endpos <= tpos, 0.0, NEG)
    p3 = _softmax_bias(_dot_nt(q4, kc_ref[...].astype(BF16)).reshape(nh, tq, ncp), bias_c,
                       tpos >= CMP_BLK - 1)
    o_cmp = _dot(p3.reshape(rows, ncp).astype(BF16), vc_ref[...].astype(BF16))
    imp = jnp.dot(jnp.sum(p3, axis=0), msel_ref[...], precision=lax.Precision.HIGHEST,
                  preferred_element_type=F32)
    ns8 = -(-ns // 8) * 8
    sel_t = _select_mask_t(imp.T[:ns8], tpos_row, ns)
    sel = jnp.concatenate([sel_t, jnp.zeros((imp.shape[1] - ns8, tq), F32)], axis=0).T.astype(BF16)

    def body(c, carry):
        m, l, acc = carry
        k0 = pl.multiple_of(c * kvc, kvc)
        kpos = k0 + lax.broadcasted_iota(jnp.int32, (1, kvc), 1)
        bias = jnp.where(kpos <= tpos, (_dot(sel, e_ref[c]) - 1.0) * -NEG, NEG)
        sm = _dot_nt(q4, ks_ref[pl.ds(k0, kvc), :]).reshape(nh, tq, kvc) + bias[None]
        m_new = jnp.maximum(m, jnp.max(sm, axis=-1, keepdims=True))
        alpha = jnp.exp2((m - m_new) * EXP_C)
        e = jnp.exp2((sm - m_new) * EXP_C)
        l = alpha * l + jnp.sum(e, axis=-1, keepdims=True)
        pv = _dot(e.reshape(rows, kvc).astype(BF16), vs_ref[pl.ds(k0, kvc), :])
        return m_new, l, alpha * acc + pv.reshape(nh, tq, HEAD_DIM)

    nch = (t0 + tq + kvc - 1) // kvc
    m, l, acc = lax.fori_loop(0, nch, body, (jnp.full((nh, tq, 1), NEG, F32), jnp.zeros((nh, tq, 1), F32),
                                             jnp.zeros((nh, tq, HEAD_DIM), F32)))
    o_sel = (acc * jnp.where(l > 0.0, 1.0 / l, 0.0)).reshape(rows, HEAD_DIM)

    nwin = WINDOW + tq
    w0 = pl.multiple_of(jnp.maximum(t0 - WINDOW, 0), tq)
    diff = tpos - (w0 + lax.broadcasted_iota(jnp.int32, (1, nwin), 1))
    bias_w = jnp.where((diff >= 0) & (diff <= WINDOW), 0.0, NEG)
    pw = _softmax_bias(_dot_nt(q4, kw_ref[pl.ds(w0, nwin), :]).reshape(nh, tq, nwin), bias_w, None)
    o_win = _dot(pw.reshape(rows, nwin).astype(BF16), vw_ref[pl.ds(w0, nwin), :])

    gs = _sigmoid(gn_ref[...])
    lane = lax.broadcasted_iota(jnp.int32, gs.shape, 1)
    for h in range(nh):
        r = slice(h * tq, (h + 1) * tq)
        o = None
        for br, ob in enumerate((o_cmp, o_sel, o_win)):
            cidx = br * N_HEADS + g * nh + h
            gcol = jnp.sum(jnp.where(lane == cidx, gs, 0.0), axis=-1, keepdims=True)
            o = gcol * ob[r] if o is None else o + gcol * ob[r]
        hs = slice(h * HEAD_DIM, (h + 1) * HEAD_DIM)
        o_ref[:, hs] = (o * gate_ref[:, hs].astype(F32)).astype(o_ref.dtype)


def attn_prompt(zq, zkv, kcmp, vcmp, gn, gate, b, t):
    tq, kvc = 256, 512
    assert t // CMP_STRIDE == LANES and t % kvc == 0 and t >= WINDOW + tq
    nq = t // tq
    ns = -(-t // SEL_BLK)
    nc = t // CMP_STRIDE - CMP_BLK // CMP_STRIDE + 1
    msel = _sel_matrix(LANES, nc, LANES, ns)
    kk = np.arange(t)
    e3 = (kk[None, :] // SEL_BLK == np.arange(LANES)[:, None]).astype(np.float32)
    e3 = jnp.asarray(e3.reshape(LANES, t // kvc, kvc).transpose(1, 0, 2), BF16)
    hb = lambda off: (lambda bi, g, i: (bi, off // HEAD_DIM + g))
    kv_spec = lambda off: pl.BlockSpec((t, HEAD_DIM), hb(off))
    cm_spec = pl.BlockSpec((None, t // CMP_STRIDE, HEAD_DIM), lambda bi, g, i: (bi * N_KV_HEADS + g, 0, 0))
    row4 = pl.BlockSpec((tq, Q_PER_KV * HEAD_DIM), lambda bi, g, i: (bi * nq + i, g))
    return pl.pallas_call(
        functools.partial(_attn_prompt_kernel, tq=tq, kvc=kvc, ns=ns),
        grid=(b, N_KV_HEADS, nq),
        in_specs=[row4, cm_spec, cm_spec,
                  kv_spec(2 * KV_W), kv_spec(3 * KV_W), kv_spec(4 * KV_W), kv_spec(5 * KV_W),
                  pl.BlockSpec((tq, LANES), lambda bi, g, i: (bi * nq + i, 0)),
                  row4,
                  pl.BlockSpec(msel.shape, lambda bi, g, i: (0, 0)),
                  pl.BlockSpec(e3.shape, lambda bi, g, i: (0, 0, 0))],
        out_specs=row4,
        out_shape=jax.ShapeDtypeStruct((b * t, NSA_W), BF16),
        compiler_params=_cparams(("parallel", "parallel", "arbitrary")),
        name="attn_prompt",
    )(zq, kcmp, vcmp, zkv, zkv, zkv, zkv, gn, gate, msel, e3)


def _layer_norm(v, g, b):
    vc = v - jnp.mean(v, axis=-1, keepdims=True)
    var = jnp.mean(vc * vc, axis=-1, keepdims=True)
    return vc * lax.rsqrt(var + EPS) * g + b


def _mlp_prompt_kernel(u_ref, v_ref, gate_ref, lg_ref, lb_ref, ws_ref, bst_ref, o_ref):
    tm = u_ref.shape[0]
    gw = ws_ref.shape[1]
    vb = _layer_norm(v_ref[...], lg_ref[...], lb_ref[...]).astype(BF16)
    row = lax.broadcasted_iota(jnp.int32, (CHUNK, CHUNK), 0)
    col = lax.broadcasted_iota(jnp.int32, (CHUNK, CHUNK), 1)
    for gi in range(MLP_GROUPS):
        cs = slice(gi * gw, (gi + 1) * gw)
        wsg = jnp.where(row >= col, ws_ref[gi], 0.0).astype(BF16)
        bias = bst_ref[:, gi:gi + 1]
        for ch in range(tm // CHUNK):
            rs = slice(ch * CHUNK, (ch + 1) * CHUNK)
            mixed = _dot(wsg, vb[rs, cs]) + bias
            o_ref[rs, cs] = (u_ref[rs, cs] * mixed * gate_ref[rs, cs].astype(F32)).astype(o_ref.dtype)


def mlp_prompt(z, cu, cg, ln_g, ln_b, w_s, b_s, tm):
    m = z.shape[0]
    w = ln_g.shape[0]
    assert w // MLP_GROUPS == CHUNK == w_s.shape[1]
    full = lambda a: pl.BlockSpec(a.shape, lambda i: (0,) * a.ndim)
    bst = b_s.T
    return pl.pallas_call(
        _mlp_prompt_kernel,
        grid=(m // tm,),
        in_specs=[pl.BlockSpec((tm, w), lambda i: (i, cu)), pl.BlockSpec((tm, w), lambda i: (i, cu + 1)),
                  pl.BlockSpec((tm, w), lambda i: (i, cg)),
                  pl.BlockSpec((1, w), lambda i: (0, 0)), pl.BlockSpec((1, w), lambda i: (0, 0)),
                  full(w_s), full(bst)],
        out_specs=pl.BlockSpec((tm, w), lambda i: (i, 0)),
        out_shape=jax.ShapeDtypeStruct((m, w), BF16),
        compiler_params=_cparams(("parallel",)),
        name="mlp_prompt",
    )(z, z, z, ln_g.reshape(1, w), ln_b.reshape(1, w), w_s, bst)


def _softplus(x):
    return jnp.maximum(x, 0.0) + jnp.log1p(jnp.exp(-jnp.abs(x)))


def _lru_gates(xc, wa_ref, wx_ref, ba, bx, lam):
    nh = xc.shape[1] // HEAD_DIM
    rs, is_ = [], []
    for hh in range(nh):
        xh = xc[:, hh * HEAD_DIM:(hh + 1) * HEAD_DIM].astype(BF16)
        rs.append(_dot(xh, wa_ref[hh]))
        is_.append(_dot(xh, wx_ref[hh]))
    r = _sigmoid(jnp.concatenate(rs, axis=1) + ba)
    i = _sigmoid(jnp.concatenate(is_, axis=1) + bx)
    log_a = -LRU_C * r * _softplus(-lam)
    th = jnp.tanh(log_a)
    return jnp.exp(log_a), jnp.sqrt(-2.0 * th / (1.0 - th)) * (i * xc)


SCAN_GROUP = 4


def _lru_prompt_kernel(x_ref, gate_ref, cw_ref, cb_ref, wa_ref, wx_ref, ba_ref, bx_ref, lam_ref,
                       o_ref, h_ref, a_scr, u_scr):
    t, wb = x_ref.shape
    x = x_ref[...]
    row = lax.broadcasted_iota(jnp.int32, (t, wb), 0)
    xc = cb_ref[...] + x * cw_ref[CONV_W - 1:CONV_W, :]
    for d in range(1, CONV_W):
        xs = jnp.where(row >= d, pltpu.roll(x, d, 0), 0.0)
        xc = xc + xs * cw_ref[CONV_W - 1 - d:CONV_W - d, :]
    a, u = _lru_gates(xc, wa_ref, wx_ref, ba_ref[...], bx_ref[...], lam_ref[...])
    a_scr[...] = a
    u_scr[...] = u
    row8 = lax.broadcasted_iota(jnp.int32, (8, wb), 0)

    def body(bi, h):
        r0 = pl.multiple_of(bi * (8 * SCAN_GROUP), 8 * SCAN_GROUP)
        tiles = []
        for k in range(SCAN_GROUP):
            a8 = a_scr[pl.ds(r0 + 8 * k, 8), :]
            u8 = u_scr[pl.ds(r0 + 8 * k, 8), :]
            for d in (1, 2, 4):
                a_sh = jnp.where(row8 >= d, pltpu.roll(a8, d, 0), 1.0)
                u_sh = jnp.where(row8 >= d, pltpu.roll(u8, d, 0), 0.0)
                u8 = a8 * u_sh + u8
                a8 = a8 * a_sh
            tiles.append((a8, u8))
        for k, (a8, u8) in enumerate(tiles):
            h8 = a8 * h + u8
            u_scr[pl.ds(r0 + 8 * k, 8), :] = h8
            h = h8[7:8, :]
        return h

    h = lax.fori_loop(0, t // (8 * SCAN_GROUP), body, jnp.zeros((1, wb), F32))
    h_ref[...] = h
    o_ref[...] = (u_scr[...] * gate_ref[...].astype(F32)).astype(o_ref.dtype)


def lru_prompt(z, cx, cg, conv_w, conv_b, wa, wx, ba, bx, lam, b, t):
    w = lam.shape[0]
    wb = 512
    nh = wb // HEAD_DIM
    assert w % wb == 0 and t % (8 * SCAN_GROUP) == 0 and wa.shape[1] == HEAD_DIM
    vec = lambda a: a.reshape(1, w)
    vspec = pl.BlockSpec((1, wb), lambda i, j: (0, j))
    blk = pl.BlockSpec((t, wb), lambda i, j: (i, j))
    zblk = lambda c0: pl.BlockSpec((t, wb), lambda i, j: (i, c0 + j))
    hspec = pl.BlockSpec((nh, HEAD_DIM, HEAD_DIM), lambda i, j: (j, 0, 0))
    return pl.pallas_call(
        _lru_prompt_kernel,
        grid=(b, w // wb),
        in_specs=[zblk(cx), zblk(cg), pl.BlockSpec((CONV_W, wb), lambda i, j: (0, j)), vspec,
                  hspec, hspec, vspec, vspec, vspec],
        out_specs=[blk, pl.BlockSpec((None, 1, wb), lambda i, j: (i, 0, j))],
        out_shape=[jax.ShapeDtypeStruct((b * t, w), BF16), jax.ShapeDtypeStruct((b, 1, w), F32)],
        scratch_shapes=[pltpu.VMEM((t, wb), F32), pltpu.VMEM((t, wb), F32)],
        compiler_params=_cparams(("parallel", "parallel")),
        name="lru_prompt",
    )(z, z, conv_w, vec(conv_b), wa.astype(BF16), wx.astype(BF16), vec(ba), vec(bx), vec(lam))


ATT_W = NSA_W + 6 * KV_W
SEG = 512


PACK_TAIL = 64


def _pack_kernel(a_ref, b_ref, o_ref, *, gn_tile, ngate):
    j = pl.program_id(0)

    @pl.when(j < gn_tile)
    def _():
        o_ref[...] = a_ref[...].T.astype(o_ref.dtype)

    @pl.when(j == gn_tile)
    def _():
        row = lax.broadcasted_iota(jnp.int32, a_ref.shape, 0)
        o_ref[...] = jnp.where(row < ngate, a_ref[...], 0.0).T.astype(o_ref.dtype)

    @pl.when(j > gn_tile)
    def _():
        o_ref[...] = jnp.concatenate([a_ref[ngate:, :], b_ref[:ngate, :]], axis=0).T.astype(o_ref.dtype)


def _pack_w_in(w_in, l, d_model, mlp_w, lru_w):
    ngate = N_BRANCH * N_HEADS
    sizes = (("att", ATT_W), ("gn", SEG), ("gate_nsa", NSA_W), ("uv", 2 * mlp_w), ("gate_mlp", mlp_w),
             ("xl", lru_w), ("gate_lru", lru_w), ("gm", N_BRANCH * d_model))
    offs, acc = {}, 0
    for name, size in sizes:
        offs[name] = (acc, size)
        acc += size
    wt = jnp.swapaxes(w_in, 1, 2)
    n_in, k = wt.shape[1:]
    assert acc == n_in - ngate + SEG and ATT_W % SEG == 0 and ngate <= PACK_TAIL and ngate % 16 == 0
    gn_tile = ATT_W // SEG
    w = pl.pallas_call(
        functools.partial(_pack_kernel, gn_tile=gn_tile, ngate=ngate),
        grid=(acc // SEG,),
        in_specs=[pl.BlockSpec((None, SEG, k), lambda j: (l, jnp.where(j > gn_tile, j - 1, j), 0)),
                  pl.BlockSpec((None, PACK_TAIL, k), lambda j: (l, (SEG // PACK_TAIL) * j, 0))],
        out_specs=pl.BlockSpec((k, SEG), lambda j: (0, j)),
        out_shape=jax.ShapeDtypeStruct((k, acc), BF16),
        compiler_params=_cparams(("parallel",)),
        name="pack_w",
    )(wt, wt)
    return w, offs


def _norm_vectors(q_g, k_g):
    one = jnp.ones((KV_W,), F32)
    zero = jnp.zeros((KV_W,), F32)
    gain = jnp.concatenate([jnp.tile(q_g, N_HEADS), one, one, jnp.tile(k_g[1], N_KV_HEADS), one,
                            jnp.tile(k_g[2], N_KV_HEADS), one])
    flag = jnp.concatenate([jnp.ones((NSA_W,), F32), zero, zero, one, zero, one, zero])
    return gain.reshape(1, ATT_W), flag.reshape(1, ATT_W)


MID_SEGMENTS = (("gate_nsa", "silu"), ("uv", "gelu"), ("gate_mlp", "silu"), ("xl", "none"), ("gate_lru", "silu"))


def _in_proj(x2d, xs2d, lw, tm):
    h = rms_rows(x2d, lw["norm_g"], min(tm, 512))
    hs = rms_rows(xs2d, lw["norm_g"], xs2d.shape[0])
    w, offs = lw["w_pack"]
    p = lambda name, act, dt, tn=COL_TILE, n=None, off=0, tm=tm, **kw: proj(
        h, hs, w, offs[name][0] + off, n or offs[name][1], act, dt if isinstance(dt, tuple) else (dt,),
        tm, tn, **kw)
    gain, flag = lw["gain"], lw["flag"]
    zq, zqs = p("att", "norm", BF16, n=NSA_W, gain=gain[:, :NSA_W], flag=flag[:, :NSA_W])
    kvf, kvb, kvh, kvs = p("att", "norm", (F32, BF16, "heads"), n=ATT_W - NSA_W, off=NSA_W, tn=SEG,
                           gain=gain[:, NSA_W:], flag=flag[:, NSA_W:])
    z, zs = dict(q=zq, kvf=kvf, kvb=kvb, kvh=kvh), dict(q=zqs, kv=kvs)
    z["gn"], zs["gn"] = p("gn", "none", F32, tn=LANES, n=LANES)
    z["gm"], zs["gm"] = p("gm", "sigmoid", BF16)
    mid0 = offs["gate_nsa"][0]
    ranges, mid = [], {}
    for name, act in MID_SEGMENTS:
        lo = offs[name][0] - mid0
        ranges.append((act, lo // COL_TILE, (lo + offs[name][1]) // COL_TILE))
        mid[name] = (lo, offs[name][1])
    z["mid"], zs["mid"] = p("gate_nsa", "mixed", F32, n=sum(n for _, n in mid.values()), ranges=tuple(ranges))
    z["mid_off"] = zs["mid_off"] = mid
    return z, zs


ROW_TILE = 1024
COL_TILE = 512


def prompt_layer(x2d, z, lw, l, b, t, w_buf):
    tm = ROW_TILE
    kcmp, vcmp = compress_prompt(z["kvf"], b, t, lw["cmp_k"], lw["cmp_v"], lw["k_norm_g"][0:1])
    mid, off = z["mid"], z["mid_off"]
    assert off["gate_nsa"][0] == 0
    o_nsa = attn_prompt(z["q"], z["kvb"], kcmp, vcmp, z["gn"], mid, b, t)
    mw = off["gate_mlp"][1]
    o_mlp = mlp_prompt(mid, off["uv"][0] // mw, off["gate_mlp"][0] // mw, lw["mlp_ln_g"], lw["mlp_ln_b"],
                       lw["w_s"], lw["b_s"], 512)
    o_lru, h_last = lru_prompt(mid, off["xl"][0] // 512, off["gate_lru"][0] // 512, lw["conv_w"], lw["conv_b"],
                               lw["lru_wa"], lw["lru_wx"], lw["lru_ba"], lw["lru_bx"], lw["lru_lambda"], b, t)
    m = merge([o_nsa, o_mlp, o_lru], lw["w_br"], l, z["gm"], tm, COL_TILE)
    y = out_proj(m, lw["w_out"], l, x2d, tm, COL_TILE)
    kv = lambda i: z["kvh"][i].reshape(b, t, N_KV_HEADS, HEAD_DIM)
    xl = mid.reshape(b, t, -1)[:, t - (CONV_W - 1):, off["xl"][0]:off["xl"][0] + off["xl"][1]]
    assert t >= w_buf and t >= CONV_W - 1
    state = dict(cmp_k=kv(0), cmp_v=kv(1), sel_k=kv(2), sel_v=kv(3),
                 win_k=kv(4)[:, t - w_buf:], win_v=kv(5)[:, t - w_buf:],
                 lru_h=h_last.reshape(b, -1), lru_conv=xl)
    return y, state


PAGES_PER_STEP = 32


def _cmp_paged_kernel(pt_ref, *refs, pg):
    k_refs, v_refs = refs[:pg], refs[pg:2 * pg]
    wk_ref, wv_ref, ko_ref, vo_ref, r_scr = refs[2 * pg:]
    rows = k_refs[0].shape[0]
    sub = N_KV_HEADS * CMP_STRIDE
    nsp = rows // sub
    low = lax.broadcasted_iota(jnp.int32, (8, HEAD_DIM), 0) < N_KV_HEADS
    for page_refs, w_ref, o_ref in ((k_refs, wk_ref, ko_ref), (v_refs, wv_ref, vo_ref)):
        blocks = []
        for r in page_refs:
            for ip in range(nsp // 2):
                cols = []
                for m in range(sub // 8):
                    ta = r[pl.ds(sub * (2 * ip) + 8 * m, 8), :]
                    tb = r[pl.ds(sub * (2 * ip + 1) + 8 * m, 8), :]
                    cols.append(jnp.where(low, ta, pltpu.roll(tb, N_KV_HEADS, 0)))
                    cols.append(jnp.where(low, pltpu.roll(ta, N_KV_HEADS, 0), tb))
                blocks.append(jnp.concatenate(cols, axis=1))
        res = _dot(jnp.concatenate(blocks, axis=0).astype(BF16), w_ref[...])
        n = res.shape[0]
        r_scr[0] = res[:, :HEAD_DIM]
        r_scr[1] = res[:, HEAD_DIM:]
        for g in range(N_KV_HEADS):
            o_ref[g] = jnp.concatenate([r_scr[0, pl.ds(g, n // N_KV_HEADS, stride=N_KV_HEADS), :],
                                        r_scr[1, pl.ds(g, n // N_KV_HEADS, stride=N_KV_HEADS), :]], axis=1)


def _cmp_paged_weights(wcat):
    return wcat.reshape(CMP_STRIDE * HEAD_DIM, 2 * HEAD_DIM)


def compress_paged(pool_k, pool_v, page_table, wkc, wvc, page0):
    b, n_pages = page_table.shape
    rows = pool_k.shape[1]
    pg = PAGES_PER_STEP
    nsp = rows // N_KV_HEADS // CMP_STRIDE
    assert n_pages % pg == 0 and N_KV_HEADS * 2 == 8 and nsp % 2 == 0
    wkc, wvc = _cmp_paged_weights(wkc), _cmp_paged_weights(wvc)
    page_spec = lambda p: pl.BlockSpec((None, rows, HEAD_DIM), lambda i, c, pt: (page0 + pt[i, c * pg + p], 0, 0))
    full = lambda a: pl.BlockSpec(a.shape, lambda i, c, pt: (0,) * a.ndim)
    out = jax.ShapeDtypeStruct((b, N_KV_HEADS, n_pages * nsp, 2 * HEAD_DIM), F32)
    ospec = pl.BlockSpec((None, N_KV_HEADS, pg * nsp, 2 * HEAD_DIM), lambda i, c, pt: (i, 0, c, 0))
    return pl.pallas_call(
        functools.partial(_cmp_paged_kernel, pg=pg),
        grid_spec=pltpu.PrefetchScalarGridSpec(
            num_scalar_prefetch=1,
            grid=(b, n_pages // pg),
            in_specs=[page_spec(p) for p in range(pg)] * 2 + [full(wkc), full(wvc)],
            out_specs=[ospec, ospec],
            scratch_shapes=[pltpu.VMEM((2, pg * nsp * N_KV_HEADS, HEAD_DIM), F32)]),
        out_shape=[out, out],
        compiler_params=_cparams(("parallel", "arbitrary")),
        name="cmp_paged",
    )(page_table, *([pool_k] * pg), *([pool_v] * pg), wkc, wvc)


def _cmp_sample_kernel(tbk_ref, tbv_ref, q_ref, pek_ref, pev_ref, wkf_ref, wvf_ref, gk_ref, msel_ref,
                       o_ref, idx_ref, *, tpos, nc, ns):
    ck = _pe_const(pek_ref, wkf_ref)
    cv = _pe_const(pev_ref, wvf_ref)
    q = q_ref[...].astype(BF16)
    nh = q.shape[0]
    nsub = tbk_ref.shape[1]
    hrow = lax.broadcasted_iota(jnp.int32, (nh, 1), 0)
    blk = lax.broadcasted_iota(jnp.int32, (1, nsub), 1)
    cmask = (blk * CMP_STRIDE + (CMP_BLK - 1) <= tpos) & (blk < nc)
    row8 = lax.broadcasted_iota(jnp.int32, (8, nsub), 0)
    o = jnp.zeros((nh, HEAD_DIM), F32)
    psum = jnp.zeros((8, nsub), F32)
    for g in range(N_KV_HEADS):
        kc = _compress_combine(tbk_ref[g], ck, nc)
        ms = jnp.mean(kc * kc, axis=-1, keepdims=True)
        kc = (kc * lax.rsqrt(ms + EPS)) * gk_ref[...]
        vc = _compress_combine(tbv_ref[g], cv, nc)
        p = _softmax_rows(_dot_nt(q, kc.astype(BF16)) * SCALE, cmask & (hrow // Q_PER_KV == g))
        o = o + _dot(p.astype(BF16), vc.astype(BF16))
        psum = jnp.where(row8 == g, jnp.sum(p, axis=0, keepdims=True), psum)
    o_ref[...] = o
    imp = jnp.dot(psum, msel_ref[...], precision=lax.Precision.HIGHEST, preferred_element_type=F32)
    lane = lax.broadcasted_iota(jnp.int32, imp.shape, 1)
    cur = tpos // SEL_BLK
    forced = (lane == 0) | (lane == cur) | (lane == cur - 1)
    impm = jnp.where(lane > cur, -BIG, jnp.where(forced, BIG, imp))
    impm = jnp.where(lane < ns, impm, -jnp.inf)
    lane_f = lane.astype(F32)
    out_lane = lax.broadcasted_iota(jnp.int32, idx_ref.shape, 1)
    idxs = jnp.zeros(idx_ref.shape, F32)
    for j in range(min(N_SEL, ns)):
        mx = jnp.max(impm, axis=-1, keepdims=True)
        am = jnp.min(jnp.where(impm == mx, lane_f, 1e9), axis=-1, keepdims=True)
        idxs = jnp.where(out_lane == j, am, idxs)
        impm = jnp.where(lane_f == am, -jnp.inf, impm)
    idx_ref[...] = idxs.astype(jnp.int32)


def cmp_sample(tbk, tbv, q3, ck, cv, gk, tpos):
    b, nkv, nsub, _ = tbk.shape
    nc = (tpos + 1) // CMP_STRIDE - CMP_BLK // CMP_STRIDE + 1
    ns = -(-(tpos + 1) // SEL_BLK)
    ns_pad = -(-ns // LANES) * LANES
    assert nc <= nsub
    msel = _sel_matrix(nsub, nc, ns_pad, ns)
    _, pek, wkf = ck
    _, pev, wvf = cv
    full = lambda a: pl.BlockSpec(a.shape, lambda i: (0,) * a.ndim)
    tb_spec = pl.BlockSpec((None, nkv, nsub, 2 * HEAD_DIM), lambda i: (i, 0, 0, 0))
    return pl.pallas_call(
        functools.partial(_cmp_sample_kernel, tpos=tpos, nc=nc, ns=ns),
        grid=(b,),
        in_specs=[tb_spec, tb_spec, pl.BlockSpec((None, N_HEADS, HEAD_DIM), lambda i: (i, 0, 0)),
                  full(pek), full(pev), full(wkf), full(wvf), full(gk), full(msel)],
        out_specs=[pl.BlockSpec((None, N_HEADS, HEAD_DIM), lambda i: (i, 0, 0)),
                   pl.BlockSpec((None, 8, LANES), lambda i: (i, 0, 0))],
        out_shape=[jax.ShapeDtypeStruct((b, N_HEADS, HEAD_DIM), F32),
                   jax.ShapeDtypeStruct((b, 8, LANES), jnp.int32)],
        compiler_params=_cparams(("parallel",)),
        name="cmp_sample",
    )(tbk, tbv, q3, pek, pev, wkf, wvf, gk, msel)


def _attend_with_new(q, k_all, v_all, mask, k_new, v_new, new_ok):
    s = jnp.where(mask, _dot_nt(q.astype(BF16), k_all) * SCALE, NEG)
    s_new = jnp.where(new_ok, jnp.sum(q * k_new, axis=-1, keepdims=True) * SCALE, NEG)
    m = jnp.maximum(jnp.max(s, axis=-1, keepdims=True), s_new)
    e = jnp.where(mask, jnp.exp(s - m), 0.0)
    e_new = jnp.where(new_ok, jnp.exp(s_new - m), 0.0)
    d = jnp.sum(e, axis=-1, keepdims=True) + e_new
    o = _dot(e.astype(BF16), v_all) + e_new * v_new
    return o / jnp.where(d > 0.0, d, 1.0)


def _attn_sample_kernel(pt_ref, idx_ref, *refs, nblk, tpos, past_len, w_buf):
    k_refs, v_refs = refs[:nblk], refs[nblk:2 * nblk]
    (q_ref, ocmp_ref, kw_ref, vw_ref, ksn_ref, vsn_ref, kwn_ref, vwn_ref, gn_ref, gate_ref,
     o_ref) = refs[2 * nblk:]
    b = pl.program_id(0)
    g = pl.program_id(1)
    q = q_ref[...]
    rb = k_refs[0].shape[0]
    r = lax.broadcasted_iota(jnp.int32, (1, rb), 1)
    tok, hd = r // N_KV_HEADS, r % N_KV_HEADS
    masks = []
    new_sel = False
    for j in range(nblk):
        s = idx_ref[b, g * nblk + j]
        kpos = s * SEL_BLK + tok
        masks.append((hd == g) & (kpos <= tpos) & (kpos < past_len))
        new_sel = new_sel | (s == past_len // SEL_BLK)
    k_all = jnp.concatenate([kr[...].astype(BF16) for kr in k_refs], axis=0)
    v_all = jnp.concatenate([vr[...].astype(BF16) for vr in v_refs], axis=0)
    o_sel = _attend_with_new(q, k_all, v_all, jnp.concatenate(masks, axis=1),
                             ksn_ref[...], vsn_ref[...], new_sel & (past_len <= tpos))
    rw = lax.broadcasted_iota(jnp.int32, (1, kw_ref.shape[0]), 1)
    diff = tpos - (past_len - w_buf + rw // N_KV_HEADS)
    wmask = (rw % N_KV_HEADS == g) & (diff >= 0) & (diff <= WINDOW)
    o_win = _attend_with_new(q, kw_ref[...].astype(BF16), vw_ref[...].astype(BF16), wmask,
                             kwn_ref[...], vwn_ref[...], tpos - past_len <= WINDOW)
    gs = _sigmoid(gn_ref[...])
    o = gs[:, 0:1] * ocmp_ref[...] + gs[:, 1:2] * o_sel + gs[:, 2:3] * o_win
    o_ref[...] = o * gate_ref[...]


def attn_sample(pool_k, pool_v, page_table, idx, q4, ocmp4, win_k, win_v, ks_new, vs_new, kw_new, vw_new,
                gn4, gate4, past_len, w_buf, page_size, page0, seq0):
    b, n_pages = page_table.shape
    nblk = idx.shape[1] // N_KV_HEADS
    rb = pool_k.shape[1]
    halves = page_size // SEL_BLK
    tpos = past_len

    def blk_map(j):
        def f(i, g, pt, ix):
            s = ix[i, g * nblk + j]
            page = pt[i, jnp.minimum(s // halves, n_pages - 1)]
            return ((page0 + page) * halves + s % halves, 0, 0)
        return f

    blk_spec = lambda j: pl.BlockSpec((None, rb, HEAD_DIM), blk_map(j))
    per_bg = lambda a: pl.BlockSpec((None, None) + a.shape[2:], lambda i, g, pt, ix: (i, g, 0, 0))
    per_b = lambda a: pl.BlockSpec((None,) + a.shape[1:], lambda i, g, pt, ix: (seq0 + i, 0, 0))
    small = (q4, ocmp4)
    news = (ks_new, vs_new, kw_new, vw_new, gn4, gate4)
    return pl.pallas_call(
        functools.partial(_attn_sample_kernel, nblk=nblk, tpos=tpos, past_len=past_len, w_buf=w_buf),
        grid_spec=pltpu.PrefetchScalarGridSpec(
            num_scalar_prefetch=2,
            grid=(b, N_KV_HEADS),
            in_specs=[blk_spec(j) for j in range(nblk)] * 2 + [per_bg(a) for a in small]
            + [per_b(win_k), per_b(win_v)] + [per_bg(a) for a in news],
            out_specs=per_bg(q4)),
        out_shape=jax.ShapeDtypeStruct(q4.shape, F32),
        compiler_params=_cparams(("parallel", "arbitrary")),
        name="attn_sample",
    )(page_table, idx, *([pool_k] * nblk), *([pool_v] * nblk), q4, ocmp4, win_k, win_v, *news)


def _point_sample_kernel(u_ref, v_ref, gmlp_ref, lg_ref, lb_ref, ws0_ref, bs0_ref,
                         x_ref, glru_ref, buf_ref, h0_ref, cw_ref, cb_ref, wa_ref, wx_ref, ba_ref, bx_ref,
                         lam_ref, omlp_ref, vrow_ref, olru_ref, hnew_ref):
    vn = _layer_norm(v_ref[...], lg_ref[...], lb_ref[...])
    vrow_ref[...] = vn
    mixed = ws0_ref[...] * vn + bs0_ref[...]
    omlp_ref[...] = (u_ref[...] * mixed * gmlp_ref[...].astype(F32)).astype(omlp_ref.dtype)
    x = x_ref[...]
    xc = cb_ref[...] + x * cw_ref[CONV_W - 1:CONV_W, :]
    for j in range(CONV_W - 1):
        xc = xc + buf_ref[j] * cw_ref[j:j + 1, :]
    a, u = _lru_gates(xc, wa_ref, wx_ref, ba_ref[...], bx_ref[...], lam_ref[...])
    h = a * h0_ref[...] + u
    hnew_ref[...] = h
    olru_ref[...] = (h * glru_ref[...].astype(F32)).astype(olru_ref.dtype)


def point_sample(uv, gate_mlp, xl, gate_lru, buf_t, h0, lw):
    b, w2 = uv.shape
    w = w2 // 2
    gw = w // MLP_GROUPS
    vec = lambda a: a.reshape(1, -1)
    ws0 = vec(jnp.repeat(lw["w_s"][:, 0, 0], gw))
    bs0 = vec(jnp.repeat(lw["b_s"][:, 0], gw))
    args = (uv[:, :w], uv[:, w:], gate_mlp, vec(lw["mlp_ln_g"]), vec(lw["mlp_ln_b"]), ws0, bs0,
            xl, gate_lru, buf_t, h0, lw["conv_w"], vec(lw["conv_b"]), lw["lru_wa"].astype(BF16),
            lw["lru_wx"].astype(BF16), vec(lw["lru_ba"]), vec(lw["lru_bx"]), vec(lw["lru_lambda"]))
    lw_ = xl.shape[1]
    return pl.pallas_call(
        _point_sample_kernel,
        out_shape=[jax.ShapeDtypeStruct((b, w), BF16), jax.ShapeDtypeStruct((b, w), F32),
                   jax.ShapeDtypeStruct((b, lw_), BF16), jax.ShapeDtypeStruct((b, lw_), F32)],
        compiler_params=pltpu.CompilerParams(vmem_limit_bytes=VMEM_LIMIT),
        name="point_sample",
    )(*args)


def sample_layer(x2d, z, lw, l, caches, page_table, past_len, w_buf):
    b = x2d.shape[0]
    seg = lambda i: z["kv"][:, i * KV_W:(i + 1) * KV_W]
    kvh = lambda a: a.reshape(b, N_KV_HEADS, 1, HEAD_DIM)
    n_pool, page_size = caches["cmp_k"].shape[1:3]
    page0 = l * n_pool
    pool3 = lambda a: a.reshape(-1, page_size * N_KV_HEADS, HEAD_DIM)
    halves = lambda a: a.reshape(-1, SEL_BLK * N_KV_HEADS, HEAD_DIM)
    tbk, tbv = compress_paged(pool3(caches["cmp_k"]), pool3(caches["cmp_v"]), page_table,
                              lw["cmp_k"][0], lw["cmp_v"][0], page0)
    q3 = z["q"].reshape(b, N_HEADS, HEAD_DIM)
    o_cmp, idx = cmp_sample(tbk, tbv, q3, lw["cmp_k"], lw["cmp_v"], lw["k_norm_g"][0:1], past_len)
    n_sel = min(N_SEL, -(-(past_len + 1) // SEL_BLK))
    idx = idx[:, :N_KV_HEADS, :n_sel].reshape(b, N_KV_HEADS * n_sel)
    four = lambda a: a.reshape(b, N_KV_HEADS, Q_PER_KV, -1)
    mseg = lambda name: z["mid"][:, z["mid_off"][name][0]:z["mid_off"][name][0] + z["mid_off"][name][1]]
    gn4 = z["gn"][:, :N_BRANCH * N_HEADS].reshape(b, N_BRANCH, N_KV_HEADS, Q_PER_KV).transpose(0, 2, 3, 1)
    win3 = lambda a: a.reshape(-1, w_buf * N_KV_HEADS, HEAD_DIM)
    o_nsa = attn_sample(halves(caches["sel_k"]), halves(caches["sel_v"]), page_table, idx,
                        four(q3), four(o_cmp), win3(caches["win_k"]), win3(caches["win_v"]),
                        kvh(seg(2)), kvh(seg(3)), kvh(seg(4)), kvh(seg(5)),
                        gn4, four(mseg("gate_nsa")), past_len, w_buf, page_size, page0, l * b)
    o_mlp, v_rows, o_lru, h_new = point_sample(
        mseg("uv"), mseg("gate_mlp"), mseg("xl"), mseg("gate_lru"), caches["lru_conv"].transpose(1, 0, 2),
        caches["lru_h"].astype(F32), lw)
    m = merge([o_nsa.reshape(b, NSA_W).astype(BF16), o_mlp, o_lru], lw["w_br"], l, z["gm"], b, SEG)
    y = out_proj(m, lw["w_out"], l, x2d, b, SEG)
    tok = lambda a: a.reshape(b, 1, N_KV_HEADS, HEAD_DIM)
    state = dict(cmp_k=tok(seg(0)), cmp_v=tok(seg(1)), sel_k=tok(seg(2)), sel_v=tok(seg(3)),
                 win_k=jnp.concatenate([caches["win_k"][l], tok(seg(4))], axis=1)[:, -w_buf:],
                 win_v=jnp.concatenate([caches["win_v"][l], tok(seg(5))], axis=1)[:, -w_buf:],
                 lru_h=h_new, lru_conv=jnp.concatenate([caches["lru_conv"], mseg("xl")[:, None]], axis=1)[:, 1:],
                 mlp_v=v_rows[:, None])
    return y, state


def _layer_weights(l, p):
    d_model = p["w_in"].shape[1]
    mlp_w = p["mlp_ln_g"].shape[1]
    lru_w = p["lru_lambda"].shape[1]
    names = ("norm_g", "q_norm_g", "k_norm_g", "mlp_ln_g", "mlp_ln_b", "w_s", "b_s", "conv_w", "conv_b",
             "lru_wa", "lru_ba", "lru_wx", "lru_bx", "lru_lambda")
    lw = {n: p[n][l] for n in names}
    lw["w_pack"] = _pack_w_in(p["w_in"], l, d_model, mlp_w, lru_w)
    lw["gain"], lw["flag"] = _norm_vectors(p["q_norm_g"][l], p["k_norm_g"][l])
    lw["cmp_k"] = _cmp_weights(p["w_cmp_k"][l], p["cmp_pe_k"][l])
    lw["cmp_v"] = _cmp_weights(p["w_cmp_v"][l], p["cmp_pe_v"][l])
    lw["w_br"] = [p[n] for n in ("w_br_nsa_bf", "w_br_mlp_bf", "w_br_lru_bf")]
    lw["w_out"] = p["w_out_bf"]
    return lw


def kernel(x_prompt, x_sample, cache_cmp_k, cache_cmp_v, cache_sel_k, cache_sel_v, state_win_k, state_win_v,
           state_lru_h, state_lru_conv, page_table, norm_g, w_in, q_norm_g, k_norm_g, cmp_pe_k, cmp_pe_v,
           w_cmp_k, w_cmp_v, mlp_ln_g, mlp_ln_b, w_s, b_s, conv_w, conv_b, lru_wa, lru_ba, lru_wx, lru_bx,
           lru_lambda, w_br_nsa, w_br_mlp, w_br_lru, w_out):
    params = dict(norm_g=norm_g, w_in=w_in, q_norm_g=q_norm_g, k_norm_g=k_norm_g, cmp_pe_k=cmp_pe_k,
                  cmp_pe_v=cmp_pe_v, w_cmp_k=w_cmp_k, w_cmp_v=w_cmp_v, mlp_ln_g=mlp_ln_g, mlp_ln_b=mlp_ln_b,
                  w_s=w_s, b_s=b_s, conv_w=conv_w, conv_b=conv_b, lru_wa=lru_wa, lru_ba=lru_ba, lru_wx=lru_wx,
                  lru_bx=lru_bx, lru_lambda=lru_lambda, w_br_nsa=w_br_nsa, w_br_mlp=w_br_mlp,
                  w_br_lru=w_br_lru, w_out=w_out)
    depth = w_in.shape[0]
    b, t, d = x_prompt.shape
    bs, ts, _ = x_sample.shape
    assert ts == 1
    w_buf = state_win_k.shape[2]
    past_len = page_table.shape[1] * cache_cmp_k.shape[2]
    yp = x_prompt.reshape(b * t, d)
    ys = x_sample.reshape(bs * ts, d)
    p_st, s_st = [], []
    for n in ("w_br_nsa", "w_br_mlp", "w_br_lru", "w_out"):
        params[n + "_bf"] = params[n].astype(BF16)
    for l in range(depth):
        lw = _layer_weights(l, params)
        z, zs = _in_proj(yp, ys, lw, ROW_TILE)
        yp, sp = prompt_layer(yp, z, lw, l, b, t, w_buf)
        caches = dict(cmp_k=cache_cmp_k, cmp_v=cache_cmp_v, sel_k=cache_sel_k, sel_v=cache_sel_v,
                      win_k=state_win_k, win_v=state_win_v, lru_h=state_lru_h[l],
                      lru_conv=state_lru_conv[l])
        ys, ss = sample_layer(ys, zs, lw, l, caches, page_table, past_len, w_buf)
        p_st.append(sp)
        s_st.append(ss)
    stk = lambda sts, name: jnp.stack([st[name] for st in sts])
    names = ("cmp_k", "cmp_v", "sel_k", "sel_v", "win_k", "win_v", "lru_h", "lru_conv")
    return ((yp.reshape(b, t, d), ys.reshape(bs, ts, d))
            + tuple(stk(p_st, n) for n in names)
            + tuple(stk(s_st, n) for n in names + ("mlp_v",)))
```

```python
import functools

import numpy as np
import jax
import jax.numpy as jnp
from jax import lax
from jax.experimental import pallas as pl
from jax.experimental.pallas import tpu as pltpu

N_HEADS = 16
HEAD_DIM = 128
N_KV_HEADS = 4
Q_PER_KV = N_HEADS // N_KV_HEADS
NSA_W = N_HEADS * HEAD_DIM
KV_W = N_KV_HEADS * HEAD_DIM
CMP_BLK = 32
CMP_STRIDE = 16
SEL_BLK = 64
N_SEL = 16
WINDOW = 512
CHUNK = 128
MLP_GROUPS = 8
LRU_HEADS = 8
CONV_W = 4
LRU_C = 8.0
N_BRANCH = 3
EPS = 1e-6
BIG = 1e9
NEG = -1e30
SCALE = HEAD_DIM ** -0.5

LANES = 128
VMEM_LIMIT = 56 * 1024 * 1024

BF16 = jnp.bfloat16
F32 = jnp.float32


def _cparams(sem):
    return pltpu.CompilerParams(dimension_semantics=sem, vmem_limit_bytes=VMEM_LIMIT)


def _gelu(x):
    return 0.5 * x * (1.0 + jnp.tanh(0.7978845608028654 * (x + 0.044715 * (x * x * x))))


def _sigmoid(x):
    return 1.0 / (1.0 + jnp.exp(-x))


def _silu(x):
    return x * _sigmoid(x)


def _dot(a, b):
    return jnp.dot(a, b, preferred_element_type=F32)


def _dot_nt(a, b):
    return lax.dot_general(a, b, (((1,), (1,)), ((), ())), preferred_element_type=F32)


def _rms_kernel(x_ref, g_ref, o_ref):
    x = x_ref[...]
    ms = jnp.mean(x * x, axis=-1, keepdims=True)
    o_ref[...] = ((x * lax.rsqrt(ms + EPS)) * g_ref[...]).astype(o_ref.dtype)


def rms_rows(x, g, tm):
    m, d = x.shape
    return pl.pallas_call(
        _rms_kernel,
        grid=(m // tm,),
        in_specs=[pl.BlockSpec((tm, d), lambda i: (i, 0)),
                  pl.BlockSpec((1, d), lambda i: (0, 0))],
        out_specs=pl.BlockSpec((tm, d), lambda i: (i, 0)),
        out_shape=jax.ShapeDtypeStruct((m, d), BF16),
        compiler_params=_cparams(("parallel",)),
        name="rms_rows",
    )(x, g.reshape(1, d))


W_BUFFERS = 2


def _row_tile(shape):
    return pl.BlockSpec(shape, lambda i, j: (i, 0))


def _proj_act(acc, act, gain_ref, flag_ref):
    if act == "silu":
        return _silu(acc)
    if act == "gelu":
        return _gelu(acc)
    if act == "sigmoid":
        return _sigmoid(acc)
    if act == "norm":
        outs = []
        for c in range(acc.shape[1] // LANES):
            sl = slice(c * LANES, (c + 1) * LANES)
            z = acc[:, sl]
            ms = jnp.mean(z * z, axis=-1, keepdims=True)
            zn = (z * lax.rsqrt(ms + EPS)) * gain_ref[:, sl]
            outs.append(jnp.where(flag_ref[:, sl] > 0.5, zn, z))
        return jnp.concatenate(outs, axis=1)
    return acc


def _proj_kernel(a_ref, a2_ref, w_ref, *rest, act, outs, ranges):
    if act == "mixed":
        o_ref, o2_ref = rest
        j = pl.program_id(1)
        w = w_ref[...]

        def store(ref, x_ref):
            acc = _dot(x_ref[...], w)
            for a, lo, hi in ranges:
                @pl.when((j >= lo) & (j < hi))
                def _(a=a):
                    ref[...] = _proj_act(acc, a, None, None).astype(ref.dtype)

        store(o_ref, a_ref)
        pl.when(pl.program_id(0) == 0)(lambda: store(o2_ref, a2_ref))
        return
    if act == "norm":
        gain_ref, flag_ref = rest[:2]
        rest = rest[2:]
    else:
        gain_ref = flag_ref = None
    w = w_ref[...]
    z = _proj_act(_dot(a_ref[...], w), act, gain_ref, flag_ref)
    for kind, o_ref in zip(outs, rest):
        if kind == "heads":
            nh = z.shape[1] // HEAD_DIM
            for g in range(nh):
                o_ref[pl.ds(g, z.shape[0], stride=nh), :] = z[:, g * HEAD_DIM:(g + 1) * HEAD_DIM]
        else:
            o_ref[...] = z.astype(o_ref.dtype)
    o2_ref = rest[-1]

    @pl.when(pl.program_id(0) == 0)
    def _():
        o2_ref[...] = _proj_act(_dot(a2_ref[...], w), act, gain_ref, flag_ref).astype(o2_ref.dtype)


def proj(a, a2, w, row0, nrows, act, outs, tm, tn, gain=None, flag=None, ranges=None):
    m, k = a.shape
    m2 = a2.shape[0]
    assert row0 % tn == 0 and nrows % tn == 0 and m % tm == 0
    jb = row0 // tn
    nj = nrows // tn
    in_specs = [_row_tile((tm, k)),
                pl.BlockSpec((m2, k), lambda i, j: (0, 0)),
                pl.BlockSpec((k, tn), lambda i, j: (0, jb + j), pipeline_mode=pl.Buffered(W_BUFFERS))]
    args = [a, a2, w]
    if act == "norm":
        in_specs += [pl.BlockSpec((1, tn), lambda i, j: (0, j))] * 2
        args += [gain, flag]
    nh = tn // HEAD_DIM
    out_specs, out_shape = [], []
    for kind in outs:
        if kind == "heads":
            out_specs.append(pl.BlockSpec((None, tm * nh, HEAD_DIM), lambda i, j: (j, i, 0)))
            out_shape.append(jax.ShapeDtypeStruct((nj, m * nh, HEAD_DIM), F32))
        else:
            out_specs.append(pl.BlockSpec((tm, tn), lambda i, j: (i, j)))
            out_shape.append(jax.ShapeDtypeStruct((m, nrows), kind))
    out_specs.append(pl.BlockSpec((m2, tn), lambda i, j: (0, jnp.where(i == 0, j, nj - 1))))
    out_shape.append(jax.ShapeDtypeStruct((m2, nrows), F32 if act == "norm" else outs[0]))
    return pl.pallas_call(
        functools.partial(_proj_kernel, act=act, outs=tuple(outs), ranges=ranges),
        grid=(m // tm, nj),
        in_specs=in_specs,
        out_specs=out_specs,
        out_shape=out_shape,
        compiler_params=_cparams(("arbitrary", "arbitrary")),
        name="proj_" + act,
    )(*args)


def _merge_kernel(a1, a2, a3, w1, w2, w3, g1, g2, g3, o_ref):
    m = g1[...].astype(F32) * _dot(a1[...], w1[...])
    m = m + g2[...].astype(F32) * _dot(a2[...], w2[...])
    m = m + g3[...].astype(F32) * _dot(a3[...], w3[...])
    o_ref[...] = m.astype(o_ref.dtype)


def merge(a_list, w_list, l, gm, tm, tn):
    m = a_list[0].shape[0]
    n = w_list[0].shape[2]
    nb = n // tn
    in_specs = [_row_tile((tm, a.shape[1])) for a in a_list]
    in_specs += [pl.BlockSpec((None, w.shape[1], tn), lambda i, j: (l, 0, j),
                              pipeline_mode=pl.Buffered(W_BUFFERS)) for w in w_list]
    in_specs += [pl.BlockSpec((tm, tn), functools.partial(lambda i, j, b: (i, b * nb + j), b=b))
                 for b in range(N_BRANCH)]
    return pl.pallas_call(
        _merge_kernel,
        grid=(m // tm, nb),
        in_specs=in_specs,
        out_specs=pl.BlockSpec((tm, tn), lambda i, j: (i, j)),
        out_shape=jax.ShapeDtypeStruct((m, n), BF16),
        compiler_params=_cparams(("parallel", "arbitrary")),
        name="merge",
    )(*a_list, *w_list, gm, gm, gm)


def _resid_kernel(a_ref, w_ref, x_ref, o_ref):
    o_ref[...] = x_ref[...] + _dot(a_ref[...], w_ref[...])


def out_proj(a, w, l, x, tm, tn):
    m, k = a.shape
    n = w.shape[2]
    if m % tm or m // tm < 2:
        return pl.pallas_call(
            _resid_kernel,
            grid=(m // tm, n // tn),
            in_specs=[_row_tile((tm, k)),
                      pl.BlockSpec((None, k, tn), lambda i, j: (l, 0, j)),
                      pl.BlockSpec((tm, tn), lambda i, j: (i, j))],
            out_specs=pl.BlockSpec((tm, tn), lambda i, j: (i, j)),
            out_shape=jax.ShapeDtypeStruct((m, n), F32),
            compiler_params=_cparams(("parallel", "arbitrary")),
            name="out_proj",
        )(a, w, x)

    def outer(a_hbm, w_hbm, x_hbm, o_hbm):
        pltpu.emit_pipeline(
            _resid_kernel,
            grid=(m // tm, n // tn),
            in_specs=[pl.BlockSpec((tm, k), lambda i, j: (i, 0)),
                      pl.BlockSpec((k, tn), lambda i, j: (0, j), pipeline_mode=pl.Buffered(3)),
                      pl.BlockSpec((tm, tn), lambda i, j: (i, j))],
            out_specs=[pl.BlockSpec((tm, tn), lambda i, j: (i, j))],
        )(a_hbm, w_hbm.at[l], x_hbm, o_hbm)

    anyspec = pl.BlockSpec(memory_space=pl.ANY)
    return pl.pallas_call(
        outer,
        in_specs=[anyspec, anyspec, anyspec],
        out_specs=anyspec,
        out_shape=jax.ShapeDtypeStruct((m, n), F32),
        compiler_params=pltpu.CompilerParams(vmem_limit_bytes=VMEM_LIMIT),
        name="out_proj_piped",
    )(a, w, x)


def _softmax_rows(s, mask):
    s = jnp.where(mask, s, NEG)
    m = jnp.max(s, axis=-1, keepdims=True)
    e = jnp.where(mask, jnp.exp(s - m), 0.0)
    d = jnp.sum(e, axis=-1, keepdims=True)
    return e / jnp.where(d > 0.0, d, 1.0)


def _compress_combine(acc, const, nvalid):
    nsub = acc.shape[0]
    top = acc[:, :LANES]
    bot = pltpu.roll(acc[:, LANES:], nsub - 1, 0)
    row = lax.broadcasted_iota(jnp.int32, (nsub, LANES), 0)
    return jnp.where(row < nvalid, top + bot + const, 0.0)


def _pe_const(pe_ref, w_ref):
    return jnp.dot(pe_ref[...], w_ref[...], precision=lax.Precision.HIGHEST,
                   preferred_element_type=F32)[0:1, :]


def _cmp_prompt_kernel(k_ref, v_ref, wk_ref, wv_ref, pek_ref, pev_ref, wkf_ref, wvf_ref, g_ref,
                       ko_ref, vo_ref):
    nsub = ko_ref.shape[0]
    acck = jnp.zeros((nsub, 2 * HEAD_DIM), F32)
    accv = jnp.zeros((nsub, 2 * HEAD_DIM), F32)
    for l in range(CMP_STRIDE):
        xk = k_ref[pl.ds(l, nsub, stride=CMP_STRIDE), :].astype(BF16)
        xv = v_ref[pl.ds(l, nsub, stride=CMP_STRIDE), :].astype(BF16)
        acck = acck + _dot(xk, wk_ref[l])
        accv = accv + _dot(xv, wv_ref[l])
    kc = _compress_combine(acck, _pe_const(pek_ref, wkf_ref), nsub - 1)
    ms = jnp.mean(kc * kc, axis=-1, keepdims=True)
    ko_ref[...] = (kc * lax.rsqrt(ms + EPS)) * g_ref[...]
    vo_ref[...] = _compress_combine(accv, _pe_const(pev_ref, wvf_ref), nsub - 1)


def _cmp_weights(w_cmp, pe):
    wcat = jnp.concatenate([w_cmp[:CMP_STRIDE], w_cmp[CMP_STRIDE:]], axis=2).astype(BF16)
    pe8 = jnp.broadcast_to(pe.reshape(1, CMP_BLK * HEAD_DIM), (8, CMP_BLK * HEAD_DIM))
    return wcat, pe8, w_cmp.reshape(CMP_BLK * HEAD_DIM, HEAD_DIM)


def compress_prompt(zf, b, t, wk, wv, gk):
    nsub = t // CMP_STRIDE
    wkc, pek, wkf = wk
    wvc, pev, wvf = wv
    full = lambda a: pl.BlockSpec(a.shape, lambda i, g: (0,) * a.ndim)
    head = lambda off: pl.BlockSpec((t, HEAD_DIM), lambda i, g: (i, off // HEAD_DIM + g))
    out = jax.ShapeDtypeStruct((b * N_KV_HEADS, nsub, HEAD_DIM), F32)
    return pl.pallas_call(
        _cmp_prompt_kernel,
        grid=(b, N_KV_HEADS),
        in_specs=[head(0), head(KV_W),
                  full(wkc), full(wvc), full(pek), full(pev), full(wkf), full(wvf), full(gk)],
        out_specs=[pl.BlockSpec((None, nsub, HEAD_DIM), lambda i, g: (i * N_KV_HEADS + g, 0, 0))] * 2,
        out_shape=[out, out],
        compiler_params=_cparams(("parallel", "parallel")),
        name="cmp_prompt",
    )(zf, zf, wkc, wvc, pek, pev, wkf, wvf, gk)


def _sel_matrix(nc_pad, nc, ns_pad, ns):
    c0 = np.arange(nc_pad) * CMP_STRIDE
    s0 = np.arange(ns_pad) * SEL_BLK
    m = (c0[:, None] < s0[None, :] + SEL_BLK) & (c0[:, None] + CMP_BLK > s0[None, :])
    m &= (np.arange(nc_pad)[:, None] < nc) & (np.arange(ns_pad)[None, :] < ns)
    return jnp.asarray(m.astype(np.float32))


EXP_C = SCALE * 1.4426950408889634


def _select_mask_t(imp_t, tpos_row, ns):
    blk = lax.broadcasted_iota(jnp.int32, imp_t.shape, 0)
    cur = tpos_row // SEL_BLK
    forced = (blk == 0) | (blk == cur) | (blk == cur - 1)
    impm = jnp.where(blk > cur, -BIG, jnp.where(forced, BIG, imp_t))
    rank = jnp.zeros(imp_t.shape, F32)
    for s2 in range(ns):
        row = impm[s2:s2 + 1, :]
        gt = jnp.where(row > impm, 1.0, 0.0)
        ge = jnp.where(row >= impm, 1.0, 0.0)
        rank = rank + jnp.where(blk > s2, ge, gt)
    return jnp.where(rank < float(min(N_SEL, ns)), 1.0, 0.0)


def _softmax_bias(s3, bias, valid):
    sm = s3 + bias[None]
    m = jnp.max(sm, axis=-1, keepdims=True)
    e = jnp.exp2((sm - m) * EXP_C)
    inv = 1.0 / jnp.sum(e, axis=-1, keepdims=True)
    if valid is not None:
        inv = jnp.where(valid[None], inv, 0.0)
    return e * inv


def _attn_prompt_kernel(q_ref, kc_ref, vc_ref, ks_ref, vs_ref, kw_ref, vw_ref, gn_ref, gate_ref,
                        msel_ref, e_ref, o_ref, *, tq, kvc, ns):
    g = pl.program_id(1)
    t0 = pl.program_id(2) * tq
    nh = Q_PER_KV
    rows = nh * tq
    q4 = jnp.concatenate([q_ref[:, h * HEAD_DIM:(h + 1) * HEAD_DIM] for h in range(nh)], axis=0)
    tpos = t0 + lax.broadcasted_iota(jnp.int32, (tq, 1), 0)
    tpos_row = t0 + lax.broadcasted_iota(jnp.int32, (1, tq), 1)

    ncp = kc_ref.shape[0]
    endpos = lax.broadcasted_iota(jnp.int32, (1, ncp), 1) * CMP_STRIDE + (CMP_BLK - 1)
    bias_c = jnp.where(endpos <= tpos, 0.0, NEG)
    p3 = _softmax_bias(_dot_nt(q4, kc_ref[...].astype(BF16)).reshape(nh, tq, ncp), bias_c,
                       tpos >= CMP_BLK - 1)
    o_cmp = _dot(p3.reshape(rows, ncp).astype(BF16), vc_ref[...].astype(BF16))
    imp = jnp.dot(jnp.sum(p3, axis=0), msel_ref[...], precision=lax.Precision.HIGHEST,
                  preferred_element_type=F32)
    ns8 = -(-ns // 8) * 8
    sel_t = _select_mask_t(imp.T[:ns8], tpos_row, ns)
    sel = jnp.concatenate([sel_t, jnp.zeros((imp.shape[1] - ns8, tq), F32)], axis=0).T.astype(BF16)

    def body(c, carry):
        m, l, acc = carry
        k0 = pl.multiple_of(c * kvc, kvc)
        kpos = k0 + lax.broadcasted_iota(jnp.int32, (1, kvc), 1)
        bias = jnp.where(kpos <= tpos, (_dot(sel, e_ref[c]) - 1.0) * -NEG, NEG)
        sm = _dot_nt(q4, ks_ref[pl.ds(k0, kvc), :]).reshape(nh, tq, kvc) + bias[None]
        m_new = jnp.maximum(m, jnp.max(sm, axis=-1, keepdims=True))
        alpha = jnp.exp2((m - m_new) * EXP_C)
        e = jnp.exp2((sm - m_new) * EXP_C)
        l = alpha * l + jnp.sum(e, axis=-1, keepdims=True)
        pv = _dot(e.reshape(rows, kvc).astype(BF16), vs_ref[pl.ds(k0, kvc), :])
        return m_new, l, alpha * acc + pv.reshape(nh, tq, HEAD_DIM)

    nch = (t0 + tq + kvc - 1) // kvc
    m, l, acc = lax.fori_loop(0, nch, body, (jnp.full((nh, tq, 1), NEG, F32), jnp.zeros((nh, tq, 1), F32),
                                             jnp.zeros((nh, tq, HEAD_DIM), F32)))
    o_sel = (acc * jnp.where(l > 0.0, 1.0 / l, 0.0)).reshape(rows, HEAD_DIM)

    nwin = WINDOW + tq
    w0 = pl.multiple_of(jnp.maximum(t0 - WINDOW, 0), tq)
    diff = tpos - (w0 + lax.broadcasted_iota(jnp.int32, (1, nwin), 1))
    bias_w = jnp.where((diff >= 0) & (diff <= WINDOW), 0.0, NEG)
    pw = _softmax_bias(_dot_nt(q4, kw_ref[pl.ds(w0, nwin), :]).reshape(nh, tq, nwin), bias_w, None)
    o_win = _dot(pw.reshape(rows, nwin).astype(BF16), vw_ref[pl.ds(w0, nwin), :])

    gs = _sigmoid(gn_ref[...])
    lane = lax.broadcasted_iota(jnp.int32, gs.shape, 1)
    for h in range(nh):
        r = slice(h * tq, (h + 1) * tq)
        o = None
        for br, ob in enumerate((o_cmp, o_sel, o_win)):
            cidx = br * N_HEADS + g * nh + h
            gcol = jnp.sum(jnp.where(lane == cidx, gs, 0.0), axis=-1, keepdims=True)
            o = gcol * ob[r] if o is None else o + gcol * ob[r]
        hs = slice(h * HEAD_DIM, (h + 1) * HEAD_DIM)
        o_ref[:, hs] = (o * gate_ref[:, hs].astype(F32)).astype(o_ref.dtype)


def attn_prompt(zq, zkv, kcmp, vcmp, gn, gate, b, t):
    tq, kvc = 256, 512
    assert t // CMP_STRIDE == LANES and t % kvc == 0 and t >= WINDOW + tq
    nq = t // tq
    ns = -(-t // SEL_BLK)
    nc = t // CMP_STRIDE - CMP_BLK // CMP_STRIDE + 1
    msel = _sel_matrix(LANES, nc, LANES, ns)
    kk = np.arange(t)
    e3 = (kk[None, :] // SEL_BLK == np.arange(LANES)[:, None]).astype(np.float32)
    e3 = jnp.asarray(e3.reshape(LANES, t // kvc, kvc).transpose(1, 0, 2), BF16)
    hb = lambda off: (lambda bi, g, i: (bi, off // HEAD_DIM + g))
    kv_spec = lambda off: pl.BlockSpec((t, HEAD_DIM), hb(off))
    cm_spec = pl.BlockSpec((None, t // CMP_STRIDE, HEAD_DIM), lambda bi, g, i: (bi * N_KV_HEADS + g, 0, 0))
    row4 = pl.BlockSpec((tq, Q_PER_KV * HEAD_DIM), lambda bi, g, i: (bi * nq + i, g))
    return pl.pallas_call(
        functools.partial(_attn_prompt_kernel, tq=tq, kvc=kvc, ns=ns),
        grid=(b, N_KV_HEADS, nq),
        in_specs=[row4, cm_spec, cm_spec,
                  kv_spec(2 * KV_W), kv_spec(3 * KV_W), kv_spec(4 * KV_W), kv_spec(5 * KV_W),
                  pl.BlockSpec((tq, LANES), lambda bi, g, i: (bi * nq + i, 0)),
                  row4,
                  pl.BlockSpec(msel.shape, lambda bi, g, i: (0, 0)),
                  pl.BlockSpec(e3.shape, lambda bi, g, i: (0, 0, 0))],
        out_specs=row4,
        out_shape=jax.ShapeDtypeStruct((b * t, NSA_W), BF16),
        compiler_params=_cparams(("parallel", "parallel", "arbitrary")),
        name="attn_prompt",
    )(zq, kcmp, vcmp, zkv, zkv, zkv, zkv, gn, gate, msel, e3)


def _layer_norm(v, g, b):
    vc = v - jnp.mean(v, axis=-1, keepdims=True)
    var = jnp.mean(vc * vc, axis=-1, keepdims=True)
    return vc * lax.rsqrt(var + EPS) * g + b


def _mlp_prompt_kernel(u_ref, v_ref, gate_ref, lg_ref, lb_ref, ws_ref, bst_ref, o_ref):
    tm = u_ref.shape[0]
    gw = ws_ref.shape[1]
    vb = _layer_norm(v_ref[...], lg_ref[...], lb_ref[...]).astype(BF16)
    row = lax.broadcasted_iota(jnp.int32, (CHUNK, CHUNK), 0)
    col = lax.broadcasted_iota(jnp.int32, (CHUNK, CHUNK), 1)
    for gi in range(MLP_GROUPS):
        cs = slice(gi * gw, (gi + 1) * gw)
        wsg = jnp.where(row >= col, ws_ref[gi], 0.0).astype(BF16)
        bias = bst_ref[:, gi:gi + 1]
        for ch in range(tm // CHUNK):
            rs = slice(ch * CHUNK, (ch + 1) * CHUNK)
            mixed = _dot(wsg, vb[rs, cs]) + bias
            o_ref[rs, cs] = (u_ref[rs, cs] * mixed * gate_ref[rs, cs].astype(F32)).astype(o_ref.dtype)


def mlp_prompt(z, cu, cg, ln_g, ln_b, w_s, b_s, tm):
    m = z.shape[0]
    w = ln_g.shape[0]
    assert w // MLP_GROUPS == CHUNK == w_s.shape[1]
    full = lambda a: pl.BlockSpec(a.shape, lambda i: (0,) * a.ndim)
    bst = b_s.T
    return pl.pallas_call(
        _mlp_prompt_kernel,
        grid=(m // tm,),
        in_specs=[pl.BlockSpec((tm, w), lambda i: (i, cu)), pl.BlockSpec((tm, w), lambda i: (i, cu + 1)),
                  pl.BlockSpec((tm, w), lambda i: (i, cg)),
                  pl.BlockSpec((1, w), lambda i: (0, 0)), pl.BlockSpec((1, w), lambda i: (0, 0)),
                  full(w_s), full(bst)],
        out_specs=pl.BlockSpec((tm, w), lambda i: (i, 0)),
        out_shape=jax.ShapeDtypeStruct((m, w), BF16),
        compiler_params=_cparams(("parallel",)),
        name="mlp_prompt",
    )(z, z, z, ln_g.reshape(1, w), ln_b.reshape(1, w), w_s, bst)


def _softplus(x):
    return jnp.maximum(x, 0.0) + jnp.log1p(jnp.exp(-jnp.abs(x)))


def _lru_gates(xc, wa_ref, wx_ref, ba, bx, lam):
    nh = xc.shape[1] // HEAD_DIM
    rs, is_ = [], []
    for hh in range(nh):
        xh = xc[:, hh * HEAD_DIM:(hh + 1) * HEAD_DIM].astype(BF16)
        rs.append(_dot(xh, wa_ref[hh]))
        is_.append(_dot(xh, wx_ref[hh]))
    r = _sigmoid(jnp.concatenate(rs, axis=1) + ba)
    i = _sigmoid(jnp.concatenate(is_, axis=1) + bx)
    log_a = -LRU_C * r * _softplus(-lam)
    th = jnp.tanh(log_a)
    return jnp.exp(log_a), jnp.sqrt(-2.0 * th / (1.0 - th)) * (i * xc)


SCAN_GROUP = 4


def _lru_prompt_kernel(x_ref, gate_ref, cw_ref, cb_ref, wa_ref, wx_ref, ba_ref, bx_ref, lam_ref,
                       o_ref, h_ref, a_scr, u_scr):
    t, wb = x_ref.shape
    x = x_ref[...]
    row = lax.broadcasted_iota(jnp.int32, (t, wb), 0)
    xc = cb_ref[...] + x * cw_ref[CONV_W - 1:CONV_W, :]
    for d in range(1, CONV_W):
        xs = jnp.where(row >= d, pltpu.roll(x, d, 0), 0.0)
        xc = xc + xs * cw_ref[CONV_W - 1 - d:CONV_W - d, :]
    a, u = _lru_gates(xc, wa_ref, wx_ref, ba_ref[...], bx_ref[...], lam_ref[...])
    a_scr[...] = a
    u_scr[...] = u
    row8 = lax.broadcasted_iota(jnp.int32, (8, wb), 0)

    def body(bi, h):
        r0 = pl.multiple_of(bi * (8 * SCAN_GROUP), 8 * SCAN_GROUP)
        tiles = []
        for k in range(SCAN_GROUP):
            a8 = a_scr[pl.ds(r0 + 8 * k, 8), :]
            u8 = u_scr[pl.ds(r0 + 8 * k, 8), :]
            for d in (1, 2, 4):
                a_sh = jnp.where(row8 >= d, pltpu.roll(a8, d, 0), 1.0)
                u_sh = jnp.where(row8 >= d, pltpu.roll(u8, d, 0), 0.0)
                u8 = a8 * u_sh + u8
                a8 = a8 * a_sh
            tiles.append((a8, u8))
        for k, (a8, u8) in enumerate(tiles):
            h8 = a8 * h + u8
            u_scr[pl.ds(r0 + 8 * k, 8), :] = h8
            h = h8[7:8, :]
        return h

    h = lax.fori_loop(0, t // (8 * SCAN_GROUP), body, jnp.zeros((1, wb), F32))
    h_ref[...] = h
    o_ref[...] = (u_scr[...] * gate_ref[...].astype(F32)).astype(o_ref.dtype)


def lru_prompt(z, cx, cg, conv_w, conv_b, wa, wx, ba, bx, lam, b, t):
    w = lam.shape[0]
    wb = 512
    nh = wb // HEAD_DIM
    assert w % wb == 0 and t % (8 * SCAN_GROUP) == 0 and wa.shape[1] == HEAD_DIM
    vec = lambda a: a.reshape(1, w)
    vspec = pl.BlockSpec((1, wb), lambda i, j: (0, j))
    blk = pl.BlockSpec((t, wb), lambda i, j: (i, j))
    zblk = lambda c0: pl.BlockSpec((t, wb), lambda i, j: (i, c0 + j))
    hspec = pl.BlockSpec((nh, HEAD_DIM, HEAD_DIM), lambda i, j: (j, 0, 0))
    return pl.pallas_call(
        _lru_prompt_kernel,
        grid=(b, w // wb),
        in_specs=[zblk(cx), zblk(cg), pl.BlockSpec((CONV_W, wb), lambda i, j: (0, j)), vspec,
                  hspec, hspec, vspec, vspec, vspec],
        out_specs=[blk, pl.BlockSpec((None, 1, wb), lambda i, j: (i, 0, j))],
        out_shape=[jax.ShapeDtypeStruct((b * t, w), BF16), jax.ShapeDtypeStruct((b, 1, w), F32)],
        scratch_shapes=[pltpu.VMEM((t, wb), F32), pltpu.VMEM((t, wb), F32)],
        compiler_params=_cparams(("parallel", "parallel")),
        name="lru_prompt",
    )(z, z, conv_w, vec(conv_b), wa.astype(BF16), wx.astype(BF16), vec(ba), vec(bx), vec(lam))


ATT_W = NSA_W + 6 * KV_W
SEG = 512


PACK_TAIL = 64


def _pack_kernel(a_ref, b_ref, o_ref, *, gn_tile, ngate):
    j = pl.program_id(0)

    @pl.when(j < gn_tile)
    def _():
        o_ref[...] = a_ref[...].T.astype(o_ref.dtype)

    @pl.when(j == gn_tile)
    def _():
        row = lax.broadcasted_iota(jnp.int32, a_ref.shape, 0)
        o_ref[...] = jnp.where(row < ngate, a_ref[...], 0.0).T.astype(o_ref.dtype)

    @pl.when(j > gn_tile)
    def _():
        o_ref[...] = jnp.concatenate([a_ref[ngate:, :], b_ref[:ngate, :]], axis=0).T.astype(o_ref.dtype)


def _pack_w_in(w_in, l, d_model, mlp_w, lru_w):
    ngate = N_BRANCH * N_HEADS
    sizes = (("att", ATT_W), ("gn", SEG), ("gate_nsa", NSA_W), ("uv", 2 * mlp_w), ("gate_mlp", mlp_w),
             ("xl", lru_w), ("gate_lru", lru_w), ("gm", N_BRANCH * d_model))
    offs, acc = {}, 0
    for name, size in sizes:
        offs[name] = (acc, size)
        acc += size
    wt = jnp.swapaxes(w_in, 1, 2)
    n_in, k = wt.shape[1:]
    assert acc == n_in - ngate + SEG and ATT_W % SEG == 0 and ngate <= PACK_TAIL and ngate % 16 == 0
    gn_tile = ATT_W // SEG
    w = pl.pallas_call(
        functools.partial(_pack_kernel, gn_tile=gn_tile, ngate=ngate),
        grid=(acc // SEG,),
        in_specs=[pl.BlockSpec((None, SEG, k), lambda j: (l, jnp.where(j > gn_tile, j - 1, j), 0)),
                  pl.BlockSpec((None, PACK_TAIL, k), lambda j: (l, (SEG // PACK_TAIL) * j, 0))],
        out_specs=pl.BlockSpec((k, SEG), lambda j: (0, j)),
        out_shape=jax.ShapeDtypeStruct((k, acc), BF16),
        compiler_params=_cparams(("parallel",)),
        name="pack_w",
    )(wt, wt)
    return w, offs


def _norm_vectors(q_g, k_g):
    one = jnp.ones((KV_W,), F32)
    zero = jnp.zeros((KV_W,), F32)
    gain = jnp.concatenate([jnp.tile(q_g, N_HEADS), one, one, jnp.tile(k_g[1], N_KV_HEADS), one,
                            jnp.tile(k_g[2], N_KV_HEADS), one])
    flag = jnp.concatenate([jnp.ones((NSA_W,), F32), zero, zero, one, zero, one, zero])
    return gain.reshape(1, ATT_W), flag.reshape(1, ATT_W)


MID_SEGMENTS = (("gate_nsa", "silu"), ("uv", "gelu"), ("gate_mlp", "silu"), ("xl", "none"), ("gate_lru", "silu"))


def _in_proj(x2d, xs2d, lw, tm):
    h = rms_rows(x2d, lw["norm_g"], min(tm, 512))
    hs = rms_rows(xs2d, lw["norm_g"], xs2d.shape[0])
    w, offs = lw["w_pack"]
    p = lambda name, act, dt, tn=COL_TILE, n=None, off=0, tm=tm, **kw: proj(
        h, hs, w, offs[name][0] + off, n or offs[name][1], act, dt if isinstance(dt, tuple) else (dt,),
        tm, tn, **kw)
    gain, flag = lw["gain"], lw["flag"]
    zq, zqs = p("att", "norm", BF16, n=NSA_W, gain=gain[:, :NSA_W], flag=flag[:, :NSA_W])
    kvf, kvb, kvh, kvs = p("att", "norm", (F32, BF16, "heads"), n=ATT_W - NSA_W, off=NSA_W, tn=SEG,
                           gain=gain[:, NSA_W:], flag=flag[:, NSA_W:])
    z, zs = dict(q=zq, kvf=kvf, kvb=kvb, kvh=kvh), dict(q=zqs, kv=kvs)
    z["gn"], zs["gn"] = p("gn", "none", F32, tn=LANES, n=LANES)
    z["gm"], zs["gm"] = p("gm", "sigmoid", BF16)
    mid0 = offs["gate_nsa"][0]
    ranges, mid = [], {}
    for name, act in MID_SEGMENTS:
        lo = offs[name][0] - mid0
        ranges.append((act, lo // COL_TILE, (lo + offs[name][1]) // COL_TILE))
        mid[name] = (lo, offs[name][1])
    z["mid"], zs["mid"] = p("gate_nsa", "mixed", F32, n=sum(n for _, n in mid.values()), ranges=tuple(ranges))
    z["mid_off"] = zs["mid_off"] = mid
    return z, zs


ROW_TILE = 1024
COL_TILE = 512


def prompt_layer(x2d, z, lw, l, b, t, w_buf):
    tm = ROW_TILE
    kcmp, vcmp = compress_prompt(z["kvf"], b, t, lw["cmp_k"], lw["cmp_v"], lw["k_norm_g"][0:1])
    mid, off = z["mid"], z["mid_off"]
    assert off["gate_nsa"][0] == 0
    o_nsa = attn_prompt(z["q"], z["kvb"], kcmp, vcmp, z["gn"], mid, b, t)
    mw = off["gate_mlp"][1]
    o_mlp = mlp_prompt(mid, off["uv"][0] // mw, off["gate_mlp"][0] // mw, lw["mlp_ln_g"], lw["mlp_ln_b"],
                       lw["w_s"], lw["b_s"], 512)
    o_lru, h_last = lru_prompt(mid, off["xl"][0] // 512, off["gate_lru"][0] // 512, lw["conv_w"], lw["conv_b"],
                               lw["lru_wa"], lw["lru_wx"], lw["lru_ba"], lw["lru_bx"], lw["lru_lambda"], b, t)
    m = merge([o_nsa, o_mlp, o_lru], lw["w_br"], l, z["gm"], tm, COL_TILE)
    y = out_proj(m, lw["w_out"], l, x2d, tm, COL_TILE)
    kv = lambda i: z["kvh"][i].reshape(b, t, N_KV_HEADS, HEAD_DIM)
    xl = mid.reshape(b, t, -1)[:, t - (CONV_W - 1):, off["xl"][0]:off["xl"][0] + off["xl"][1]]
    assert t >= w_buf and t >= CONV_W - 1
    state = dict(cmp_k=kv(0), cmp_v=kv(1), sel_k=kv(2), sel_v=kv(3),
                 win_k=kv(4)[:, t - w_buf:], win_v=kv(5)[:, t - w_buf:],
                 lru_h=h_last.reshape(b, -1), lru_conv=xl)
    return y, state


PAGES_PER_STEP = 32


def _cmp_paged_kernel(pt_ref, *refs, pg):
    k_refs, v_refs = refs[:pg], refs[pg:2 * pg]
    wk_ref, wv_ref, ko_ref, vo_ref, r_scr = refs[2 * pg:]
    rows = k_refs[0].shape[0]
    sub = N_KV_HEADS * CMP_STRIDE
    nsp = rows // sub
    low = lax.broadcasted_iota(jnp.int32, (8, HEAD_DIM), 0) < N_KV_HEADS
    for page_refs, w_ref, o_ref in ((k_refs, wk_ref, ko_ref), (v_refs, wv_ref, vo_ref)):
        blocks = []
        for r in page_refs:
            for ip in range(nsp // 2):
                cols = []
                for m in range(sub // 8):
                    ta = r[pl.ds(sub * (2 * ip) + 8 * m, 8), :]
                    tb = r[pl.ds(sub * (2 * ip + 1) + 8 * m, 8), :]
                    cols.append(jnp.where(low, ta, pltpu.roll(tb, N_KV_HEADS, 0)))
                    cols.append(jnp.where(low, pltpu.roll(ta, N_KV_HEADS, 0), tb))
                blocks.append(jnp.concatenate(cols, axis=1))
        res = _dot(jnp.concatenate(blocks, axis=0).astype(BF16), w_ref[...])
        n = res.shape[0]
        r_scr[0] = res[:, :HEAD_DIM]
        r_scr[1] = res[:, HEAD_DIM:]
        for g in range(N_KV_HEADS):
            o_ref[g] = jnp.concatenate([r_scr[0, pl.ds(g, n // N_KV_HEADS, stride=N_KV_HEADS), :],
                                        r_scr[1, pl.ds(g, n // N_KV_HEADS, stride=N_KV_HEADS), :]], axis=1)


def _cmp_paged_weights(wcat):
    return wcat.reshape(CMP_STRIDE * HEAD_DIM, 2 * HEAD_DIM)


def compress_paged(pool_k, pool_v, page_table, wkc, wvc, page0):
    b, n_pages = page_table.shape
    rows = pool_k.shape[1]
    pg = PAGES_PER_STEP
    nsp = rows // N_KV_HEADS // CMP_STRIDE
    assert n_pages % pg == 0 and N_KV_HEADS * 2 == 8 and nsp % 2 == 0
    wkc, wvc = _cmp_paged_weights(wkc), _cmp_paged_weights(wvc)
    page_spec = lambda p: pl.BlockSpec((None, rows, HEAD_DIM), lambda i, c, pt: (page0 + pt[i, c * pg + p], 0, 0))
    full = lambda a: pl.BlockSpec(a.shape, lambda i, c, pt: (0,) * a.ndim)
    out = jax.ShapeDtypeStruct((b, N_KV_HEADS, n_pages * nsp, 2 * HEAD_DIM), F32)
    ospec = pl.BlockSpec((None, N_KV_HEADS, pg * nsp, 2 * HEAD_DIM), lambda i, c, pt: (i, 0, c, 0))
    return pl.pallas_call(
        functools.partial(_cmp_paged_kernel, pg=pg),
        grid_spec=pltpu.PrefetchScalarGridSpec(
            num_scalar_prefetch=1,
            grid=(b, n_pages // pg),
            in_specs=[page_spec(p) for p in range(pg)] * 2 + [full(wkc), full(wvc)],
            out_specs=[ospec, ospec],
            scratch_shapes=[pltpu.VMEM((2, pg * nsp * N_KV_HEADS, HEAD_DIM), F32)]),
        out_shape=[out, out],
        compiler_params=_cparams(("parallel", "arbitrary")),
        name="cmp_paged",
    )(page_table, *([pool_k] * pg), *([pool_v] * pg), wkc, wvc)


def _cmp_sample_kernel(tbk_ref, tbv_ref, q_ref, pek_ref, pev_ref, wkf_ref, wvf_ref, gk_ref, msel_ref,
                       o_ref, idx_ref, *, tpos, nc, ns):
    ck = _pe_const(pek_ref, wkf_ref)
    cv = _pe_const(pev_ref, wvf_ref)
    q = q_ref[...].astype(BF16)
    nh = q.shape[0]
    nsub = tbk_ref.shape[1]
    hrow = lax.broadcasted_iota(jnp.int32, (nh, 1), 0)
    blk = lax.broadcasted_iota(jnp.int32, (1, nsub), 1)
    cmask = (blk * CMP_STRIDE + (CMP_BLK - 1) <= tpos) & (blk < nc)
    row8 = lax.broadcasted_iota(jnp.int32, (8, nsub), 0)
    o = jnp.zeros((nh, HEAD_DIM), F32)
    psum = jnp.zeros((8, nsub), F32)
    for g in range(N_KV_HEADS):
        kc = _compress_combine(tbk_ref[g], ck, nc)
        ms = jnp.mean(kc * kc, axis=-1, keepdims=True)
        kc = (kc * lax.rsqrt(ms + EPS)) * gk_ref[...]
        vc = _compress_combine(tbv_ref[g], cv, nc)
        p = _softmax_rows(_dot_nt(q, kc.astype(BF16)) * SCALE, cmask & (hrow // Q_PER_KV == g))
        o = o + _dot(p.astype(BF16), vc.astype(BF16))
        psum = jnp.where(row8 == g, jnp.sum(p, axis=0, keepdims=True), psum)
    o_ref[...] = o
    imp = jnp.dot(psum, msel_ref[...], precision=lax.Precision.HIGHEST, preferred_element_type=F32)
    lane = lax.broadcasted_iota(jnp.int32, imp.shape, 1)
    cur = tpos // SEL_BLK
    forced = (lane == 0) | (lane == cur) | (lane == cur - 1)
    impm = jnp.where(lane > cur, -BIG, jnp.where(forced, BIG, imp))
    impm = jnp.where(lane < ns, impm, -jnp.inf)
    lane_f = lane.astype(F32)
    out_lane = lax.broadcasted_iota(jnp.int32, idx_ref.shape, 1)
    idxs = jnp.zeros(idx_ref.shape, F32)
    for j in range(min(N_SEL, ns)):
        mx = jnp.max(impm, axis=-1, keepdims=True)
        am = jnp.min(jnp.where(impm == mx, lane_f, 1e9), axis=-1, keepdims=True)
        idxs = jnp.where(out_lane == j, am, idxs)
        impm = jnp.where(lane_f == am, -jnp.inf, impm)
    idx_ref[...] = idxs.astype(jnp.int32)


def cmp_sample(tbk, tbv, q3, ck, cv, gk, tpos):
    b, nkv, nsub, _ = tbk.shape
    nc = (tpos + 1) // CMP_STRIDE - CMP_BLK // CMP_STRIDE + 1
    ns = -(-(tpos + 1) // SEL_BLK)
    ns_pad = -(-ns // LANES) * LANES
    assert nc <= nsub
    msel = _sel_matrix(nsub, nc, ns_pad, ns)
    _, pek, wkf = ck
    _, pev, wvf = cv
    full = lambda a: pl.BlockSpec(a.shape, lambda i: (0,) * a.ndim)
    tb_spec = pl.BlockSpec((None, nkv, nsub, 2 * HEAD_DIM), lambda i: (i, 0, 0, 0))
    return pl.pallas_call(
        functools.partial(_cmp_sample_kernel, tpos=tpos, nc=nc, ns=ns),
        grid=(b,),
        in_specs=[tb_spec, tb_spec, pl.BlockSpec((None, N_HEADS, HEAD_DIM), lambda i: (i, 0, 0)),
                  full(pek), full(pev), full(wkf), full(wvf), full(gk), full(msel)],
        out_specs=[pl.BlockSpec((None, N_HEADS, HEAD_DIM), lambda i: (i, 0, 0)),
                   pl.BlockSpec((None, 8, LANES), lambda i: (i, 0, 0))],
        out_shape=[jax.ShapeDtypeStruct((b, N_HEADS, HEAD_DIM), F32),
                   jax.ShapeDtypeStruct((b, 8, LANES), jnp.int32)],
        compiler_params=_cparams(("parallel",)),
        name="cmp_sample",
    )(tbk, tbv, q3, pek, pev, wkf, wvf, gk, msel)


def _attend_with_new(q, k_all, v_all, mask, k_new, v_new, new_ok):
    s = jnp.where(mask, _dot_nt(q.astype(BF16), k_all) * SCALE, NEG)
    s_new = jnp.where(new_ok, jnp.sum(q * k_new, axis=-1, keepdims=True) * SCALE, NEG)
    m = jnp.maximum(jnp.max(s, axis=-1, keepdims=True), s_new)
    e = jnp.where(mask, jnp.exp(s - m), 0.0)
    e_new = jnp.where(new_ok, jnp.exp(s_new - m), 0.0)
    d = jnp.sum(e, axis=-1, keepdims=True) + e_new
    o = _dot(e.astype(BF16), v_all) + e_new * v_new
    return o / jnp.where(d > 0.0, d, 1.0)


def _attn_sample_kernel(pt_ref, idx_ref, *refs, nblk, tpos, past_len, w_buf):
    k_refs, v_refs = refs[:nblk], refs[nblk:2 * nblk]
    (q_ref, ocmp_ref, kw_ref, vw_ref, ksn_ref, vsn_ref, kwn_ref, vwn_ref, gn_ref, gate_ref,
     o_ref) = refs[2 * nblk:]
    b = pl.program_id(0)
    g = pl.program_id(1)
    q = q_ref[...]
    rb = k_refs[0].shape[0]
    r = lax.broadcasted_iota(jnp.int32, (1, rb), 1)
    tok, hd = r // N_KV_HEADS, r % N_KV_HEADS
    masks = []
    new_sel = False
    for j in range(nblk):
        s = idx_ref[b, g * nblk + j]
        kpos = s * SEL_BLK + tok
        masks.append((hd == g) & (kpos <= tpos) & (kpos < past_len))
        new_sel = new_sel | (s == past_len // SEL_BLK)
    k_all = jnp.concatenate([kr[...].astype(BF16) for kr in k_refs], axis=0)
    v_all = jnp.concatenate([vr[...].astype(BF16) for vr in v_refs], axis=0)
    o_sel = _attend_with_new(q, k_all, v_all, jnp.concatenate(masks, axis=1),
                             ksn_ref[...], vsn_ref[...], new_sel & (past_len <= tpos))
    rw = lax.broadcasted_iota(jnp.int32, (1, kw_ref.shape[0]), 1)
    diff = tpos - (past_len - w_buf + rw // N_KV_HEADS)
    wmask = (rw % N_KV_HEADS == g) & (diff >= 0) & (diff <= WINDOW)
    o_win = _attend_with_new(q, kw_ref[...].astype(BF16), vw_ref[...].astype(BF16), wmask,
                             kwn_ref[...], vwn_ref[...], tpos - past_len <= WINDOW)
    gs = _sigmoid(gn_ref[...])
    o = gs[:, 0:1] * ocmp_ref[...] + gs[:, 1:2] * o_sel + gs[:, 2:3] * o_win
    o_ref[...] = o * gate_ref[...]


def attn_sample(pool_k, pool_v, page_table, idx, q4, ocmp4, win_k, win_v, ks_new, vs_new, kw_new, vw_new,
                gn4, gate4, past_len, w_buf, page_size, page0, seq0):
    b, n_pages = page_table.shape
    nblk = idx.shape[1] // N_KV_HEADS
    rb = pool_k.shape[1]
    halves = page_size // SEL_BLK
    tpos = past_len

    def blk_map(j):
        def f(i, g, pt, ix):
            s = ix[i, g * nblk + j]
            page = pt[i, jnp.minimum(s // halves, n_pages - 1)]
            return ((page0 + page) * halves + s % halves, 0, 0)
        return f

    blk_spec = lambda j: pl.BlockSpec((None, rb, HEAD_DIM), blk_map(j))
    per_bg = lambda a: pl.BlockSpec((None, None) + a.shape[2:], lambda i, g, pt, ix: (i, g, 0, 0))
    per_b = lambda a: pl.BlockSpec((None,) + a.shape[1:], lambda i, g, pt, ix: (seq0 + i, 0, 0))
    small = (q4, ocmp4)
    news = (ks_new, vs_new, kw_new, vw_new, gn4, gate4)
    return pl.pallas_call(
        functools.partial(_attn_sample_kernel, nblk=nblk, tpos=tpos, past_len=past_len, w_buf=w_buf),
        grid_spec=pltpu.PrefetchScalarGridSpec(
            num_scalar_prefetch=2,
            grid=(b, N_KV_HEADS),
            in_specs=[blk_spec(j) for j in range(nblk)] * 2 + [per_bg(a) for a in small]
            + [per_b(win_k), per_b(win_v)] + [per_bg(a) for a in news],
            out_specs=per_bg(q4)),
        out_shape=jax.ShapeDtypeStruct(q4.shape, F32),
        compiler_params=_cparams(("parallel", "arbitrary")),
        name="attn_sample",
    )(page_table, idx, *([pool_k] * nblk), *([pool_v] * nblk), q4, ocmp4, win_k, win_v, *news)


def _point_sample_kernel(u_ref, v_ref, gmlp_ref, lg_ref, lb_ref, ws0_ref, bs0_ref,
                         x_ref, glru_ref, buf_ref, h0_ref, cw_ref, cb_ref, wa_ref, wx_ref, ba_ref, bx_ref,
                         lam_ref, omlp_ref, vrow_ref, olru_ref, hnew_ref):
    vn = _layer_norm(v_ref[...], lg_ref[...], lb_ref[...])
    vrow_ref[...] = vn
    mixed = ws0_ref[...] * vn + bs0_ref[...]
    omlp_ref[...] = (u_ref[...] * mixed * gmlp_ref[...].astype(F32)).astype(omlp_ref.dtype)
    x = x_ref[...]
    xc = cb_ref[...] + x * cw_ref[CONV_W - 1:CONV_W, :]
    for j in range(CONV_W - 1):
        xc = xc + buf_ref[j] * cw_ref[j:j + 1, :]
    a, u = _lru_gates(xc, wa_ref, wx_ref, ba_ref[...], bx_ref[...], lam_ref[...])
    h = a * h0_ref[...] + u
    hnew_ref[...] = h
    olru_ref[...] = (h * glru_ref[...].astype(F32)).astype(olru_ref.dtype)


def point_sample(uv, gate_mlp, xl, gate_lru, buf_t, h0, lw):
    b, w2 = uv.shape
    w = w2 // 2
    gw = w // MLP_GROUPS
    vec = lambda a: a.reshape(1, -1)
    ws0 = vec(jnp.repeat(lw["w_s"][:, 0, 0], gw))
    bs0 = vec(jnp.repeat(lw["b_s"][:, 0], gw))
    args = (uv[:, :w], uv[:, w:], gate_mlp, vec(lw["mlp_ln_g"]), vec(lw["mlp_ln_b"]), ws0, bs0,
            xl, gate_lru, buf_t, h0, lw["conv_w"], vec(lw["conv_b"]), lw["lru_wa"].astype(BF16),
            lw["lru_wx"].astype(BF16), vec(lw["lru_ba"]), vec(lw["lru_bx"]), vec(lw["lru_lambda"]))
    lw_ = xl.shape[1]
    return pl.pallas_call(
        _point_sample_kernel,
        out_shape=[jax.ShapeDtypeStruct((b, w), BF16), jax.ShapeDtypeStruct((b, w), F32),
                   jax.ShapeDtypeStruct((b, lw_), BF16), jax.ShapeDtypeStruct((b, lw_), F32)],
        compiler_params=pltpu.CompilerParams(vmem_limit_bytes=VMEM_LIMIT),
        name="point_sample",
    )(*args)


def sample_layer(x2d, z, lw, l, caches, page_table, past_len, w_buf):
    b = x2d.shape[0]
    seg = lambda i: z["kv"][:, i * KV_W:(i + 1) * KV_W]
    kvh = lambda a: a.reshape(b, N_KV_HEADS, 1, HEAD_DIM)
    n_pool, page_size = caches["cmp_k"].shape[1:3]
    page0 = l * n_pool
    pool3 = lambda a: a.reshape(-1, page_size * N_KV_HEADS, HEAD_DIM)
    halves = lambda a: a.reshape(-1, SEL_BLK * N_KV_HEADS, HEAD_DIM)
    tbk, tbv = compress_paged(pool3(caches["cmp_k"]), pool3(caches["cmp_v"]), page_table,
                              lw["cmp_k"][0], lw["cmp_v"][0], page0)
    q3 = z["q"].reshape(b, N_HEADS, HEAD_DIM)
    o_cmp, idx = cmp_sample(tbk, tbv, q3, lw["cmp_k"], lw["cmp_v"], lw["k_norm_g"][0:1], past_len)
    n_sel = min(N_SEL, -(-(past_len + 1) // SEL_BLK))
    idx = idx[:, :N_KV_HEADS, :n_sel].reshape(b, N_KV_HEADS * n_sel)
    four = lambda a: a.reshape(b, N_KV_HEADS, Q_PER_KV, -1)
    mseg = lambda name: z["mid"][:, z["mid_off"][name][0]:z["mid_off"][name][0] + z["mid_off"][name][1]]
    gn4 = z["gn"][:, :N_BRANCH * N_HEADS].reshape(b, N_BRANCH, N_KV_HEADS, Q_PER_KV).transpose(0, 2, 3, 1)
    win3 = lambda a: a.reshape(-1, w_buf * N_KV_HEADS, HEAD_DIM)
    o_nsa = attn_sample(halves(caches["sel_k"]), halves(caches["sel_v"]), page_table, idx,
                        four(q3), four(o_cmp), win3(caches["win_k"]), win3(caches["win_v"]),
                        kvh(seg(2)), kvh(seg(3)), kvh(seg(4)), kvh(seg(5)),
                        gn4, four(mseg("gate_nsa")), past_len, w_buf, page_size, page0, l * b)
    o_mlp, v_rows, o_lru, h_new = point_sample(
        mseg("uv"), mseg("gate_mlp"), mseg("xl"), mseg("gate_lru"), caches["lru_conv"].transpose(1, 0, 2),
        caches["lru_h"].astype(F32), lw)
    m = merge([o_nsa.reshape(b, NSA_W).astype(BF16), o_mlp, o_lru], lw["w_br"], l, z["gm"], b, SEG)
    y = out_proj(m, lw["w_out"], l, x2d, b, SEG)
    tok = lambda a: a.reshape(b, 1, N_KV_HEADS, HEAD_DIM)
    state = dict(cmp_k=tok(seg(0)), cmp_v=tok(seg(1)), sel_k=tok(seg(2)), sel_v=tok(seg(3)),
                 win_k=jnp.concatenate([caches["win_k"][l], tok(seg(4))], axis=1)[:, -w_buf:],
                 win_v=jnp.concatenate([caches["win_v"][l], tok(seg(5))], axis=1)[:, -w_buf:],
                 lru_h=h_new, lru_conv=jnp.concatenate([caches["lru_conv"], mseg("xl")[:, None]], axis=1)[:, 1:],
                 mlp_v=v_rows[:, None])
    return y, state


def _layer_weights(l, p):
    d_model = p["w_in"].shape[1]
    mlp_w = p["mlp_ln_g"].shape[1]
    lru_w = p["lru_lambda"].shape[1]
    names = ("norm_g", "q_norm_g", "k_norm_g", "mlp_ln_g", "mlp_ln_b", "w_s", "b_s", "conv_w", "conv_b",
             "lru_wa", "lru_ba", "lru_wx", "lru_bx", "lru_lambda")
    lw = {n: p[n][l] for n in names}
    lw["w_pack"] = _pack_w_in(p["w_in"], l, d_model, mlp_w, lru_w)
    lw["gain"], lw["flag"] = _norm_vectors(p["q_norm_g"][l], p["k_norm_g"][l])
    lw["cmp_k"] = _cmp_weights(p["w_cmp_k"][l], p["cmp_pe_k"][l])
    lw["cmp_v"] = _cmp_weights(p["w_cmp_v"][l], p["cmp_pe_v"][l])
    lw["w_br"] = [p[n] for n in ("w_br_nsa_bf", "w_br_mlp_bf", "w_br_lru_bf")]
    lw["w_out"] = p["w_out_bf"]
    return lw


def kernel(x_prompt, x_sample, cache_cmp_k, cache_cmp_v, cache_sel_k, cache_sel_v, state_win_k, state_win_v,
           state_lru_h, state_lru_conv, page_table, norm_g, w_in, q_norm_g, k_norm_g, cmp_pe_k, cmp_pe_v,
           w_cmp_k, w_cmp_v, mlp_ln_g, mlp_ln_b, w_s, b_s, conv_w, conv_b, lru_wa, lru_ba, lru_wx, lru_bx,
           lru_lambda, w_br_nsa, w_br_mlp, w_br_lru, w_out):
    params = dict(norm_g=norm_g, w_in=w_in, q_norm_g=q_norm_g, k_norm_g=k_norm_g, cmp_pe_k=cmp_pe_k,
                  cmp_pe_v=cmp_pe_v, w_cmp_k=w_cmp_k, w_cmp_v=w_cmp_v, mlp_ln_g=mlp_ln_g, mlp_ln_b=mlp_ln_b,
                  w_s=w_s, b_s=b_s, conv_w=conv_w, conv_b=conv_b, lru_wa=lru_wa, lru_ba=lru_ba, lru_wx=lru_wx,
                  lru_bx=lru_bx, lru_lambda=lru_lambda, w_br_nsa=w_br_nsa, w_br_mlp=w_br_mlp,
                  w_br_lru=w_br_lru, w_out=w_out)
    depth = w_in.shape[0]
    b, t, d = x_prompt.shape
    bs, ts, _ = x_sample.shape
    assert ts == 1
    w_buf = state_win_k.shape[2]
    past_len = page_table.shape[1] * cache_cmp_k.shape[2]
    yp = x_prompt.reshape(b * t, d)
    ys = x_sample.reshape(bs * ts, d)
    p_st, s_st = [], []
    for n in ("w_br_nsa", "w_br_mlp", "w_br_lru", "w_out"):
        params[n + "_bf"] = params[n].astype(BF16)
    for l in range(depth):
        lw = _layer_weights(l, params)
        z, zs = _in_proj(yp, ys, lw, ROW_TILE)
        yp, sp = prompt_layer(yp, z, lw, l, b, t, w_buf)
        caches = dict(cmp_k=cache_cmp_k, cmp_v=cache_cmp_v, sel_k=cache_sel_k, sel_v=cache_sel_v,
                      win_k=state_win_k, win_v=state_win_v, lru_h=state_lru_h[l],
                      lru_conv=state_lru_conv[l])
        ys, ss = sample_layer(ys, zs, lw, l, caches, page_table, past_len, w_buf)
        p_st.append(sp)
        s_st.append(ss)
    stk = lambda sts, name: jnp.stack([st[name] for st in sts])
    names = ("cmp_k", "cmp_v", "sel_k", "sel_v", "win_k", "win_v", "lru_h", "lru_conv")
    return ((yp.reshape(b, t, d), ys.reshape(bs, ts, d))
            + tuple(stk(p_st, n) for n in names)
            + tuple(stk(s_st, n) for n in names + ("mlp_v",)))
```
